```python
import jax, jax.numpy as jnp
from jax import lax
import numpy as np

D_MODEL = 1024
BATCH = 16
SEQ = 2048
DEPTH = 1

MEM_LEN = 256
MIX_WIDTH = D_MODEL
FOX_HEADS = 8
FOX_HEAD_DIM = (MIX_WIDTH // 2) // FOX_HEADS
FOX_WIDTH = FOX_HEADS * FOX_HEAD_DIM
GDN_HEADS = 4
GDN_HEAD_DIM = (MIX_WIDTH // 2) // GDN_HEADS
GDN_WIDTH = GDN_HEADS * GDN_HEAD_DIM
CONV_WIDTH = 4
GDN_CHUNK = 64
Q_BLOCK = 128
XATTN_HEADS = 4
XATTN_HEAD_DIM = 128
XATTN_WIDTH = XATTN_HEADS * XATTN_HEAD_DIM
D_FF = 4 * D_MODEL
EPS = 1e-6
NEG_INF = -1e30

IN_SIZES = [3 * FOX_WIDTH, FOX_HEADS, 3 * GDN_WIDTH, GDN_HEADS, GDN_HEADS, GDN_WIDTH]
IN_DIM = sum(IN_SIZES)
IN_OFFSETS = [int(o) for o in np.cumsum(IN_SIZES)[:-1]]

kernel_name = "hymba_fox_gdn_memxattn_block"


def rms_norm(x, g):
    xf = x.astype(jnp.float32)
    y = xf * lax.rsqrt(jnp.mean(xf * xf, axis=-1, keepdims=True) + EPS)
    return (y * g.astype(jnp.float32)).astype(x.dtype)


def l2_norm(x):
    xf = x.astype(jnp.float32)
    return (xf * lax.rsqrt(jnp.sum(xf * xf, axis=-1, keepdims=True) + EPS)).astype(x.dtype)


def causal_depthwise_conv(x, w):
    C = x.shape[-1]
    return lax.conv_general_dilated(
        x, w[:, None, :].astype(x.dtype), window_strides=(1,),
        padding=((CONV_WIDTH - 1, 0),),
        dimension_numbers=("NWC", "WIO", "NWC"),
        feature_group_count=C)


def forgetting_attention(q, k, v, f_logit):
    B, S, H, Dh = q.shape
    nb = S // Q_BLOCK
    scale = Dh ** -0.5
    c = jnp.cumsum(jax.nn.log_sigmoid(f_logit.astype(jnp.float32)), axis=1)
    c = c.transpose(0, 2, 1)
    q_blocks = q.reshape(B, nb, Q_BLOCK, H, Dh).transpose(1, 0, 3, 2, 4)
    c_blocks = c.reshape(B, H, nb, Q_BLOCK).transpose(2, 0, 1, 3)
    kpos = jnp.arange(S)

    def block(args):
        i, q_i, c_i = args
        qpos = i * Q_BLOCK + jnp.arange(Q_BLOCK)
        s = jnp.einsum("bhqd,bshd->bhqs", q_i, k,
                       preferred_element_type=jnp.float32) * scale
        s = s + c_i[..., :, None] - c[:, :, None, :]
        s = jnp.where(kpos[None, :] <= qpos[:, None], s, NEG_INF)
        p = jax.nn.softmax(s, axis=-1)
        return jnp.einsum("bhqs,bshd->bqhd", p.astype(v.dtype), v)

    out = lax.map(block, (jnp.arange(nb), q_blocks, c_blocks))
    return out.transpose(1, 0, 2, 3, 4).reshape(B, S, H, Dh)


def gated_delta_rule(q, k, v, g, beta):
    out_dtype = v.dtype
    B, S, H, Dk = q.shape
    Dv = v.shape[-1]
    C = GDN_CHUNK
    N = S // C
    f32 = jnp.float32

    def chunk(t):
        t = t.astype(f32).reshape((B, N, C, H) + t.shape[3:])
        return jnp.moveaxis(t, 3, 1)

    q = chunk(q) * (Dk ** -0.5)
    k = chunk(k)
    v = chunk(v)
    beta = chunk(beta)
    gc = jnp.cumsum(chunk(g), axis=-1)

    idx = jnp.arange(C)
    lower_incl = idx[:, None] >= idx[None, :]
    strict = idx[:, None] > idx[None, :]
    decay = jnp.exp(jnp.where(lower_incl, gc[..., :, None] - gc[..., None, :], NEG_INF))

    k_beta = k * beta[..., None]
    v_beta = v * beta[..., None]
    L = jnp.where(strict, jnp.einsum("bhncd,bhnsd->bhncs", k_beta, k) * decay, 0.0)
    eye = jnp.broadcast_to(jnp.eye(C, dtype=f32), L.shape)
    T = lax.linalg.triangular_solve(eye + L, eye, left_side=True, lower=True)
    u = jnp.einsum("bhncs,bhnse->bhnce", T, v_beta)
    w = jnp.einsum("bhncs,bhnsd->bhncd", T, k_beta * jnp.exp(gc)[..., None])
    intra = jnp.einsum("bhncd,bhnsd->bhncs", q, k) * decay

    def to_scan(t):
        return jnp.moveaxis(t, 2, 0)

    def step(state, xs):
        q_n, k_n, u_n, w_n, gc_n, intra_n = xs
        v_new = u_n - jnp.einsum("bhcd,bhde->bhce", w_n, state)
        o = (jnp.einsum("bhcd,bhde->bhce", q_n * jnp.exp(gc_n)[..., None], state)
             + jnp.einsum("bhcs,bhse->bhce", intra_n, v_new))
        g_last = gc_n[..., -1]
        state = (state * jnp.exp(g_last)[..., None, None]
                 + jnp.einsum("bhcd,bhce->bhde",
                              k_n * jnp.exp(g_last[..., None] - gc_n)[..., None], v_new))
        return state, o

    state0 = jnp.zeros((B, H, Dk, Dv), f32)
    _, o = lax.scan(step, state0, (to_scan(q), to_scan(k), to_scan(u), to_scan(w),
                                   to_scan(gc), to_scan(intra)))
    return o.transpose(1, 0, 3, 2, 4).reshape(B, S, H, Dv).astype(out_dtype)


def setup_inputs(seed: int = 0) -> dict:
    key = jax.random.key(seed)
    ks = jax.random.split(key, 26)
    L = DEPTH
    f32 = jnp.float32

    def nrm(k, shape, scale):
        return jax.random.normal(k, shape, f32) * scale

    def gain(k, dim):
        return 1.0 + 0.02 * jax.random.normal(k, (L, dim), f32)

    dt = jnp.exp(jax.random.uniform(ks[10], (L, GDN_HEADS), f32,
                                    minval=np.log(0.001), maxval=np.log(0.1)))
    return {
        "x": nrm(ks[0], (BATCH, SEQ, D_MODEL), 1.0),
        "mem": nrm(ks[1], (BATCH, MEM_LEN, D_MODEL), 1.0),
        "norm_mix_g": gain(ks[2], D_MODEL),
        "w_in": nrm(ks[3], (L, D_MODEL, IN_DIM), D_MODEL ** -0.5),
        "fox_qnorm_g": gain(ks[4], FOX_HEAD_DIM),
        "fox_knorm_g": gain(ks[5], FOX_HEAD_DIM),
        "fox_f_bias": 2.0 + 0.5 * jax.random.normal(ks[6], (L, FOX_HEADS), f32),
        "fox_onorm_g": gain(ks[7], FOX_HEAD_DIM),
        "gdn_conv_w": nrm(ks[8], (L, CONV_WIDTH, 3 * GDN_WIDTH), CONV_WIDTH ** -0.5),
        "gdn_A_log": jnp.log(jax.random.uniform(ks[9], (L, GDN_HEADS), f32, minval=1.0, maxval=16.0)),
        "gdn_dt_bias": dt + jnp.log(-jnp.expm1(-dt)),
        "gdn_onorm_g": gain(ks[11], GDN_HEAD_DIM),
        "w_out": nrm(ks[12], (L, MIX_WIDTH, D_MODEL), MIX_WIDTH ** -0.5),
        "norm_xattn_g": gain(ks[13], D_MODEL),
        "mem_norm_g": gain(ks[14], D_MODEL),
        "w_cq": nrm(ks[15], (L, D_MODEL, XATTN_WIDTH), D_MODEL ** -0.5),
        "w_ckv": nrm(ks[16], (L, D_MODEL, 2 * XATTN_WIDTH), D_MODEL ** -0.5),
        "xattn_qnorm_g": gain(ks[17], XATTN_HEAD_DIM),
        "xattn_knorm_g": gain(ks[18], XATTN_HEAD_DIM),
        "w_co": nrm(ks[19], (L, XATTN_WIDTH, D_MODEL), XATTN_WIDTH ** -0.5),
        "norm_mlp_g": gain(ks[20], D_MODEL),
        "w_mlp1": nrm(ks[21], (L, D_MODEL, D_FF), D_MODEL ** -0.5),
        "w_mlp2": nrm(ks[22], (L, D_FF, D_MODEL), D_FF ** -0.5),
    }


def reference(x, mem, norm_mix_g, w_in, fox_qnorm_g, fox_knorm_g, fox_f_bias, fox_onorm_g,
              gdn_conv_w, gdn_A_log, gdn_dt_bias, gdn_onorm_g, w_out,
              norm_xattn_g, mem_norm_g, w_cq, w_ckv, xattn_qnorm_g, xattn_knorm_g, w_co,
              norm_mlp_g, w_mlp1, w_mlp2):
    B, S, _ = x.shape
    M = mem.shape[1]
    for l in range(DEPTH):
        h = rms_norm(x, norm_mix_g[l])
        proj = h @ w_in[l]
        fox_qkv, fox_f, gdn_qkv, gdn_b, gdn_a, gdn_z = jnp.split(proj, IN_OFFSETS, axis=-1)

        fq, fk, fv = jnp.split(fox_qkv.reshape(B, S, 3 * FOX_HEADS, FOX_HEAD_DIM), 3, axis=2)
        fq = rms_norm(fq, fox_qnorm_g[l])
        fk = rms_norm(fk, fox_knorm_g[l])
        o_a = forgetting_attention(fq, fk, fv, fox_f + fox_f_bias[l])
        o_a = rms_norm(o_a, fox_onorm_g[l]).reshape(B, S, FOX_WIDTH)

        gqkv = jax.nn.silu(causal_depthwise_conv(gdn_qkv, gdn_conv_w[l]))
        gq, gk, gv = jnp.split(gqkv.reshape(B, S, 3 * GDN_HEADS, GDN_HEAD_DIM), 3, axis=2)
        gq = l2_norm(gq)
        gk = l2_norm(gk)
        beta = jax.nn.sigmoid(gdn_b.astype(jnp.float32))
        g = -jnp.exp(gdn_A_log[l].astype(jnp.float32)) * jax.nn.softplus(
            (gdn_a + gdn_dt_bias[l]).astype(jnp.float32))
        o_b = gated_delta_rule(gq, gk, gv, g, beta)
        o_b = rms_norm(o_b, gdn_onorm_g[l]) * jax.nn.silu(gdn_z.reshape(B, S, GDN_HEADS, GDN_HEAD_DIM))
        o_b = o_b.reshape(B, S, GDN_WIDTH)

        x = x + jnp.concatenate([o_a, o_b], axis=-1) @ w_out[l]

        hq = rms_norm(x, norm_xattn_g[l])
        hm = rms_norm(mem, mem_norm_g[l])
        cq = rms_norm((hq @ w_cq[l]).reshape(B, S, XATTN_HEADS, XATTN_HEAD_DIM), xattn_qnorm_g[l])
        ck, cv = jnp.split((hm @ w_ckv[l]).reshape(B, M, 2 * XATTN_HEADS, XATTN_HEAD_DIM), 2, axis=2)
        ck = rms_norm(ck, xattn_knorm_g[l])
        s = jnp.einsum("bqhd,bmhd->bhqm", cq, ck,
                       preferred_element_type=jnp.float32) * (XATTN_HEAD_DIM ** -0.5)
        p = jax.nn.softmax(s, axis=-1)
        co = jnp.einsum("bhqm,bmhd->bqhd", p.astype(cv.dtype), cv).reshape(B, S, XATTN_WIDTH)
        x = x + co @ w_co[l]

        hf = rms_norm(x, norm_mlp_g[l])
        x = x + jnp.square(jax.nn.relu(hf @ w_mlp1[l])) @ w_mlp2[l]
    return x
```

```python
import functools

import jax
import jax.numpy as jnp
from jax import lax
from jax.experimental import pallas as pl
from jax.experimental.pallas import tpu as pltpu

EPS = 1e-6
NEG_INF = -1e30

FOX_HEADS = 8
FOX_HEAD_DIM = 64
FOX_WIDTH = FOX_HEADS * FOX_HEAD_DIM
GDN_HEADS = 4
GDN_HEAD_DIM = 128
GDN_WIDTH = GDN_HEADS * GDN_HEAD_DIM
CONV_WIDTH = 4
GDN_CHUNK = 64
XATTN_HEADS = 4
XATTN_HEAD_DIM = 128
XATTN_WIDTH = XATTN_HEADS * XATTN_HEAD_DIM

LANES = 128
SUBLANES = 8
SMALL_ROWS = 16
BETA_ROW = FOX_HEADS
DECAY_ROW = FOX_HEADS + GDN_HEADS
VMEM_LIMIT_BYTES = 56 * 1024 * 1024

F32 = jnp.float32
BF16 = jnp.bfloat16


def _bdot(a, b):
    return jnp.dot(a.astype(BF16), b.astype(BF16), preferred_element_type=F32)


def _bdot_nt(a, b):
    return lax.dot_general(a.astype(BF16), b.astype(BF16), (((1,), (1,)), ((), ())),
                           preferred_element_type=F32)


def _rms(x, gain):
    return x * lax.rsqrt(jnp.mean(x * x, axis=-1, keepdims=True) + EPS) * gain


def _half_lane_rms(x, gain, lane_lo):
    ss = x * x
    s_lo = jnp.sum(jnp.where(lane_lo, ss, 0.0), axis=-1, keepdims=True)
    s_hi = jnp.sum(jnp.where(lane_lo, 0.0, ss), axis=-1, keepdims=True)
    ms = jnp.where(lane_lo, s_lo, s_hi) * (1.0 / FOX_HEAD_DIM)
    return x * lax.rsqrt(ms + EPS) * gain


def _softplus(y):
    return jnp.maximum(y, 0.0) + jnp.log1p(jnp.exp(-jnp.abs(y)))


def _silu(y):
    return y / (1.0 + jnp.exp(-y))


def _split3_bf16(v):
    p1 = v.astype(BF16)
    r1 = v - p1.astype(F32)
    p2 = r1.astype(BF16)
    r2 = r1 - p2.astype(F32)
    return p1, p2, r2.astype(BF16)


def _const_spec(shape):
    return pl.BlockSpec(shape, lambda *_: (0,) * len(shape), pipeline_mode=pl.Buffered(1))


def _params(n_grid):
    return pltpu.CompilerParams(dimension_semantics=("arbitrary",) * n_grid,
                                vmem_limit_bytes=VMEM_LIMIT_BYTES)


def _mem_kv_kernel(mem_ref, g_ref, w_ref, kg_ref, ck_ref, cv_ref):
    hm = _rms(mem_ref[0], g_ref[...])
    kv = _bdot(hm, w_ref[...])
    for h in range(XATTN_HEADS):
        sl = slice(h * XATTN_HEAD_DIM, (h + 1) * XATTN_HEAD_DIM)
        ck_ref[0, :, sl] = _rms(kv[:, sl], kg_ref[...]).astype(BF16)
    cv_ref[0] = kv[:, XATTN_WIDTH:].astype(BF16)


def _mem_kv(mem, g, w_ckv, kg):
    B, M, D = mem.shape
    return pl.pallas_call(
        _mem_kv_kernel,
        grid=(B,),
        in_specs=[pl.BlockSpec((1, M, D), lambda b: (b, 0, 0)),
                  _const_spec((1, D)), _const_spec(w_ckv.shape), _const_spec((1, XATTN_HEAD_DIM))],
        out_specs=[pl.BlockSpec((1, M, XATTN_WIDTH), lambda b: (b, 0, 0))] * 2,
        out_shape=[jax.ShapeDtypeStruct((B, M, XATTN_WIDTH), BF16)] * 2,
        compiler_params=_params(1),
        name="mem_kv",
    )(mem, g, w_ckv, kg)


def _in_proj_kernel(x_ref, g_ref, w_ref, qg_ref, kg_ref, convw_ref, bias_ref, alog_ref,
                    fq_ref, fk_ref, fv_ref, gq_ref, gk_ref, gv_ref, z_ref, scol_ref, srow_ref,
                    gbuf_ref, carry_ref, *, tiles_per_seq):
    tm = x_ref.shape[0]
    i = pl.program_id(0)

    @pl.when(i % tiles_per_seq == 0)
    def _():
        gbuf_ref[0:SUBLANES, :] = jnp.zeros((SUBLANES, 3 * GDN_WIDTH), F32)
        carry_ref[...] = jnp.zeros(carry_ref.shape, F32)

    hb = _rms(x_ref[...], g_ref[...]).astype(BF16)
    lane_lo = lax.broadcasted_iota(jnp.int32, (1, LANES), 1) < FOX_HEAD_DIM

    for j in range(FOX_WIDTH // LANES):
        sl = slice(j * LANES, (j + 1) * LANES)
        q = jnp.dot(hb, w_ref[:, j * LANES:(j + 1) * LANES], preferred_element_type=F32)
        fq_ref[:, sl] = _half_lane_rms(q, qg_ref[...], lane_lo).astype(BF16)
        k = jnp.dot(hb, w_ref[:, FOX_WIDTH + j * LANES:FOX_WIDTH + (j + 1) * LANES],
                    preferred_element_type=F32)
        fk_ref[:, sl] = _half_lane_rms(k, kg_ref[...], lane_lo).astype(BF16)
    fv_ref[...] = jnp.dot(hb, w_ref[:, 2 * FOX_WIDTH:3 * FOX_WIDTH],
                          preferred_element_type=F32).astype(BF16)

    off = 3 * FOX_WIDTH
    gbuf_ref[SUBLANES:SUBLANES + tm, :] = jnp.dot(
        hb, w_ref[:, off:off + 3 * GDN_WIDTH], preferred_element_type=F32)
    for part, out_ref in enumerate((gq_ref, gk_ref, gv_ref)):
        for h in range(GDN_HEADS):
            c0 = part * GDN_WIDTH + h * GDN_HEAD_DIM
            cs = slice(c0, c0 + GDN_HEAD_DIM)
            acc = convw_ref[CONV_WIDTH - 1:CONV_WIDTH, cs] * gbuf_ref[SUBLANES:SUBLANES + tm, cs]
            for d in range(1, CONV_WIDTH):
                acc = acc + (convw_ref[CONV_WIDTH - 1 - d:CONV_WIDTH - d, cs]
                             * gbuf_ref[SUBLANES - d:SUBLANES - d + tm, cs])
            y = _silu(acc)
            if part == 0:
                y = y * lax.rsqrt(jnp.sum(y * y, axis=-1, keepdims=True) + EPS) * (GDN_HEAD_DIM ** -0.5)
            elif part == 1:
                y = y * lax.rsqrt(jnp.sum(y * y, axis=-1, keepdims=True) + EPS)
            out_ref[:, h * GDN_HEAD_DIM:(h + 1) * GDN_HEAD_DIM] = y
    gbuf_ref[0:SUBLANES, :] = gbuf_ref[tm:tm + SUBLANES, :]

    off += 3 * GDN_WIDTH
    z_ref[...] = _silu(jnp.dot(hb, w_ref[:, off:off + GDN_WIDTH], preferred_element_type=F32))

    off += GDN_WIDTH
    ps = jnp.dot(hb, w_ref[:, off:off + LANES], preferred_element_type=F32)
    xb = ps.T[0:SMALL_ROWS, :] + bias_ref[...]
    row = lax.broadcasted_iota(jnp.int32, (SMALL_ROWS, tm), 0)
    is_fox = row < BETA_ROW
    sp = _softplus(jnp.where(is_fox, -xb, xb))
    val = jnp.where(is_fox, -sp,
                    jnp.where(row < DECAY_ROW, 1.0 / (1.0 + jnp.exp(-xb)), -jnp.exp(alog_ref[...]) * sp))
    ti = lax.broadcasted_iota(jnp.int32, (tm, tm), 0)
    tj = lax.broadcasted_iota(jnp.int32, (tm, tm), 1)
    u_full = ti <= tj
    u_chunk = u_full & ((ti // GDN_CHUNK) == (tj // GDN_CHUNK))
    u_full = jnp.where(u_full, 1.0, 0.0).astype(BF16)
    u_chunk = jnp.where(u_chunk, 1.0, 0.0).astype(BF16)
    cs_full = jnp.zeros((SMALL_ROWS, tm), F32)
    cs_chunk = jnp.zeros((SMALL_ROWS, tm), F32)
    for part in _split3_bf16(val):
        cs_full = cs_full + jnp.dot(part, u_full, preferred_element_type=F32)
        cs_chunk = cs_chunk + jnp.dot(part, u_chunk, preferred_element_type=F32)
    cs_full = cs_full + carry_ref[:, 0:1]
    carry_ref[...] = jnp.broadcast_to(cs_full[:, tm - 1:tm], carry_ref.shape)
    rows = jnp.where(is_fox, cs_full, jnp.where(row < DECAY_ROW, val, cs_chunk))
    srow_ref[0] = rows
    padded = jnp.concatenate([rows, jnp.zeros((LANES - SMALL_ROWS, tm), F32)], axis=0)
    scol_ref[...] = padded.T


def _in_proj(x2d, g, w_all, qg, kg, conv_w, bias_col, alog_col, *, seq, tm):
    T, D = x2d.shape
    n = T // tm
    row_spec = lambda width: pl.BlockSpec((tm, width), lambda i: (i, 0))
    kernel = functools.partial(_in_proj_kernel, tiles_per_seq=seq // tm)
    return pl.pallas_call(
        kernel,
        grid=(n,),
        in_specs=[row_spec(D), _const_spec((1, D)), _const_spec(w_all.shape),
                  _const_spec((1, LANES)), _const_spec((1, LANES)), _const_spec(conv_w.shape),
                  _const_spec((SMALL_ROWS, 1)), _const_spec((SMALL_ROWS, 1))],
        out_specs=[row_spec(FOX_WIDTH)] * 3 + [row_spec(GDN_WIDTH)] * 4
        + [row_spec(LANES), pl.BlockSpec((1, SMALL_ROWS, tm), lambda i: (i, 0, 0))],
        out_shape=[jax.ShapeDtypeStruct((T, FOX_WIDTH), BF16)] * 3
        + [jax.ShapeDtypeStruct((T, GDN_WIDTH), F32)] * 4
        + [jax.ShapeDtypeStruct((T, LANES), F32), jax.ShapeDtypeStruct((n, SMALL_ROWS, tm), F32)],
        scratch_shapes=[pltpu.VMEM((SUBLANES + tm, 3 * GDN_WIDTH), F32),
                        pltpu.VMEM((SMALL_ROWS, LANES), F32)],
        compiler_params=_params(1),
        name="in_proj",
    )(x2d, g, w_all, qg, kg, conv_w, bias_col, alog_col)


def _fox_kernel(q_ref, k_ref, v_ref, ccol_ref, crow_ref, og_ref, o_ref, *, tq, tk):
    hp = pl.program_id(1)
    qi = pl.program_id(2)
    lane = lax.broadcasted_iota(jnp.int32, (1, LANES), 1)
    lane_lo = lane < FOX_HEAD_DIM
    row16 = lax.broadcasted_iota(jnp.int32, (SMALL_ROWS, 1), 0)
    q = q_ref[...]
    ccol = ccol_ref[...]
    qpos = qi * tq + lax.broadcasted_iota(jnp.int32, (tq, 1), 0)
    n_kv = (qi * tq + tq - 1) // tk + 1
    out = None
    for hh in range(2):
        h = 2 * hp + hh
        in_head = lane_lo if hh == 0 else jnp.logical_not(lane_lo)
        qh = jnp.where(in_head, q, jnp.zeros_like(q))
        cq = jnp.sum(jnp.where(lane == h, ccol, 0.0), axis=-1, keepdims=True)

        def body(j, carry, qh=qh, cq=cq, h=h):
            m, l, acc = carry
            r0 = pl.multiple_of(j * tk, tk)
            kb = k_ref[pl.ds(r0, tk), :]
            vb = v_ref[pl.ds(r0, tk), :]
            ck = jnp.sum(jnp.where(row16 == h, crow_ref[j], 0.0), axis=0, keepdims=True)
            s = _bdot_nt(qh, kb) + (cq - ck)
            kpos = j * tk + lax.broadcasted_iota(jnp.int32, (1, tk), 1)
            s = jnp.where(kpos <= qpos, s, NEG_INF)
            m_new = jnp.maximum(m, jnp.max(s, axis=-1, keepdims=True))
            alpha = jnp.exp(m - m_new)
            p = jnp.exp(s - m_new)
            l = alpha * l + jnp.sum(p, axis=-1, keepdims=True)
            acc = alpha * acc + jnp.dot(p.astype(BF16), vb, preferred_element_type=F32)
            return m_new, l, acc

        init = (jnp.full((tq, 1), NEG_INF, F32), jnp.zeros((tq, 1), F32), jnp.zeros((tq, LANES), F32))
        _, l, acc = lax.fori_loop(0, n_kv, body, init)
        o_h = acc / l
        out = o_h if out is None else jnp.where(lane_lo, out, o_h)
    o_ref[...] = _half_lane_rms(out, og_ref[...], lane_lo).astype(BF16)


def _fox_attn(fq, fk, fv, scol, srow, og, *, batch, seq, tq, tk):
    T = fq.shape[0]
    nq = seq // tq
    nk = seq // tk
    kernel = functools.partial(_fox_kernel, tq=tq, tk=tk)
    kv_spec = pl.BlockSpec((seq, LANES), lambda b, hp, qi: (b, hp))
    return pl.pallas_call(
        kernel,
        grid=(batch, FOX_WIDTH // LANES, nq),
        in_specs=[pl.BlockSpec((tq, LANES), lambda b, hp, qi: (b * nq + qi, hp)),
                  kv_spec, kv_spec,
                  pl.BlockSpec((tq, LANES), lambda b, hp, qi: (b * nq + qi, 0)),
                  pl.BlockSpec((nk, SMALL_ROWS, tk), lambda b, hp, qi: (b, 0, 0)),
                  _const_spec((1, LANES))],
        out_specs=pl.BlockSpec((tq, LANES), lambda b, hp, qi: (b * nq + qi, hp)),
        out_shape=jax.ShapeDtypeStruct((T, FOX_WIDTH), BF16),
        compiler_params=_params(3),
        name="fox_attn",
    )(fq, fk, fv, scol, srow, og)


def _gdn_kernel(q_ref, k_ref, v_ref, z_ref, scol_ref, og_ref, o_ref, state_ref, *, n_chunks):
    C = GDN_CHUNK
    state_ref[...] = jnp.zeros(state_ref.shape, F32)
    ri = lax.broadcasted_iota(jnp.int32, (C, C), 0)
    ci = lax.broadcasted_iota(jnp.int32, (C, C), 1)
    lower = ri >= ci
    strict = ri > ci
    eye = jnp.where(ri == ci, 1.0, 0.0)

    def chunk_body(n, _):
        r0 = pl.multiple_of(n * C, C)
        rows = pl.ds(r0, C)
        sm = scol_ref[rows, :]
        sm_t = sm.T
        for h in range(GDN_HEADS):
            cs = slice(h * GDN_HEAD_DIM, (h + 1) * GDN_HEAD_DIM)
            q = q_ref[rows, cs]
            k = k_ref[rows, cs]
            v = v_ref[rows, cs]
            beta = sm[:, BETA_ROW + h:BETA_ROW + h + 1]
            gc = sm[:, DECAY_ROW + h:DECAY_ROW + h + 1]
            gc_row = sm_t[DECAY_ROW + h:DECAY_ROW + h + 1, :]
            g_last = gc[C - 1:C, :]
            eg = jnp.exp(gc)
            decay = jnp.exp(jnp.where(lower, gc - gc_row, NEG_INF))
            kb = k * beta
            a = _bdot_nt(jnp.concatenate([kb, q], axis=0), k)
            lmat = jnp.where(strict, a[0:C] * decay, 0.0)
            intra = a[C:2 * C] * decay
            tmat = eye - lmat
            pw = _bdot(lmat, lmat)
            for step in range(5):
                tmat = tmat + _bdot(tmat, pw)
                if step < 4:
                    pw = _bdot(pw, pw)
            uw = _bdot(tmat, jnp.concatenate([v * beta, kb * eg], axis=1))
            state = state_ref[h]
            ws = _bdot(jnp.concatenate([uw[:, GDN_HEAD_DIM:], q * eg], axis=0), state)
            v_new = uw[:, 0:GDN_HEAD_DIM] - ws[0:C]
            o = ws[C:2 * C] + _bdot(intra, v_new)
            kd = k * jnp.exp(g_last - gc)
            state_ref[h] = state * jnp.exp(g_last) + _bdot(kd.T, v_new)
            o_ref[rows, cs] = (_rms(o, og_ref[...]) * z_ref[rows, cs]).astype(BF16)
        return 0

    lax.fori_loop(0, n_chunks, chunk_body, 0)


def _gdn(gq, gk, gv, z, scol, og, *, batch, seq):
    T = gq.shape[0]
    kernel = functools.partial(_gdn_kernel, n_chunks=seq // GDN_CHUNK)
    seq_spec = lambda width: pl.BlockSpec((seq, width), lambda b: (b, 0))
    return pl.pallas_call(
        kernel,
        grid=(batch,),
        in_specs=[seq_spec(GDN_WIDTH)] * 4 + [seq_spec(LANES), _const_spec((1, GDN_HEAD_DIM))],
        out_specs=seq_spec(GDN_WIDTH),
        out_shape=jax.ShapeDtypeStruct((T, GDN_WIDTH), BF16),
        scratch_shapes=[pltpu.VMEM((GDN_HEADS, GDN_HEAD_DIM, GDN_HEAD_DIM), F32)],
        compiler_params=_params(1),
        name="gdn",
    )(gq, gk, gv, z, scol, og)


def _mix_out_kernel(x_ref, oa_ref, ob_ref, wout_ref, gx_ref, wcq_ref, cqg_ref, ck_ref, cv_ref,
                    wco_ref, o_ref):
    o = jnp.concatenate([oa_ref[...], ob_ref[...]], axis=-1)
    x1 = x_ref[...] + jnp.dot(o, wout_ref[...], preferred_element_type=F32)
    hq = _rms(x1, gx_ref[...])
    cq = _bdot(hq, wcq_ref[...])
    heads = []
    for h in range(XATTN_HEADS):
        sl = slice(h * XATTN_HEAD_DIM, (h + 1) * XATTN_HEAD_DIM)
        s = _bdot_nt(_rms(cq[:, sl], cqg_ref[...]), ck_ref[0, :, sl])
        p = jnp.exp(s - jnp.max(s, axis=-1, keepdims=True))
        pv = jnp.dot(p.astype(BF16), cv_ref[0, :, sl], preferred_element_type=F32)
        heads.append(pv / jnp.sum(p, axis=-1, keepdims=True))
    co = jnp.concatenate(heads, axis=-1)
    o_ref[...] = x1 + _bdot(co, wco_ref[...])


def _mix_out(x2d, oa, ob, w_out, gx, w_cq, cqg, ck, cv, w_co, *, seq, tm):
    T, D = x2d.shape
    M = ck.shape[1]
    tiles_per_seq = seq // tm
    row_spec = lambda width: pl.BlockSpec((tm, width), lambda i: (i, 0))
    mem_spec = pl.BlockSpec((1, M, XATTN_WIDTH), lambda i: (i // tiles_per_seq, 0, 0))
    return pl.pallas_call(
        _mix_out_kernel,
        grid=(T // tm,),
        in_specs=[row_spec(D), row_spec(FOX_WIDTH), row_spec(GDN_WIDTH), _const_spec(w_out.shape),
                  _const_spec((1, D)), _const_spec(w_cq.shape), _const_spec((1, XATTN_HEAD_DIM)),
                  mem_spec, mem_spec, _const_spec(w_co.shape)],
        out_specs=row_spec(D),
        out_shape=jax.ShapeDtypeStruct((T, D), F32),
        compiler_params=_params(1),
        name="mix_out",
    )(x2d, oa, ob, w_out, gx, w_cq, cqg, ck, cv, w_co)


def _mlp_kernel(x_ref, g_ref, w1_ref, w2_ref, o_ref, *, ff_block):
    x = x_ref[...]
    hb = _rms(x, g_ref[...]).astype(BF16)
    acc = x
    for c in range(w1_ref.shape[1] // ff_block):
        sl = slice(c * ff_block, (c + 1) * ff_block)
        a = jnp.maximum(jnp.dot(hb, w1_ref[:, sl], preferred_element_type=F32), 0.0)
        acc = acc + jnp.dot((a * a).astype(BF16), w2_ref[sl, :], preferred_element_type=F32)
    o_ref[...] = acc


def _mlp(x2d, g, w1, w2, *, tm, ff_block):
    T, D = x2d.shape
    row_spec = pl.BlockSpec((tm, D), lambda i: (i, 0))
    return pl.pallas_call(
        functools.partial(_mlp_kernel, ff_block=ff_block),
        grid=(T // tm,),
        in_specs=[row_spec, _const_spec((1, D)), _const_spec(w1.shape), _const_spec(w2.shape)],
        out_specs=row_spec,
        out_shape=jax.ShapeDtypeStruct((T, D), F32),
        compiler_params=_params(1),
        name="mlp",
    )(x2d, g, w1, w2)


def _row_tile(seq):
    return min(seq, 512)


def _layer(x, mem, norm_mix_g, w_in, fox_qnorm_g, fox_knorm_g, fox_f_bias, fox_onorm_g,
           gdn_conv_w, gdn_A_log, gdn_dt_bias, gdn_onorm_g, w_out,
           norm_xattn_g, mem_norm_g, w_cq, w_ckv, xattn_qnorm_g, xattn_knorm_g, w_co,
           norm_mlp_g, w_mlp1, w_mlp2):
    B, S, D = x.shape
    T = B * S
    tm = _row_tile(S)
    row = lambda v: v.reshape(1, -1).astype(F32)

    o_f = 3 * FOX_WIDTH
    o_g = o_f + FOX_HEADS
    o_b = o_g + 3 * GDN_WIDTH
    o_a = o_b + GDN_HEADS
    o_z = o_a + GDN_HEADS
    w_small = jnp.concatenate([w_in[:, o_f:o_g], w_in[:, o_b:o_z]], axis=1)
    w_small = jnp.pad(w_small, ((0, 0), (0, LANES - SMALL_ROWS)))
    w_all = jnp.concatenate([w_in[:, :o_f], w_in[:, o_g:o_b], w_in[:, o_z:], w_small], axis=1).astype(BF16)
    bias_col = jnp.concatenate([fox_f_bias, jnp.zeros((GDN_HEADS,), F32), gdn_dt_bias]).reshape(-1, 1)
    alog_col = jnp.concatenate([jnp.zeros((DECAY_ROW,), F32), gdn_A_log]).reshape(-1, 1)
    qg = row(jnp.tile(fox_qnorm_g, 2)) * (FOX_HEAD_DIM ** -0.5)
    kg = row(jnp.tile(fox_knorm_g, 2))
    og = row(jnp.tile(fox_onorm_g, 2))

    x2d = x.reshape(T, D)
    ck, cv = _mem_kv(mem, row(mem_norm_g), w_ckv.astype(BF16), row(xattn_knorm_g))
    fq, fk, fv, gq, gk, gv, z, scol, srow = _in_proj(
        x2d, row(norm_mix_g), w_all, qg, kg, gdn_conv_w, bias_col, alog_col, seq=S, tm=tm)
    o_fox = _fox_attn(fq, fk, fv, scol, srow, og, batch=B, seq=S, tq=min(S, 256), tk=tm)
    o_gdn = _gdn(gq, gk, gv, z, scol, row(gdn_onorm_g), batch=B, seq=S)
    x2 = _mix_out(x2d, o_fox, o_gdn, w_out.astype(BF16), row(norm_xattn_g), w_cq.astype(BF16),
                  row(xattn_qnorm_g) * (XATTN_HEAD_DIM ** -0.5), ck, cv, w_co.astype(BF16),
                  seq=S, tm=tm)
    out = _mlp(x2, row(norm_mlp_g), w_mlp1.astype(BF16), w_mlp2.astype(BF16), tm=tm, ff_block=1024)
    return out.reshape(B, S, D)


def kernel(x, mem, norm_mix_g, w_in, fox_qnorm_g, fox_knorm_g, fox_f_bias, fox_onorm_g, gdn_conv_w, gdn_A_log, gdn_dt_bias, gdn_onorm_g, w_out, norm_xattn_g, mem_norm_g, w_cq, w_ckv, xattn_qnorm_g, xattn_knorm_g, w_co, norm_mlp_g, w_mlp1, w_mlp2):
    for l in range(w_in.shape[0]):
        x = _layer(x, mem, norm_mix_g[l], w_in[l], fox_qnorm_g[l], fox_knorm_g[l], fox_f_bias[l],
                   fox_onorm_g[l], gdn_conv_w[l], gdn_A_log[l], gdn_dt_bias[l], gdn_onorm_g[l],
                   w_out[l], norm_xattn_g[l], mem_norm_g[l], w_cq[l], w_ckv[l], xattn_qnorm_g[l],
                   xattn_knorm_g[l], w_co[l], norm_mlp_g[l], w_mlp1[l], w_mlp2[l])
    return x
```

```python
import functools

import jax
import jax.numpy as jnp
from jax import lax
from jax.experimental import pallas as pl
from jax.experimental.pallas import tpu as pltpu

EPS = 1e-6
NEG_INF = -1e30

FOX_HEADS = 8
FOX_HEAD_DIM = 64
FOX_WIDTH = FOX_HEADS * FOX_HEAD_DIM
GDN_HEADS = 4
GDN_HEAD_DIM = 128
GDN_WIDTH = GDN_HEADS * GDN_HEAD_DIM
CONV_WIDTH = 4
GDN_CHUNK = 64
GDN_GROUP = 256
XATTN_HEADS = 4
XATTN_HEAD_DIM = 128
XATTN_WIDTH = XATTN_HEADS * XATTN_HEAD_DIM

LANES = 128
SUBLANES = 8
SMALL_ROWS = 24
BETA_ROW = FOX_HEADS
DECAY_ROW = BETA_ROW + GDN_HEADS
REM_ROW = DECAY_ROW + GDN_HEADS
VMEM_LIMIT_BYTES = 56 * 1024 * 1024

F32 = jnp.float32
BF16 = jnp.bfloat16


def _bdot(a, b):
    return jnp.dot(a.astype(BF16), b.astype(BF16), preferred_element_type=F32)


def _bdot_nt(a, b):
    return lax.dot_general(a.astype(BF16), b.astype(BF16), (((1,), (1,)), ((), ())),
                           preferred_element_type=F32)


def _rms(x, gain):
    return x * lax.rsqrt(jnp.mean(x * x, axis=-1, keepdims=True) + EPS) * gain


def _half_lane_rms(x, gain, lane_lo):
    ss = x * x
    s_lo = jnp.sum(jnp.where(lane_lo, ss, 0.0), axis=-1, keepdims=True)
    s_hi = jnp.sum(jnp.where(lane_lo, 0.0, ss), axis=-1, keepdims=True)
    ms = jnp.where(lane_lo, s_lo, s_hi) * (1.0 / FOX_HEAD_DIM)
    return x * lax.rsqrt(ms + EPS) * gain


def _softplus(y):
    return jnp.maximum(y, 0.0) + jnp.log1p(jnp.exp(-jnp.abs(y)))


def _silu(y):
    return y / (1.0 + jnp.exp(-y))


def _split3_bf16(v):
    p1 = v.astype(BF16)
    r1 = v - p1.astype(F32)
    p2 = r1.astype(BF16)
    r2 = r1 - p2.astype(F32)
    return p1, p2, r2.astype(BF16)


def _const_spec(shape):
    return pl.BlockSpec(shape, lambda *_: (0,) * len(shape), pipeline_mode=pl.Buffered(1))


def _params(n_grid):
    return pltpu.CompilerParams(dimension_semantics=("arbitrary",) * n_grid,
                                vmem_limit_bytes=VMEM_LIMIT_BYTES)


def _mem_kv_kernel(mem_ref, g_ref, w_ref, kg_ref, ck_ref, cv_ref):
    hm = _rms(mem_ref[0], g_ref[...])
    kv = _bdot(hm, w_ref[...])
    for h in range(XATTN_HEADS):
        sl = slice(h * XATTN_HEAD_DIM, (h + 1) * XATTN_HEAD_DIM)
        ck_ref[0, :, sl] = _rms(kv[:, sl], kg_ref[...]).astype(BF16)
    cv_ref[0] = kv[:, XATTN_WIDTH:].astype(BF16)


def _mem_kv(mem, g, w_ckv, kg):
    B, M, D = mem.shape
    return pl.pallas_call(
        _mem_kv_kernel,
        grid=(B,),
        in_specs=[pl.BlockSpec((1, M, D), lambda b: (b, 0, 0)),
                  _const_spec((1, D)), _const_spec(w_ckv.shape), _const_spec((1, XATTN_HEAD_DIM))],
        out_specs=[pl.BlockSpec((1, M, XATTN_WIDTH), lambda b: (b, 0, 0))] * 2,
        out_shape=[jax.ShapeDtypeStruct((B, M, XATTN_WIDTH), BF16)] * 2,
        compiler_params=_params(1),
        name="mem_kv",
    )(mem, g, w_ckv, kg)


def _in_proj_kernel(x_ref, g_ref, w_ref, qg_ref, kg_ref, convw_ref, bias_ref, alog_ref,
                    fq_ref, fk_ref, fv_ref, gq_ref, gk_ref, gv_ref, z_ref, scol_ref, srow_ref,
                    gbuf_ref, carry_ref, *, tiles_per_seq):
    tm = x_ref.shape[0]
    i = pl.program_id(0)

    @pl.when(i % tiles_per_seq == 0)
    def _():
        gbuf_ref[0:SUBLANES, :] = jnp.zeros((SUBLANES, 3 * GDN_WIDTH), F32)
        carry_ref[...] = jnp.zeros(carry_ref.shape, F32)

    hb = _rms(x_ref[...], g_ref[...]).astype(BF16)
    lane_lo = lax.broadcasted_iota(jnp.int32, (1, LANES), 1) < FOX_HEAD_DIM

    for j in range(FOX_WIDTH // LANES):
        sl = slice(j * LANES, (j + 1) * LANES)
        q = jnp.dot(hb, w_ref[:, j * LANES:(j + 1) * LANES], preferred_element_type=F32)
        fq_ref[:, sl] = _half_lane_rms(q, qg_ref[...], lane_lo).astype(BF16)
        k = jnp.dot(hb, w_ref[:, FOX_WIDTH + j * LANES:FOX_WIDTH + (j + 1) * LANES],
                    preferred_element_type=F32)
        fk_ref[:, sl] = _half_lane_rms(k, kg_ref[...], lane_lo).astype(BF16)
    fv_ref[...] = jnp.dot(hb, w_ref[:, 2 * FOX_WIDTH:3 * FOX_WIDTH],
                          preferred_element_type=F32).astype(BF16)

    off = 3 * FOX_WIDTH
    gbuf_ref[SUBLANES:SUBLANES + tm, :] = jnp.dot(
        hb, w_ref[:, off:off + 3 * GDN_WIDTH], preferred_element_type=F32)
    for part, out_ref in enumerate((gq_ref, gk_ref, gv_ref)):
        for h in range(GDN_HEADS):
            c0 = part * GDN_WIDTH + h * GDN_HEAD_DIM
            cs = slice(c0, c0 + GDN_HEAD_DIM)
            acc = convw_ref[CONV_WIDTH - 1:CONV_WIDTH, cs] * gbuf_ref[SUBLANES:SUBLANES + tm, cs]
            for d in range(1, CONV_WIDTH):
                acc = acc + (convw_ref[CONV_WIDTH - 1 - d:CONV_WIDTH - d, cs]
                             * gbuf_ref[SUBLANES - d:SUBLANES - d + tm, cs])
            y = _silu(acc)
            if part == 0:
                y = y * lax.rsqrt(jnp.sum(y * y, axis=-1, keepdims=True) + EPS) * (GDN_HEAD_DIM ** -0.5)
            elif part == 1:
                y = y * lax.rsqrt(jnp.sum(y * y, axis=-1, keepdims=True) + EPS)
            out_ref[:, h * GDN_HEAD_DIM:(h + 1) * GDN_HEAD_DIM] = y
    gbuf_ref[0:SUBLANES, :] = gbuf_ref[tm:tm + SUBLANES, :]

    off += 3 * GDN_WIDTH
    z_ref[...] = _silu(jnp.dot(hb, w_ref[:, off:off + GDN_WIDTH], preferred_element_type=F32))

    off += GDN_WIDTH
    ps = jnp.dot(hb, w_ref[:, off:off + LANES], preferred_element_type=F32)
    xb = ps.T[0:SMALL_ROWS, :] + bias_ref[...]
    row = lax.broadcasted_iota(jnp.int32, (SMALL_ROWS, tm), 0)
    is_fox = row < BETA_ROW
    is_beta = row < DECAY_ROW
    sp = _softplus(jnp.where(is_fox, -xb, xb))
    val = jnp.where(is_fox, -sp,
                    jnp.where(is_beta, 1.0 / (1.0 + jnp.exp(-xb)), -jnp.exp(alog_ref[...]) * sp))
    ti = lax.broadcasted_iota(jnp.int32, (tm, tm), 0)
    tj = lax.broadcasted_iota(jnp.int32, (tm, tm), 1)
    same_chunk = (ti // GDN_CHUNK) == (tj // GDN_CHUNK)
    one_hot = lambda m: jnp.where(m, 1.0, 0.0).astype(BF16)
    scan_mats = jnp.concatenate([one_hot(ti <= tj), one_hot((ti <= tj) & same_chunk),
                                 one_hot((ti > tj) & same_chunk)], axis=1)
    sums = jnp.zeros((SMALL_ROWS, 3 * tm), F32)
    for part in _split3_bf16(val):
        sums = sums + jnp.dot(part, scan_mats, preferred_element_type=F32)
    cs_full = sums[:, 0:tm] + carry_ref[:, 0:1]
    carry_ref[...] = jnp.broadcast_to(cs_full[:, tm - 1:tm], carry_ref.shape)
    rows = jnp.where(is_fox, cs_full,
                     jnp.where(is_beta, val, jnp.where(row < REM_ROW, sums[:, tm:2 * tm], sums[:, 2 * tm:])))
    srow_ref[0] = rows
    padded = jnp.concatenate([rows, jnp.zeros((LANES - SMALL_ROWS, tm), F32)], axis=0)
    scol_ref[...] = padded.T


def _in_proj(x2d, g, w_all, qg, kg, conv_w, bias_col, alog_col, *, seq, tm):
    T, D = x2d.shape
    n = T // tm
    row_spec = lambda width: pl.BlockSpec((tm, width), lambda i: (i, 0))
    kernel = functools.partial(_in_proj_kernel, tiles_per_seq=seq // tm)
    return pl.pallas_call(
        kernel,
        grid=(n,),
        in_specs=[row_spec(D), _const_spec((1, D)), _const_spec(w_all.shape),
                  _const_spec((1, LANES)), _const_spec((1, LANES)), _const_spec(conv_w.shape),
                  _const_spec((SMALL_ROWS, 1)), _const_spec((SMALL_ROWS, 1))],
        out_specs=[row_spec(FOX_WIDTH)] * 3 + [row_spec(GDN_WIDTH)] * 4
        + [row_spec(LANES), pl.BlockSpec((1, SMALL_ROWS, tm), lambda i: (i, 0, 0))],
        out_shape=[jax.ShapeDtypeStruct((T, FOX_WIDTH), BF16)] * 3
        + [jax.ShapeDtypeStruct((T, GDN_WIDTH), F32)] * 4
        + [jax.ShapeDtypeStruct((T, LANES), F32), jax.ShapeDtypeStruct((n, SMALL_ROWS, tm), F32)],
        scratch_shapes=[pltpu.VMEM((SUBLANES + tm, 3 * GDN_WIDTH), F32),
                        pltpu.VMEM((SMALL_ROWS, LANES), F32)],
        compiler_params=_params(1),
        name="in_proj",
    )(x2d, g, w_all, qg, kg, conv_w, bias_col, alog_col)


def _fox_kernel(q_ref, k_ref, v_ref, ccol_ref, crow_ref, og_ref, o_ref, *, tq, tk):
    hp = pl.program_id(1)
    qi = pl.program_id(2)
    lane = lax.broadcasted_iota(jnp.int32, (1, LANES), 1)
    lane_lo = lane < FOX_HEAD_DIM
    row16 = lax.broadcasted_iota(jnp.int32, (SMALL_ROWS, 1), 0)
    q = q_ref[...]
    ccol = ccol_ref[...]
    qpos = qi * tq + lax.broadcasted_iota(jnp.int32, (tq, 1), 0)
    n_kv = (qi * tq + tq - 1) // tk + 1
    out = None
    for hh in range(2):
        h = 2 * hp + hh
        in_head = lane_lo if hh == 0 else jnp.logical_not(lane_lo)
        qh = jnp.where(in_head, q, jnp.zeros_like(q))
        cq = jnp.sum(jnp.where(lane == h, ccol, 0.0), axis=-1, keepdims=True)

        def body(j, carry, qh=qh, cq=cq, h=h):
            m, l, acc = carry
            r0 = pl.multiple_of(j * tk, tk)
            kb = k_ref[pl.ds(r0, tk), :]
            vb = v_ref[pl.ds(r0, tk), :]
            ck = jnp.sum(jnp.where(row16 == h, crow_ref[j], 0.0), axis=0, keepdims=True)
            s = _bdot_nt(qh, kb) + (cq - ck)
            kpos = j * tk + lax.broadcasted_iota(jnp.int32, (1, tk), 1)
            s = jnp.where(kpos <= qpos, s, NEG_INF)
            m_new = jnp.maximum(m, jnp.max(s, axis=-1, keepdims=True))
            alpha = jnp.exp(m - m_new)
            p = jnp.exp(s - m_new)
            l = alpha * l + jnp.sum(p, axis=-1, keepdims=True)
            acc = alpha * acc + jnp.dot(p.astype(BF16), vb, preferred_element_type=F32)
            return m_new, l, acc

        init = (jnp.full((tq, 1), NEG_INF, F32), jnp.zeros((tq, 1), F32), jnp.zeros((tq, LANES), F32))
        _, l, acc = lax.fori_loop(0, n_kv, body, init)
        o_h = acc / l
        out = o_h if out is None else jnp.where(lane_lo, out, o_h)
    o_ref[...] = _half_lane_rms(out, og_ref[...], lane_lo).astype(BF16)


def _fox_attn(fq, fk, fv, scol, srow, og, *, batch, seq, tq, tk):
    T = fq.shape[0]
    nq = seq // tq
    nk = seq // tk
    kernel = functools.partial(_fox_kernel, tq=tq, tk=tk)
    kv_spec = pl.BlockSpec((seq, LANES), lambda b, hp, qi: (b, hp))
    return pl.pallas_call(
        kernel,
        grid=(batch, FOX_WIDTH // LANES, nq),
        in_specs=[pl.BlockSpec((tq, LANES), lambda b, hp, qi: (b * nq + qi, hp)),
                  kv_spec, kv_spec,
                  pl.BlockSpec((tq, LANES), lambda b, hp, qi: (b * nq + qi, 0)),
                  pl.BlockSpec((nk, SMALL_ROWS, tk), lambda b, hp, qi: (b, 0, 0)),
                  _const_spec((1, LANES))],
        out_specs=pl.BlockSpec((tq, LANES), lambda b, hp, qi: (b * nq + qi, hp)),
        out_shape=jax.ShapeDtypeStruct((T, FOX_WIDTH), BF16),
        compiler_params=_params(3),
        name="fox_attn",
    )(fq, fk, fv, scol, srow, og)


def _gdn_kernel(q_ref, k_ref, v_ref, z_ref, scol_ref, og_ref, o_ref, state_ref):
    C = GDN_CHUNK
    G = q_ref.shape[0]
    Dh = GDN_HEAD_DIM

    @pl.when(pl.program_id(1) == 0)
    def _():
        state_ref[...] = jnp.zeros(state_ref.shape, F32)

    ri = lax.broadcasted_iota(jnp.int32, (G, G), 0)
    ci = lax.broadcasted_iota(jnp.int32, (G, G), 1)
    same_chunk = (ri // C) == (ci // C)
    lower = same_chunk & (ri >= ci)
    strict = same_chunk & (ri > ci)
    sm = scol_ref[...]
    sm_t = sm.T
    heads = range(GDN_HEADS)
    col = lambda h: slice(h * Dh, (h + 1) * Dh)
    gcs, lmats, intras, ys, qes, kd_ts = [], [], [], [], [], []
    for h in heads:
        q = q_ref[:, col(h)]
        k = k_ref[:, col(h)]
        beta = sm[:, BETA_ROW + h:BETA_ROW + h + 1]
        gc = sm[:, DECAY_ROW + h:DECAY_ROW + h + 1]
        rem = sm[:, REM_ROW + h:REM_ROW + h + 1]
        gc_row = sm_t[DECAY_ROW + h:DECAY_ROW + h + 1, :]
        eg = jnp.exp(gc)
        decay = jnp.exp(jnp.where(lower, gc - gc_row, NEG_INF))
        kb = k * beta
        a = _bdot_nt(jnp.concatenate([kb, q], axis=0), k)
        lmats.append(jnp.where(strict, a[0:G] * decay, 0.0).astype(BF16))
        intra = (a[G:2 * G] * decay).astype(BF16)
        intra_fold = intra[:, 0:LANES]
        for j in range(1, G // LANES):
            intra_fold = intra_fold + intra[:, j * LANES:(j + 1) * LANES]
        intras.append(intra_fold)
        ys.append(jnp.concatenate([v_ref[:, col(h)] * beta, kb * eg], axis=1))
        qes.append((q * eg).astype(BF16))
        kd_ts.append((k * jnp.exp(rem)).T.astype(BF16))
        gcs.append(gc)
    ys = [ys[h] - _bdot(lmats[h], ys[h]) for h in heads]
    pws = [_bdot(lmats[h], lmats[h]).astype(BF16) for h in heads]
    for step in range(5):
        ys = [ys[h] + _bdot(pws[h], ys[h]) for h in heads]
        if step < 4:
            pws = [_bdot(pws[h], pws[h]).astype(BF16) for h in heads]
    us = [ys[h][:, 0:Dh] for h in heads]
    ws16 = [ys[h][:, Dh:2 * Dh].astype(BF16) for h in heads]
    states = [state_ref[h] for h in heads]
    outs = [[] for _ in heads]
    for c in range(G // C):
        rs = slice(c * C, (c + 1) * C)
        pair = (c * C) // LANES
        for h in heads:
            ws = _bdot(jnp.concatenate([ws16[h][rs], qes[h][rs]], axis=0), states[h])
            vb = (us[h][rs] - ws[0:C]).astype(BF16)
            zero = jnp.zeros_like(vb)
            outs[h].append(ws[C:2 * C] + jnp.dot(intras[h][rs], jnp.concatenate([vb, vb], axis=0),
                                                 preferred_element_type=F32))
            vpad = jnp.concatenate([vb, zero] if (c * C) % LANES == 0 else [zero, vb], axis=0)
            a_last = jnp.exp(gcs[h][(c + 1) * C - 1:(c + 1) * C, :])
            states[h] = states[h] * a_last + jnp.dot(kd_ts[h][:, pair * LANES:(pair + 1) * LANES], vpad,
                                                     preferred_element_type=F32)
    for h in heads:
        state_ref[h] = states[h]
        o = jnp.concatenate(outs[h], axis=0)
        o_ref[:, col(h)] = (_rms(o, og_ref[...]) * z_ref[:, col(h)]).astype(BF16)


def _gdn(gq, gk, gv, z, scol, og, *, batch, seq):
    T = gq.shape[0]
    G = min(seq, GDN_GROUP)
    ng = seq // G
    grp_spec = lambda width: pl.BlockSpec((G, width), lambda b, g: (b * ng + g, 0))
    return pl.pallas_call(
        _gdn_kernel,
        grid=(batch, ng),
        in_specs=[grp_spec(GDN_WIDTH)] * 4 + [grp_spec(LANES), _const_spec((1, GDN_HEAD_DIM))],
        out_specs=grp_spec(GDN_WIDTH),
        out_shape=jax.ShapeDtypeStruct((T, GDN_WIDTH), BF16),
        scratch_shapes=[pltpu.VMEM((GDN_HEADS, GDN_HEAD_DIM, GDN_HEAD_DIM), F32)],
        compiler_params=_params(2),
        name="gdn",
    )(gq, gk, gv, z, scol, og)


def _mix_out_kernel(x_ref, oa_ref, ob_ref, wout_ref, gx_ref, wcq_ref, cqg_ref, ck_ref, cv_ref,
                    wco_ref, o_ref):
    o = jnp.concatenate([oa_ref[...], ob_ref[...]], axis=-1)
    x1 = x_ref[...] + jnp.dot(o, wout_ref[...], preferred_element_type=F32)
    hq = _rms(x1, gx_ref[...])
    cq = _bdot(hq, wcq_ref[...])
    heads = []
    for h in range(XATTN_HEADS):
        sl = slice(h * XATTN_HEAD_DIM, (h + 1) * XATTN_HEAD_DIM)
        s = _bdot_nt(_rms(cq[:, sl], cqg_ref[...]), ck_ref[0, :, sl])
        p = jnp.exp(s - jnp.max(s, axis=-1, keepdims=True))
        pv = jnp.dot(p.astype(BF16), cv_ref[0, :, sl], preferred_element_type=F32)
        heads.append(pv / jnp.sum(p, axis=-1, keepdims=True))
    co = jnp.concatenate(heads, axis=-1)
    o_ref[...] = x1 + _bdot(co, wco_ref[...])


def _mix_out(x2d, oa, ob, w_out, gx, w_cq, cqg, ck, cv, w_co, *, seq, tm):
    T, D = x2d.shape
    M = ck.shape[1]
    tiles_per_seq = seq // tm
    row_spec = lambda width: pl.BlockSpec((tm, width), lambda i: (i, 0))
    mem_spec = pl.BlockSpec((1, M, XATTN_WIDTH), lambda i: (i // tiles_per_seq, 0, 0))
    return pl.pallas_call(
        _mix_out_kernel,
        grid=(T // tm,),
        in_specs=[row_spec(D), row_spec(FOX_WIDTH), row_spec(GDN_WIDTH), _const_spec(w_out.shape),
                  _const_spec((1, D)), _const_spec(w_cq.shape), _const_spec((1, XATTN_HEAD_DIM)),
                  mem_spec, mem_spec, _const_spec(w_co.shape)],
        out_specs=row_spec(D),
        out_shape=jax.ShapeDtypeStruct((T, D), F32),
        compiler_params=_params(1),
        name="mix_out",
    )(x2d, oa, ob, w_out, gx, w_cq, cqg, ck, cv, w_co)


def _mlp_kernel(x_ref, g_ref, w1_ref, w2_ref, o_ref, *, ff_block):
    x = x_ref[...]
    hb = _rms(x, g_ref[...]).astype(BF16)
    acc = x
    for c in range(w1_ref.shape[1] // ff_block):
        sl = slice(c * ff_block, (c + 1) * ff_block)
        a = jnp.maximum(jnp.dot(hb, w1_ref[:, sl], preferred_element_type=F32), 0.0)
        acc = acc + jnp.dot((a * a).astype(BF16), w2_ref[sl, :], preferred_element_type=F32)
    o_ref[...] = acc


def _mlp(x2d, g, w1, w2, *, tm, ff_block):
    T, D = x2d.shape
    row_spec = pl.BlockSpec((tm, D), lambda i: (i, 0))
    return pl.pallas_call(
        functools.partial(_mlp_kernel, ff_block=ff_block),
        grid=(T // tm,),
        in_specs=[row_spec, _const_spec((1, D)), _const_spec(w1.shape), _const_spec(w2.shape)],
        out_specs=row_spec,
        out_shape=jax.ShapeDtypeStruct((T, D), F32),
        compiler_params=_params(1),
        name="mlp",
    )(x2d, g, w1, w2)


def _row_tile(seq):
    return min(seq, 512)


def _layer(x, mem, norm_mix_g, w_in, fox_qnorm_g, fox_knorm_g, fox_f_bias, fox_onorm_g,
           gdn_conv_w, gdn_A_log, gdn_dt_bias, gdn_onorm_g, w_out,
           norm_xattn_g, mem_norm_g, w_cq, w_ckv, xattn_qnorm_g, xattn_knorm_g, w_co,
           norm_mlp_g, w_mlp1, w_mlp2):
    B, S, D = x.shape
    T = B * S
    tm = _row_tile(S)
    row = lambda v: v.reshape(1, -1).astype(F32)

    o_f = 3 * FOX_WIDTH
    o_g = o_f + FOX_HEADS
    o_b = o_g + 3 * GDN_WIDTH
    o_a = o_b + GDN_HEADS
    o_z = o_a + GDN_HEADS
    w_small = jnp.concatenate([w_in[:, o_f:o_g], w_in[:, o_b:o_z], w_in[:, o_a:o_z]], axis=1)
    w_small = jnp.pad(w_small, ((0, 0), (0, LANES - w_small.shape[1])))
    w_all = jnp.concatenate([w_in[:, :o_f], w_in[:, o_g:o_b], w_in[:, o_z:], w_small], axis=1).astype(BF16)
    pad4 = jnp.zeros((GDN_HEADS,), F32)
    bias_col = jnp.concatenate([fox_f_bias, pad4, gdn_dt_bias, gdn_dt_bias, pad4]).reshape(-1, 1)
    alog_col = jnp.concatenate([jnp.zeros((DECAY_ROW,), F32), gdn_A_log, gdn_A_log, pad4]).reshape(-1, 1)
    qg = row(jnp.tile(fox_qnorm_g, 2)) * (FOX_HEAD_DIM ** -0.5)
    kg = row(jnp.tile(fox_knorm_g, 2))
    og = row(jnp.tile(fox_onorm_g, 2))

    x2d = x.reshape(T, D)
    ck, cv = _mem_kv(mem, row(mem_norm_g), w_ckv.astype(BF16), row(xattn_knorm_g))
    fq, fk, fv, gq, gk, gv, z, scol, srow = _in_proj(
        x2d, row(norm_mix_g), w_all, qg, kg, gdn_conv_w, bias_col, alog_col, seq=S, tm=tm)
    o_fox = _fox_attn(fq, fk, fv, scol, srow, og, batch=B, seq=S, tq=min(S, 256), tk=tm)
    o_gdn = _gdn(gq, gk, gv, z, scol, row(gdn_onorm_g), batch=B, seq=S)
    x2 = _mix_out(x2d, o_fox, o_gdn, w_out.astype(BF16), row(norm_xattn_g), w_cq.astype(BF16),
                  row(xattn_qnorm_g) * (XATTN_HEAD_DIM ** -0.5), ck, cv, w_co.astype(BF16),
                  seq=S, tm=tm)
    out = _mlp(x2, row(norm_mlp_g), w_mlp1.astype(BF16), w_mlp2.astype(BF16), tm=tm, ff_block=1024)
    return out.reshape(B, S, D)


def kernel(x, mem, norm_mix_g, w_in, fox_qnorm_g, fox_knorm_g, fox_f_bias, fox_onorm_g, gdn_conv_w, gdn_A_log, gdn_dt_bias, gdn_onorm_g, w_out, norm_xattn_g, mem_norm_g, w_cq, w_ckv, xattn_qnorm_g, xattn_knorm_g, w_co, norm_mlp_g, w_mlp1, w_mlp2):
    for l in range(w_in.shape[0]):
        x = _layer(x, mem, norm_mix_g[l], w_in[l], fox_qnorm_g[l], fox_knorm_g[l], fox_f_bias[l],
                   fox_onorm_g[l], gdn_conv_w[l], gdn_A_log[l], gdn_dt_bias[l], gdn_onorm_g[l],
                   w_out[l], norm_xattn_g[l], mem_norm_g[l], w_cq[l], w_ckv[l], xattn_qnorm_g[l],
                   xattn_knorm_g[l], w_co[l], norm_mlp_g[l], w_mlp1[l], w_mlp2[l])
    return x
```

```python
import functools

import jax
import jax.numpy as jnp
from jax import lax
from jax.experimental import pallas as pl
from jax.experimental.pallas import tpu as pltpu

EPS = 1e-6
NEG_INF = -1e30
LOG2E = 1.4426950408889634

FOX_HEADS = 8
FOX_HEAD_DIM = 64
FOX_WIDTH = FOX_HEADS * FOX_HEAD_DIM
FOX_BLOCK = 256
FOX_PAIRS_PER_STEP = 4
GDN_HEADS = 4
GDN_HEAD_DIM = 128
GDN_WIDTH = GDN_HEADS * GDN_HEAD_DIM
CONV_WIDTH = 4
GDN_CHUNK = 64
GDN_GROUP = 256
XATTN_HEADS = 4
XATTN_HEAD_DIM = 128
XATTN_WIDTH = XATTN_HEADS * XATTN_HEAD_DIM

LANES = 128
SUBLANES = 8
SMALL_ROWS = 24
BETA_ROW = FOX_HEADS
DECAY_ROW = BETA_ROW + GDN_HEADS
REM_ROW = DECAY_ROW + GDN_HEADS
VMEM_LIMIT_BYTES = 56 * 1024 * 1024

F32 = jnp.float32
BF16 = jnp.bfloat16


def _bdot(a, b):
    return jnp.dot(a.astype(BF16), b.astype(BF16), preferred_element_type=F32)


def _bdot_nt(a, b):
    return lax.dot_general(a.astype(BF16), b.astype(BF16), (((1,), (1,)), ((), ())),
                           preferred_element_type=F32)


def _rms(x, gain):
    return x * lax.rsqrt(jnp.mean(x * x, axis=-1, keepdims=True) + EPS) * gain


def _half_lane_rms(x, gain, lane_lo):
    ss = x * x
    s_lo = jnp.sum(jnp.where(lane_lo, ss, 0.0), axis=-1, keepdims=True)
    s_hi = jnp.sum(jnp.where(lane_lo, 0.0, ss), axis=-1, keepdims=True)
    ms = jnp.where(lane_lo, s_lo, s_hi) * (1.0 / FOX_HEAD_DIM)
    return x * lax.rsqrt(ms + EPS) * gain


def _softplus(y):
    return jnp.maximum(y, 0.0) + jnp.log1p(jnp.exp(-jnp.abs(y)))


def _silu(y):
    return y / (1.0 + jnp.exp(-y))


def _split3_bf16(v):
    p1 = v.astype(BF16)
    r1 = v - p1.astype(F32)
    p2 = r1.astype(BF16)
    r2 = r1 - p2.astype(F32)
    return p1, p2, r2.astype(BF16)


def _const_spec(shape):
    return pl.BlockSpec(shape, lambda *_: (0,) * len(shape), pipeline_mode=pl.Buffered(1))


def _params(n_grid):
    return pltpu.CompilerParams(dimension_semantics=("arbitrary",) * n_grid,
                                vmem_limit_bytes=VMEM_LIMIT_BYTES)


def _mem_kv_kernel(mem_ref, g_ref, w_ref, kg_ref, ck_ref, cv_ref):
    hm = _rms(mem_ref[0], g_ref[...])
    kv = _bdot(hm, w_ref[...])
    for h in range(XATTN_HEADS):
        sl = slice(h * XATTN_HEAD_DIM, (h + 1) * XATTN_HEAD_DIM)
        ck_ref[0, :, sl] = _rms(kv[:, sl], kg_ref[...]).astype(BF16)
    cv_ref[0] = kv[:, XATTN_WIDTH:].astype(BF16)


def _mem_kv(mem, g, w_ckv, kg):
    B, M, D = mem.shape
    return pl.pallas_call(
        _mem_kv_kernel,
        grid=(B,),
        in_specs=[pl.BlockSpec((1, M, D), lambda b: (b, 0, 0)),
                  _const_spec((1, D)), _const_spec(w_ckv.shape), _const_spec((1, XATTN_HEAD_DIM))],
        out_specs=[pl.BlockSpec((1, M, XATTN_WIDTH), lambda b: (b, 0, 0))] * 2,
        out_shape=[jax.ShapeDtypeStruct((B, M, XATTN_WIDTH), BF16)] * 2,
        compiler_params=_params(1),
        name="mem_kv",
    )(mem, g, w_ckv, kg)


def _in_proj_kernel(x_ref, g_ref, w_ref, qg_ref, kg_ref, convw_ref, bias_ref, alog_ref,
                    fq_ref, fk_ref, fvt_ref, gq_ref, gk_ref, gv_ref, z_ref, scol_ref, kx_ref, qx_ref,
                    gbuf_ref, carry_ref, *, tiles_per_seq):
    tm = x_ref.shape[0]
    i = pl.program_id(0)

    @pl.when(i % tiles_per_seq == 0)
    def _():
        gbuf_ref[0:SUBLANES, :] = jnp.zeros((SUBLANES, 3 * GDN_WIDTH), F32)
        carry_ref[...] = jnp.zeros(carry_ref.shape, F32)

    hb = _rms(x_ref[...], g_ref[...]).astype(BF16)
    lane_lo = lax.broadcasted_iota(jnp.int32, (1, LANES), 1) < FOX_HEAD_DIM

    for j in range(FOX_WIDTH // LANES):
        sl = slice(j * LANES, (j + 1) * LANES)
        q = jnp.dot(hb, w_ref[:, j * LANES:(j + 1) * LANES], preferred_element_type=F32)
        fq_ref[:, sl] = _half_lane_rms(q, qg_ref[...], lane_lo).astype(BF16)
        k = jnp.dot(hb, w_ref[:, FOX_WIDTH + j * LANES:FOX_WIDTH + (j + 1) * LANES],
                    preferred_element_type=F32)
        fk_ref[:, sl] = _half_lane_rms(k, kg_ref[...], lane_lo).astype(BF16)
    fv = jnp.dot(hb, w_ref[:, 2 * FOX_WIDTH:3 * FOX_WIDTH], preferred_element_type=F32)
    for r in range(tm // FOX_BLOCK):
        fvt_ref[r] = fv[r * FOX_BLOCK:(r + 1) * FOX_BLOCK, :].T.astype(BF16)

    off = 3 * FOX_WIDTH
    gbuf_ref[SUBLANES:SUBLANES + tm, :] = jnp.dot(
        hb, w_ref[:, off:off + 3 * GDN_WIDTH], preferred_element_type=F32)
    for part, out_ref in enumerate((gq_ref, gk_ref, gv_ref)):
        for h in range(GDN_HEADS):
            c0 = part * GDN_WIDTH + h * GDN_HEAD_DIM
            cs = slice(c0, c0 + GDN_HEAD_DIM)
            acc = convw_ref[CONV_WIDTH - 1:CONV_WIDTH, cs] * gbuf_ref[SUBLANES:SUBLANES + tm, cs]
            for d in range(1, CONV_WIDTH):
                acc = acc + (convw_ref[CONV_WIDTH - 1 - d:CONV_WIDTH - d, cs]
                             * gbuf_ref[SUBLANES - d:SUBLANES - d + tm, cs])
            y = _silu(acc)
            if part == 0:
                y = y * lax.rsqrt(jnp.sum(y * y, axis=-1, keepdims=True) + EPS) * (GDN_HEAD_DIM ** -0.5)
            elif part == 1:
                y = y * lax.rsqrt(jnp.sum(y * y, axis=-1, keepdims=True) + EPS)
            out_ref[:, h * GDN_HEAD_DIM:(h + 1) * GDN_HEAD_DIM] = y
    gbuf_ref[0:SUBLANES, :] = gbuf_ref[tm:tm + SUBLANES, :]

    off += 3 * GDN_WIDTH
    z_ref[...] = _silu(jnp.dot(hb, w_ref[:, off:off + GDN_WIDTH], preferred_element_type=F32))

    off += GDN_WIDTH
    ps = jnp.dot(hb, w_ref[:, off:off + LANES], preferred_element_type=F32)
    xb = ps.T[0:SMALL_ROWS, :] + bias_ref[...]
    row = lax.broadcasted_iota(jnp.int32, (SMALL_ROWS, tm), 0)
    is_fox = row < BETA_ROW
    is_beta = row < DECAY_ROW
    sp = _softplus(jnp.where(is_fox, -xb, xb))
    val = jnp.where(is_fox, -sp,
                    jnp.where(is_beta, 1.0 / (1.0 + jnp.exp(-xb)), -jnp.exp(alog_ref[...]) * sp))
    ti = lax.broadcasted_iota(jnp.int32, (tm, tm), 0)
    tj = lax.broadcasted_iota(jnp.int32, (tm, tm), 1)
    same_chunk = (ti // GDN_CHUNK) == (tj // GDN_CHUNK)
    one_hot = lambda m: jnp.where(m, 1.0, 0.0).astype(BF16)
    scan_mats = jnp.concatenate([one_hot(ti <= tj), one_hot((ti <= tj) & same_chunk),
                                 one_hot((ti > tj) & same_chunk)], axis=1)
    sums = jnp.zeros((SMALL_ROWS, 3 * tm), F32)
    for part in _split3_bf16(val):
        sums = sums + jnp.dot(part, scan_mats, preferred_element_type=F32)
    cs_full = sums[:, 0:tm] + carry_ref[:, 0:1]
    carry_ref[...] = jnp.broadcast_to(cs_full[:, tm - 1:tm], carry_ref.shape)
    rows = jnp.where(is_fox, cs_full,
                     jnp.where(is_beta, val, jnp.where(row < REM_ROW, sums[:, tm:2 * tm], sums[:, 2 * tm:])))
    padded = jnp.concatenate([rows, jnp.zeros((LANES - SMALL_ROWS, tm), F32)], axis=0)
    scol_ref[...] = padded.T
    c_parts = [p.astype(F32) for p in _split3_bf16(rows[0:FOX_HEADS, :] * LOG2E)]
    zeros = lambda n: jnp.zeros((n, tm), F32)
    ones = jnp.ones((3 * FOX_HEADS, tm), F32)
    kx = jnp.concatenate([-p for p in c_parts] + [zeros(FOX_HEADS), ones, zeros(LANES - 7 * FOX_HEADS)], axis=0)
    qx = jnp.concatenate([ones, zeros(FOX_HEADS)] + c_parts + [zeros(LANES - 7 * FOX_HEADS)], axis=0)
    kx_ref[...] = kx.T.astype(BF16)
    qx_ref[...] = qx.T.astype(BF16)


def _in_proj(x2d, g, w_all, qg, kg, conv_w, bias_col, alog_col, *, seq, tm):
    T, D = x2d.shape
    n = T // tm
    row_spec = lambda width: pl.BlockSpec((tm, width), lambda i: (i, 0))
    kernel = functools.partial(_in_proj_kernel, tiles_per_seq=seq // tm)
    return pl.pallas_call(
        kernel,
        grid=(n,),
        in_specs=[row_spec(D), _const_spec((1, D)), _const_spec(w_all.shape),
                  _const_spec((1, LANES)), _const_spec((1, LANES)), _const_spec(conv_w.shape),
                  _const_spec((SMALL_ROWS, 1)), _const_spec((SMALL_ROWS, 1))],
        out_specs=[row_spec(FOX_WIDTH)] * 2
        + [pl.BlockSpec((tm // FOX_BLOCK, FOX_WIDTH, FOX_BLOCK), lambda i: (i, 0, 0))]
        + [row_spec(GDN_WIDTH)] * 4 + [row_spec(LANES)] * 3,
        out_shape=[jax.ShapeDtypeStruct((T, FOX_WIDTH), BF16)] * 2
        + [jax.ShapeDtypeStruct((T // FOX_BLOCK, FOX_WIDTH, FOX_BLOCK), BF16)]
        + [jax.ShapeDtypeStruct((T, GDN_WIDTH), F32)] * 4
        + [jax.ShapeDtypeStruct((T, LANES), F32)] + [jax.ShapeDtypeStruct((T, LANES), BF16)] * 2,
        scratch_shapes=[pltpu.VMEM((SUBLANES + tm, 3 * GDN_WIDTH), F32),
                        pltpu.VMEM((SMALL_ROWS, LANES), F32)],
        compiler_params=_params(1),
        name="in_proj",
    )(x2d, g, w_all, qg, kg, conv_w, bias_col, alog_col)


def _fox_kernel(q_ref, qx_ref, k_ref, kx_ref, vt_ref, og_ref, o_ref, *, pairs):
    t = q_ref.shape[0]
    Dh = FOX_HEAD_DIM
    qi = pl.program_id(2)
    lane = lax.broadcasted_iota(jnp.int32, (1, LANES), 1)
    lane_lo = lane < Dh
    in_bias_lanes = (lane < 3 * FOX_HEADS) | ((lane >= 4 * FOX_HEADS) & (lane < 7 * FOX_HEADS))
    qx = qx_ref[...]
    chains = [(p, hh) for p in range(pairs) for hh in range(2)]
    q_aug = []
    for p, hh in chains:
        h = (pl.program_id(1) * pairs + p) * 2 + hh
        q = q_ref[:, p * LANES:(p + 1) * LANES]
        q = jnp.where(lane_lo if hh == 0 else jnp.logical_not(lane_lo), q, jnp.zeros_like(q))
        qx_h = jnp.where(in_bias_lanes & (lane % FOX_HEADS == h), qx, jnp.zeros_like(qx))
        q_aug.append(jnp.concatenate([q, qx_h], axis=1))

    def block(j, carry, masked):
        r0 = pl.multiple_of(j * t, t)
        kx = kx_ref[pl.ds(r0, t), :]
        k_aug = [jnp.concatenate([k_ref[pl.ds(r0, t), p * LANES:(p + 1) * LANES], kx], axis=1)
                 for p in range(pairs)]
        s = [_bdot_nt(k_aug[p], q_aug[i]) for i, (p, hh) in enumerate(chains)]
        if masked:
            visible = (lax.broadcasted_iota(jnp.int32, (t, t), 0)
                       <= lax.broadcasted_iota(jnp.int32, (t, t), 1))
            s = [jnp.where(visible, x, NEG_INF) for x in s]
        m_new = [jnp.maximum(c[0], jnp.max(x, axis=0, keepdims=True)) for c, x in zip(carry, s)]
        out = []
        for i, (p, hh) in enumerate(chains):
            m, l, acc = carry[i]
            alpha = jnp.exp2(m - m_new[i])
            prob = jnp.exp2(s[i] - m_new[i])
            r = (p * 2 + hh) * Dh
            pv = jnp.dot(vt_ref[j, r:r + Dh, :], prob.astype(BF16), preferred_element_type=F32)
            out.append((m_new[i], alpha * l + jnp.sum(prob, axis=0, keepdims=True), alpha * acc + pv))
        return tuple(out)

    init = tuple((jnp.full((1, t), NEG_INF, F32), jnp.zeros((1, t), F32), jnp.zeros((Dh, t), F32))
                 for _ in chains)
    carry = lax.fori_loop(0, qi, functools.partial(block, masked=False), init)
    carry = block(qi, carry, masked=True)
    for p in range(pairs):
        (_, l0, a0), (_, l1, a1) = carry[2 * p], carry[2 * p + 1]
        o = jnp.concatenate([a0 / l0, a1 / l1], axis=0).T
        o_ref[:, p * LANES:(p + 1) * LANES] = _half_lane_rms(o, og_ref[...], lane_lo).astype(BF16)


def _fox_attn(fq, qx, fk, kx, fvt, og, *, batch, seq):
    T = fq.shape[0]
    t = FOX_BLOCK
    nq = seq // t
    pairs = FOX_PAIRS_PER_STEP
    width = pairs * LANES
    q_spec = lambda w: pl.BlockSpec((t, w), lambda b, g, qi: (b * nq + qi, g))
    return pl.pallas_call(
        functools.partial(_fox_kernel, pairs=pairs),
        grid=(batch, FOX_WIDTH // width, nq),
        in_specs=[q_spec(width),
                  pl.BlockSpec((t, LANES), lambda b, g, qi: (b * nq + qi, 0)),
                  pl.BlockSpec((seq, width), lambda b, g, qi: (b, g)),
                  pl.BlockSpec((seq, LANES), lambda b, g, qi: (b, 0)),
                  pl.BlockSpec((nq, width, t), lambda b, g, qi: (b, g, 0)),
                  _const_spec((1, LANES))],
        out_specs=q_spec(width),
        out_shape=jax.ShapeDtypeStruct((T, FOX_WIDTH), BF16),
        compiler_params=_params(3),
        name="fox_attn",
    )(fq, qx, fk, kx, fvt, og)


def _gdn_kernel(q_ref, k_ref, v_ref, z_ref, scol_ref, og_ref, o_ref, state_ref):
    C = GDN_CHUNK
    G = q_ref.shape[0]
    Dh = GDN_HEAD_DIM

    @pl.when(pl.program_id(1) == 0)
    def _():
        state_ref[...] = jnp.zeros(state_ref.shape, F32)

    ri = lax.broadcasted_iota(jnp.int32, (G, G), 0)
    ci = lax.broadcasted_iota(jnp.int32, (G, G), 1)
    same_chunk = (ri // C) == (ci // C)
    lower = same_chunk & (ri >= ci)
    strict = same_chunk & (ri > ci)
    sm = scol_ref[...]
    sm_t = sm.T
    heads = range(GDN_HEADS)
    col = lambda h: slice(h * Dh, (h + 1) * Dh)
    gcs, lmats, intras, ys, qes, kd_ts = [], [], [], [], [], []
    for h in heads:
        q = q_ref[:, col(h)]
        k = k_ref[:, col(h)]
        beta = sm[:, BETA_ROW + h:BETA_ROW + h + 1]
        gc = sm[:, DECAY_ROW + h:DECAY_ROW + h + 1]
        rem = sm[:, REM_ROW + h:REM_ROW + h + 1]
        gc_row = sm_t[DECAY_ROW + h:DECAY_ROW + h + 1, :]
        eg = jnp.exp(gc)
        decay = jnp.exp(jnp.where(lower, gc - gc_row, NEG_INF))
        kb = k * beta
        a = _bdot_nt(jnp.concatenate([kb, q], axis=0), k)
        lmats.append(jnp.where(strict, a[0:G] * decay, 0.0).astype(BF16))
        intra = (a[G:2 * G] * decay).astype(BF16)
        intra_fold = intra[:, 0:LANES]
        for j in range(1, G // LANES):
            intra_fold = intra_fold + intra[:, j * LANES:(j + 1) * LANES]
        intras.append(intra_fold)
        ys.append(jnp.concatenate([v_ref[:, col(h)] * beta, kb * eg], axis=1))
        qes.append((q * eg).astype(BF16))
        kd_ts.append((k * jnp.exp(rem)).T.astype(BF16))
        gcs.append(gc)
    ys = [ys[h] - _bdot(lmats[h], ys[h]) for h in heads]
    pws = [_bdot(lmats[h], lmats[h]).astype(BF16) for h in heads]
    for step in range(5):
        ys = [ys[h] + _bdot(pws[h], ys[h]) for h in heads]
        if step < 4:
            pws = [_bdot(pws[h], pws[h]).astype(BF16) for h in heads]
    us = [ys[h][:, 0:Dh] for h in heads]
    ws16 = [ys[h][:, Dh:2 * Dh].astype(BF16) for h in heads]
    states = [state_ref[h] for h in heads]
    outs = [[] for _ in heads]
    for c in range(G // C):
        rs = slice(c * C, (c + 1) * C)
        pair = (c * C) // LANES
        for h in heads:
            ws = _bdot(jnp.concatenate([ws16[h][rs], qes[h][rs]], axis=0), states[h])
            vb = (us[h][rs] - ws[0:C]).astype(BF16)
            zero = jnp.zeros_like(vb)
            outs[h].append(ws[C:2 * C] + jnp.dot(intras[h][rs], jnp.concatenate([vb, vb], axis=0),
                                                 preferred_element_type=F32))
            vpad = jnp.concatenate([vb, zero] if (c * C) % LANES == 0 else [zero, vb], axis=0)
            a_last = jnp.exp(gcs[h][(c + 1) * C - 1:(c + 1) * C, :])
            states[h] = states[h] * a_last + jnp.dot(kd_ts[h][:, pair * LANES:(pair + 1) * LANES], vpad,
                                                     preferred_element_type=F32)
    for h in heads:
        state_ref[h] = states[h]
        o = jnp.concatenate(outs[h], axis=0)
        o_ref[:, col(h)] = (_rms(o, og_ref[...]) * z_ref[:, col(h)]).astype(BF16)


def _gdn(gq, gk, gv, z, scol, og, *, batch, seq):
    T = gq.shape[0]
    G = min(seq, GDN_GROUP)
    ng = seq // G
    grp_spec = lambda width: pl.BlockSpec((G, width), lambda b, g: (b * ng + g, 0))
    return pl.pallas_call(
        _gdn_kernel,
        grid=(batch, ng),
        in_specs=[grp_spec(GDN_WIDTH)] * 4 + [grp_spec(LANES), _const_spec((1, GDN_HEAD_DIM))],
        out_specs=grp_spec(GDN_WIDTH),
        out_shape=jax.ShapeDtypeStruct((T, GDN_WIDTH), BF16),
        scratch_shapes=[pltpu.VMEM((GDN_HEADS, GDN_HEAD_DIM, GDN_HEAD_DIM), F32)],
        compiler_params=_params(2),
        name="gdn",
    )(gq, gk, gv, z, scol, og)


def _mix_out_kernel(x_ref, oa_ref, ob_ref, wout_ref, gx_ref, wcq_ref, cqg_ref, ck_ref, cv_ref,
                    wco_ref, o_ref):
    o = jnp.concatenate([oa_ref[...], ob_ref[...]], axis=-1)
    x1 = x_ref[...] + jnp.dot(o, wout_ref[...], preferred_element_type=F32)
    hq = _rms(x1, gx_ref[...])
    cq = _bdot(hq, wcq_ref[...])
    heads = []
    for h in range(XATTN_HEADS):
        sl = slice(h * XATTN_HEAD_DIM, (h + 1) * XATTN_HEAD_DIM)
        s = _bdot_nt(_rms(cq[:, sl], cqg_ref[...]), ck_ref[0, :, sl])
        p = jnp.exp(s - jnp.max(s, axis=-1, keepdims=True))
        pv = jnp.dot(p.astype(BF16), cv_ref[0, :, sl], preferred_element_type=F32)
        heads.append(pv / jnp.sum(p, axis=-1, keepdims=True))
    co = jnp.concatenate(heads, axis=-1)
    o_ref[...] = x1 + _bdot(co, wco_ref[...])


def _mix_out(x2d, oa, ob, w_out, gx, w_cq, cqg, ck, cv, w_co, *, seq, tm):
    T, D = x2d.shape
    M = ck.shape[1]
    tiles_per_seq = seq // tm
    row_spec = lambda width: pl.BlockSpec((tm, width), lambda i: (i, 0))
    mem_spec = pl.BlockSpec((1, M, XATTN_WIDTH), lambda i: (i // tiles_per_seq, 0, 0))
    return pl.pallas_call(
        _mix_out_kernel,
        grid=(T // tm,),
        in_specs=[row_spec(D), row_spec(FOX_WIDTH), row_spec(GDN_WIDTH), _const_spec(w_out.shape),
                  _const_spec((1, D)), _const_spec(w_cq.shape), _const_spec((1, XATTN_HEAD_DIM)),
                  mem_spec, mem_spec, _const_spec(w_co.shape)],
        out_specs=row_spec(D),
        out_shape=jax.ShapeDtypeStruct((T, D), F32),
        compiler_params=_params(1),
        name="mix_out",
    )(x2d, oa, ob, w_out, gx, w_cq, cqg, ck, cv, w_co)


def _mlp_kernel(x_ref, g_ref, w1_ref, w2_ref, o_ref, *, ff_block):
    x = x_ref[...]
    hb = _rms(x, g_ref[...]).astype(BF16)
    acc = x
    for c in range(w1_ref.shape[1] // ff_block):
        sl = slice(c * ff_block, (c + 1) * ff_block)
        a = jnp.maximum(jnp.dot(hb, w1_ref[:, sl], preferred_element_type=F32), 0.0)
        acc = acc + jnp.dot((a * a).astype(BF16), w2_ref[sl, :], preferred_element_type=F32)
    o_ref[...] = acc


def _mlp(x2d, g, w1, w2, *, tm, ff_block):
    T, D = x2d.shape
    row_spec = pl.BlockSpec((tm, D), lambda i: (i, 0))
    return pl.pallas_call(
        functools.partial(_mlp_kernel, ff_block=ff_block),
        grid=(T // tm,),
        in_specs=[row_spec, _const_spec((1, D)), _const_spec(w1.shape), _const_spec(w2.shape)],
        out_specs=row_spec,
        out_shape=jax.ShapeDtypeStruct((T, D), F32),
        compiler_params=_params(1),
        name="mlp",
    )(x2d, g, w1, w2)


def _row_tile(seq):
    return min(seq, 512)


def _layer(x, mem, norm_mix_g, w_in, fox_qnorm_g, fox_knorm_g, fox_f_bias, fox_onorm_g,
           gdn_conv_w, gdn_A_log, gdn_dt_bias, gdn_onorm_g, w_out,
           norm_xattn_g, mem_norm_g, w_cq, w_ckv, xattn_qnorm_g, xattn_knorm_g, w_co,
           norm_mlp_g, w_mlp1, w_mlp2):
    B, S, D = x.shape
    T = B * S
    tm = _row_tile(S)
    row = lambda v: v.reshape(1, -1).astype(F32)

    o_f = 3 * FOX_WIDTH
    o_g = o_f + FOX_HEADS
    o_b = o_g + 3 * GDN_WIDTH
    o_a = o_b + GDN_HEADS
    o_z = o_a + GDN_HEADS
    w_small = jnp.concatenate([w_in[:, o_f:o_g], w_in[:, o_b:o_z], w_in[:, o_a:o_z]], axis=1)
    w_small = jnp.pad(w_small, ((0, 0), (0, LANES - w_small.shape[1])))
    w_all = jnp.concatenate([w_in[:, :o_f], w_in[:, o_g:o_b], w_in[:, o_z:], w_small], axis=1).astype(BF16)
    pad4 = jnp.zeros((GDN_HEADS,), F32)
    bias_col = jnp.concatenate([fox_f_bias, pad4, gdn_dt_bias, gdn_dt_bias, pad4]).reshape(-1, 1)
    alog_col = jnp.concatenate([jnp.zeros((DECAY_ROW,), F32), gdn_A_log, gdn_A_log, pad4]).reshape(-1, 1)
    qg = row(jnp.tile(fox_qnorm_g, 2)) * (FOX_HEAD_DIM ** -0.5 * LOG2E)
    kg = row(jnp.tile(fox_knorm_g, 2))
    og = row(jnp.tile(fox_onorm_g, 2))

    x2d = x.reshape(T, D)
    ck, cv = _mem_kv(mem, row(mem_norm_g), w_ckv.astype(BF16), row(xattn_knorm_g))
    fq, fk, fvt, gq, gk, gv, z, scol, kx, qx = _in_proj(
        x2d, row(norm_mix_g), w_all, qg, kg, gdn_conv_w, bias_col, alog_col, seq=S, tm=tm)
    o_fox = _fox_attn(fq, qx, fk, kx, fvt, og, batch=B, seq=S)
    o_gdn = _gdn(gq, gk, gv, z, scol, row(gdn_onorm_g), batch=B, seq=S)
    x2 = _mix_out(x2d, o_fox, o_gdn, w_out.astype(BF16), row(norm_xattn_g), w_cq.astype(BF16),
                  row(xattn_qnorm_g) * (XATTN_HEAD_DIM ** -0.5), ck, cv, w_co.astype(BF16),
                  seq=S, tm=tm)
    out = _mlp(x2, row(norm_mlp_g), w_mlp1.astype(BF16), w_mlp2.astype(BF16), tm=tm, ff_block=1024)
    return out.reshape(B, S, D)


def kernel(x, mem, norm_mix_g, w_in, fox_qnorm_g, fox_knorm_g, fox_f_bias, fox_onorm_g, gdn_conv_w, gdn_A_log, gdn_dt_bias, gdn_onorm_g, w_out, norm_xattn_g, mem_norm_g, w_cq, w_ckv, xattn_qnorm_g, xattn_knorm_g, w_co, norm_mlp_g, w_mlp1, w_mlp2):
    for l in range(w_in.shape[0]):
        x = _layer(x, mem, norm_mix_g[l], w_in[l], fox_qnorm_g[l], fox_knorm_g[l], fox_f_bias[l],
                   fox_onorm_g[l], gdn_conv_w[l], gdn_A_log[l], gdn_dt_bias[l], gdn_onorm_g[l],
                   w_out[l], norm_xattn_g[l], mem_norm_g[l], w_cq[l], w_ckv[l], xattn_qnorm_g[l],
                   xattn_knorm_g[l], w_co[l], norm_mlp_g[l], w_mlp1[l], w_mlp2[l])
    return x
```

```python
import functools

import jax
import jax.numpy as jnp
from jax import lax
from jax.experimental import pallas as pl
from jax.experimental.pallas import tpu as pltpu

EPS = 1e-6
NEG_INF = -1e30
LOG2E = 1.4426950408889634

FOX_HEADS = 8
FOX_HEAD_DIM = 64
FOX_WIDTH = FOX_HEADS * FOX_HEAD_DIM
FOX_BLOCK = 256
FOX_PAIRS_PER_STEP = 4
GDN_HEADS = 4
GDN_HEAD_DIM = 128
GDN_WIDTH = GDN_HEADS * GDN_HEAD_DIM
CONV_WIDTH = 4
GDN_CHUNK = 64
GDN_GROUP = 256
XATTN_HEADS = 4
XATTN_HEAD_DIM = 128
XATTN_WIDTH = XATTN_HEADS * XATTN_HEAD_DIM

LANES = 128
SUBLANES = 8
SMALL_ROWS = 24
BETA_ROW = FOX_HEADS
DECAY_ROW = BETA_ROW + GDN_HEADS
REM_ROW = DECAY_ROW + GDN_HEADS
VMEM_LIMIT_BYTES = 56 * 1024 * 1024

F32 = jnp.float32
BF16 = jnp.bfloat16


def _bdot(a, b):
    return jnp.dot(a.astype(BF16), b.astype(BF16), preferred_element_type=F32)


def _bdot_nt(a, b):
    return lax.dot_general(a.astype(BF16), b.astype(BF16), (((1,), (1,)), ((), ())),
                           preferred_element_type=F32)


def _rms(x, gain):
    return x * lax.rsqrt(jnp.mean(x * x, axis=-1, keepdims=True) + EPS) * gain


def _half_lane_rms(x, gain, lane_lo):
    ss = x * x
    s_lo = jnp.sum(jnp.where(lane_lo, ss, 0.0), axis=-1, keepdims=True)
    s_hi = jnp.sum(jnp.where(lane_lo, 0.0, ss), axis=-1, keepdims=True)
    ms = jnp.where(lane_lo, s_lo, s_hi) * (1.0 / FOX_HEAD_DIM)
    return x * lax.rsqrt(ms + EPS) * gain


def _softplus(y):
    return jnp.maximum(y, 0.0) + jnp.log1p(jnp.exp(-jnp.abs(y)))


def _silu(y):
    return y / (1.0 + jnp.exp(-y))


def _split3_bf16(v):
    p1 = v.astype(BF16)
    r1 = v - p1.astype(F32)
    p2 = r1.astype(BF16)
    r2 = r1 - p2.astype(F32)
    return p1, p2, r2.astype(BF16)


def _const_spec(shape):
    return pl.BlockSpec(shape, lambda *_: (0,) * len(shape), pipeline_mode=pl.Buffered(1))


def _params(n_grid):
    return pltpu.CompilerParams(dimension_semantics=("arbitrary",) * n_grid,
                                vmem_limit_bytes=VMEM_LIMIT_BYTES)


def _mem_kv_kernel(mem_ref, g_ref, w_ref, kg_ref, ck_ref, cv_ref):
    hm = _rms(mem_ref[0], g_ref[...])
    kv = _bdot(hm, w_ref[...])
    for h in range(XATTN_HEADS):
        sl = slice(h * XATTN_HEAD_DIM, (h + 1) * XATTN_HEAD_DIM)
        ck_ref[0, :, sl] = _rms(kv[:, sl], kg_ref[...]).astype(BF16)
    cv_ref[0] = kv[:, XATTN_WIDTH:].astype(BF16)


def _mem_kv(mem, g, w_ckv, kg):
    B, M, D = mem.shape
    return pl.pallas_call(
        _mem_kv_kernel,
        grid=(B,),
        in_specs=[pl.BlockSpec((1, M, D), lambda b: (b, 0, 0)),
                  _const_spec((1, D)), _const_spec(w_ckv.shape), _const_spec((1, XATTN_HEAD_DIM))],
        out_specs=[pl.BlockSpec((1, M, XATTN_WIDTH), lambda b: (b, 0, 0))] * 2,
        out_shape=[jax.ShapeDtypeStruct((B, M, XATTN_WIDTH), BF16)] * 2,
        compiler_params=_params(1),
        name="mem_kv",
    )(mem, g, w_ckv, kg)


def _in_proj_kernel(x_ref, g_ref, w_ref, qg_ref, kg_ref, convw_ref, bias_ref, alog_ref,
                    fq_ref, fk_ref, fvt_ref, gq_ref, gk_ref, gv_ref, z_ref, scol_ref, kx_ref, qx_ref,
                    gbuf_ref, carry_ref, *, tiles_per_seq):
    tm = x_ref.shape[0]
    i = pl.program_id(0)

    @pl.when(i % tiles_per_seq == 0)
    def _():
        gbuf_ref[0:SUBLANES, :] = jnp.zeros((SUBLANES, 3 * GDN_WIDTH), F32)
        carry_ref[...] = jnp.zeros(carry_ref.shape, F32)

    hb = _rms(x_ref[...], g_ref[...]).astype(BF16)
    lane_lo = lax.broadcasted_iota(jnp.int32, (1, LANES), 1) < FOX_HEAD_DIM

    for part, (out_ref, gain_ref) in enumerate(((fq_ref, qg_ref), (fk_ref, kg_ref))):
        y = jnp.dot(hb, w_ref[:, part * FOX_WIDTH:(part + 1) * FOX_WIDTH], preferred_element_type=F32)
        for j in range(FOX_WIDTH // LANES):
            sl = slice(j * LANES, (j + 1) * LANES)
            out_ref[:, sl] = _half_lane_rms(y[:, sl], gain_ref[...], lane_lo).astype(BF16)
    fv = jnp.dot(hb, w_ref[:, 2 * FOX_WIDTH:3 * FOX_WIDTH], preferred_element_type=F32)
    for r in range(tm // FOX_BLOCK):
        fvt_ref[r] = fv[r * FOX_BLOCK:(r + 1) * FOX_BLOCK, :].T.astype(BF16)

    off = 3 * FOX_WIDTH
    gbuf_ref[SUBLANES:SUBLANES + tm, :] = jnp.dot(
        hb, w_ref[:, off:off + 3 * GDN_WIDTH], preferred_element_type=F32)
    for part, out_ref in enumerate((gq_ref, gk_ref, gv_ref)):
        for h in range(GDN_HEADS):
            c0 = part * GDN_WIDTH + h * GDN_HEAD_DIM
            cs = slice(c0, c0 + GDN_HEAD_DIM)
            acc = convw_ref[CONV_WIDTH - 1:CONV_WIDTH, cs] * gbuf_ref[SUBLANES:SUBLANES + tm, cs]
            for d in range(1, CONV_WIDTH):
                acc = acc + (convw_ref[CONV_WIDTH - 1 - d:CONV_WIDTH - d, cs]
                             * gbuf_ref[SUBLANES - d:SUBLANES - d + tm, cs])
            y = _silu(acc)
            if part == 0:
                y = y * lax.rsqrt(jnp.sum(y * y, axis=-1, keepdims=True) + EPS) * (GDN_HEAD_DIM ** -0.5)
            elif part == 1:
                y = y * lax.rsqrt(jnp.sum(y * y, axis=-1, keepdims=True) + EPS)
            out_ref[:, h * GDN_HEAD_DIM:(h + 1) * GDN_HEAD_DIM] = y
    gbuf_ref[0:SUBLANES, :] = gbuf_ref[tm:tm + SUBLANES, :]

    off += 3 * GDN_WIDTH
    z_ref[...] = _silu(jnp.dot(hb, w_ref[:, off:off + GDN_WIDTH], preferred_element_type=F32))

    off += GDN_WIDTH
    ps = jnp.dot(hb, w_ref[:, off:off + LANES], preferred_element_type=F32)
    xb = ps.T[0:SMALL_ROWS, :] + bias_ref[...]
    row = lax.broadcasted_iota(jnp.int32, (SMALL_ROWS, tm), 0)
    is_fox = row < BETA_ROW
    is_beta = row < DECAY_ROW
    sp = _softplus(jnp.where(is_fox, -xb, xb))
    val = jnp.where(is_fox, -sp,
                    jnp.where(is_beta, 1.0 / (1.0 + jnp.exp(-xb)), -jnp.exp(alog_ref[...]) * sp))
    ti = lax.broadcasted_iota(jnp.int32, (tm, tm), 0)
    tj = lax.broadcasted_iota(jnp.int32, (tm, tm), 1)
    same_chunk = (ti // GDN_CHUNK) == (tj // GDN_CHUNK)
    one_hot = lambda m: jnp.where(m, 1.0, 0.0).astype(BF16)
    scan_mats = jnp.concatenate([one_hot(ti <= tj), one_hot((ti <= tj) & same_chunk),
                                 one_hot((ti > tj) & same_chunk)], axis=1)
    sums = jnp.zeros((SMALL_ROWS, 3 * tm), F32)
    for part in _split3_bf16(val):
        sums = sums + jnp.dot(part, scan_mats, preferred_element_type=F32)
    cs_full = sums[:, 0:tm] + carry_ref[:, 0:1]
    carry_ref[...] = jnp.broadcast_to(cs_full[:, tm - 1:tm], carry_ref.shape)
    rows = jnp.where(is_fox, cs_full,
                     jnp.where(is_beta, val, jnp.where(row < REM_ROW, sums[:, tm:2 * tm], sums[:, 2 * tm:])))
    padded = jnp.concatenate([rows, jnp.zeros((LANES - SMALL_ROWS, tm), F32)], axis=0)
    scol_ref[...] = padded.T
    c_parts = [p.astype(F32) for p in _split3_bf16(rows[0:FOX_HEADS, :] * LOG2E)]
    zeros = lambda n: jnp.zeros((n, tm), F32)
    ones = jnp.ones((3 * FOX_HEADS, tm), F32)
    kx = jnp.concatenate([-p for p in c_parts] + [zeros(FOX_HEADS), ones, zeros(LANES - 7 * FOX_HEADS)], axis=0)
    qx = jnp.concatenate([ones, zeros(FOX_HEADS)] + c_parts + [zeros(LANES - 7 * FOX_HEADS)], axis=0)
    kx_ref[...] = kx.T.astype(BF16)
    qx_ref[...] = qx.T.astype(BF16)


def _in_proj(x2d, g, w_all, qg, kg, conv_w, bias_col, alog_col, *, seq, tm):
    T, D = x2d.shape
    n = T // tm
    row_spec = lambda width: pl.BlockSpec((tm, width), lambda i: (i, 0))
    kernel = functools.partial(_in_proj_kernel, tiles_per_seq=seq // tm)
    return pl.pallas_call(
        kernel,
        grid=(n,),
        in_specs=[row_spec(D), _const_spec((1, D)), _const_spec(w_all.shape),
                  _const_spec((1, LANES)), _const_spec((1, LANES)), _const_spec(conv_w.shape),
                  _const_spec((SMALL_ROWS, 1)), _const_spec((SMALL_ROWS, 1))],
        out_specs=[row_spec(FOX_WIDTH)] * 2
        + [pl.BlockSpec((tm // FOX_BLOCK, FOX_WIDTH, FOX_BLOCK), lambda i: (i, 0, 0))]
        + [row_spec(GDN_WIDTH)] * 4 + [row_spec(LANES)] * 3,
        out_shape=[jax.ShapeDtypeStruct((T, FOX_WIDTH), BF16)] * 2
        + [jax.ShapeDtypeStruct((T // FOX_BLOCK, FOX_WIDTH, FOX_BLOCK), BF16)]
        + [jax.ShapeDtypeStruct((T, GDN_WIDTH), F32)] * 4
        + [jax.ShapeDtypeStruct((T, LANES), F32)] + [jax.ShapeDtypeStruct((T, LANES), BF16)] * 2,
        scratch_shapes=[pltpu.VMEM((SUBLANES + tm, 3 * GDN_WIDTH), F32),
                        pltpu.VMEM((SMALL_ROWS, LANES), F32)],
        compiler_params=_params(1),
        name="in_proj",
    )(x2d, g, w_all, qg, kg, conv_w, bias_col, alog_col)


def _fox_kernel(q_ref, qx_ref, k_ref, kx_ref, vt_ref, og_ref, o_ref, *, pairs):
    t = q_ref.shape[0]
    Dh = FOX_HEAD_DIM
    qi = pl.program_id(2)
    lane = lax.broadcasted_iota(jnp.int32, (1, LANES), 1)
    lane_lo = lane < Dh
    in_bias_lanes = (lane < 3 * FOX_HEADS) | ((lane >= 4 * FOX_HEADS) & (lane < 7 * FOX_HEADS))
    qx = qx_ref[...]
    chains = [(p, hh) for p in range(pairs) for hh in range(2)]
    q_aug = []
    for p, hh in chains:
        h = (pl.program_id(1) * pairs + p) * 2 + hh
        q = q_ref[:, p * LANES:(p + 1) * LANES]
        q = jnp.where(lane_lo if hh == 0 else jnp.logical_not(lane_lo), q, jnp.zeros_like(q))
        qx_h = jnp.where(in_bias_lanes & (lane % FOX_HEADS == h), qx, jnp.zeros_like(qx))
        q_aug.append(jnp.concatenate([q, qx_h], axis=1))

    def block(j, carry, masked):
        r0 = pl.multiple_of(j * t, t)
        kx = kx_ref[pl.ds(r0, t), :]
        k_aug = [jnp.concatenate([k_ref[pl.ds(r0, t), p * LANES:(p + 1) * LANES], kx], axis=1)
                 for p in range(pairs)]
        s = [_bdot_nt(k_aug[p], q_aug[i]) for i, (p, hh) in enumerate(chains)]
        if masked:
            visible = (lax.broadcasted_iota(jnp.int32, (t, t), 0)
                       <= lax.broadcasted_iota(jnp.int32, (t, t), 1))
            s = [jnp.where(visible, x, NEG_INF) for x in s]
        m_new = [jnp.maximum(c[0], jnp.max(x, axis=0, keepdims=True)) for c, x in zip(carry, s)]
        out = []
        for i, (p, hh) in enumerate(chains):
            m, l, acc = carry[i]
            alpha = jnp.exp2(m - m_new[i])
            prob = jnp.exp2(s[i] - m_new[i])
            r = (p * 2 + hh) * Dh
            pv = jnp.dot(vt_ref[j, r:r + Dh, :], prob.astype(BF16), preferred_element_type=F32)
            out.append((m_new[i], alpha * l + jnp.sum(prob, axis=0, keepdims=True), alpha * acc + pv))
        return tuple(out)

    init = tuple((jnp.full((1, t), NEG_INF, F32), jnp.zeros((1, t), F32), jnp.zeros((Dh, t), F32))
                 for _ in chains)
    carry = lax.fori_loop(0, qi, functools.partial(block, masked=False), init)
    carry = block(qi, carry, masked=True)
    for p in range(pairs):
        (_, l0, a0), (_, l1, a1) = carry[2 * p], carry[2 * p + 1]
        o = jnp.concatenate([a0 / l0, a1 / l1], axis=0).T
        o_ref[:, p * LANES:(p + 1) * LANES] = _half_lane_rms(o, og_ref[...], lane_lo).astype(BF16)


def _fox_attn(fq, qx, fk, kx, fvt, og, *, batch, seq):
    T = fq.shape[0]
    t = FOX_BLOCK
    nq = seq // t
    pairs = FOX_PAIRS_PER_STEP
    width = pairs * LANES
    q_spec = lambda w: pl.BlockSpec((t, w), lambda b, g, qi: (b * nq + qi, g))
    return pl.pallas_call(
        functools.partial(_fox_kernel, pairs=pairs),
        grid=(batch, FOX_WIDTH // width, nq),
        in_specs=[q_spec(width),
                  pl.BlockSpec((t, LANES), lambda b, g, qi: (b * nq + qi, 0)),
                  pl.BlockSpec((seq, width), lambda b, g, qi: (b, g)),
                  pl.BlockSpec((seq, LANES), lambda b, g, qi: (b, 0)),
                  pl.BlockSpec((nq, width, t), lambda b, g, qi: (b, g, 0)),
                  _const_spec((1, LANES))],
        out_specs=q_spec(width),
        out_shape=jax.ShapeDtypeStruct((T, FOX_WIDTH), BF16),
        compiler_params=_params(3),
        name="fox_attn",
    )(fq, qx, fk, kx, fvt, og)


def _gdn_kernel(q_ref, k_ref, v_ref, scol_ref, z_ref, og_ref, o_ref,
                state_ref, u_ref, w_ref, qe_ref, intra_ref, kdt_ref, sm_ref, *, groups_per_seq):
    C = GDN_CHUNK
    G = q_ref.shape[0]
    Dh = GDN_HEAD_DIM
    t = pl.program_id(0)
    heads = range(GDN_HEADS)
    col = lambda h: slice(h * Dh, (h + 1) * Dh)

    @pl.when(t == 0)
    def _():
        for ref in (u_ref, w_ref, qe_ref, intra_ref, kdt_ref, sm_ref):
            ref[...] = jnp.zeros(ref.shape, ref.dtype)

    @pl.when((t == 0) | ((t - 1) % groups_per_seq == 0))
    def _():
        state_ref[...] = jnp.zeros(state_ref.shape, F32)

    cur = t % 2
    prev = 1 - cur
    us = [u_ref[prev, :, col(h)] for h in heads]
    ws16 = [w_ref[prev, :, col(h)] for h in heads]
    qes_prev = [qe_ref[prev, :, col(h)] for h in heads]
    intras_prev = [intra_ref[prev, :, col(h)] for h in heads]
    kd_ts_prev = [kdt_ref[prev, h] for h in heads]
    sm_prev = sm_ref[prev]
    states = [state_ref[h] for h in heads]
    outs = [[] for _ in heads]
    pending = {}

    def recur_first(c):
        rs = slice(c * C, (c + 1) * C)
        for h in heads:
            ws = _bdot(jnp.concatenate([ws16[h][rs], qes_prev[h][rs]], axis=0), states[h])
            pending[h] = (ws[C:2 * C], (us[h][rs] - ws[0:C]).astype(BF16))

    def recur_second(c):
        rs = slice(c * C, (c + 1) * C)
        pair = (c * C) // LANES
        for h in heads:
            qs, vb = pending[h]
            zero = jnp.zeros_like(vb)
            outs[h].append(qs + jnp.dot(intras_prev[h][rs], jnp.concatenate([vb, vb], axis=0),
                                        preferred_element_type=F32))
            vpad = jnp.concatenate([vb, zero] if (c * C) % LANES == 0 else [zero, vb], axis=0)
            g_last = sm_prev[(c + 1) * C - 1:(c + 1) * C, DECAY_ROW + h:DECAY_ROW + h + 1]
            states[h] = states[h] * jnp.exp(g_last) + jnp.dot(
                kd_ts_prev[h][:, pair * LANES:(pair + 1) * LANES], vpad, preferred_element_type=F32)

    recur = [f for c in range(G // C) for f in (functools.partial(recur_first, c),
                                                functools.partial(recur_second, c))]

    ri = lax.broadcasted_iota(jnp.int32, (G, G), 0)
    ci = lax.broadcasted_iota(jnp.int32, (G, G), 1)
    same_chunk = (ri // C) == (ci // C)
    lower = same_chunk & (ri >= ci)
    strict = same_chunk & (ri > ci)
    sm = scol_ref[...]
    sm_t = sm.T
    lmats, intras, ys, qes, kd_ts = [], [], [], [], []
    for h in heads:
        q = q_ref[:, col(h)]
        k = k_ref[:, col(h)]
        beta = sm[:, BETA_ROW + h:BETA_ROW + h + 1]
        gc = sm[:, DECAY_ROW + h:DECAY_ROW + h + 1]
        rem = sm[:, REM_ROW + h:REM_ROW + h + 1]
        gc_row = sm_t[DECAY_ROW + h:DECAY_ROW + h + 1, :]
        eg = jnp.exp(gc)
        decay = jnp.exp(jnp.where(lower, gc - gc_row, NEG_INF))
        kb = k * beta
        a = _bdot_nt(jnp.concatenate([kb, q], axis=0), k)
        lmats.append(jnp.where(strict, a[0:G] * decay, 0.0).astype(BF16))
        intra = (a[G:2 * G] * decay).astype(BF16)
        intra_fold = intra[:, 0:LANES]
        for j in range(1, G // LANES):
            intra_fold = intra_fold + intra[:, j * LANES:(j + 1) * LANES]
        intras.append(intra_fold)
        ys.append(jnp.concatenate([v_ref[:, col(h)] * beta, kb * eg], axis=1))
        qes.append((q * eg).astype(BF16))
        kd_ts.append((k * jnp.exp(rem)).T.astype(BF16))
    recur.pop(0)()
    ys = [ys[h] - _bdot(lmats[h], ys[h]) for h in heads]
    pws = [_bdot(lmats[h], lmats[h]).astype(BF16) for h in heads]
    recur.pop(0)()
    for step in range(5):
        ys = [ys[h] + _bdot(pws[h], ys[h]) for h in heads]
        if step < 4:
            pws = [_bdot(pws[h], pws[h]).astype(BF16) for h in heads]
        recur.pop(0)()
    while recur:
        recur.pop(0)()

    for h in heads:
        state_ref[h] = states[h]
        o = jnp.concatenate(outs[h], axis=0)
        o_ref[:, col(h)] = (_rms(o, og_ref[...]) * z_ref[:, col(h)]).astype(BF16)
        u_ref[cur, :, col(h)] = ys[h][:, 0:Dh]
        w_ref[cur, :, col(h)] = ys[h][:, Dh:2 * Dh].astype(BF16)
        qe_ref[cur, :, col(h)] = qes[h]
        intra_ref[cur, :, col(h)] = intras[h]
        kdt_ref[cur, h] = kd_ts[h]
    sm_ref[cur] = sm


def _gdn(gq, gk, gv, z, scol, og, *, batch, seq):
    T = gq.shape[0]
    G = min(seq, GDN_GROUP)
    ng = seq // G
    n = batch * ng
    cur_spec = lambda width: pl.BlockSpec((G, width), lambda t: (jnp.minimum(t, n - 1), 0))
    prev_spec = lambda width: pl.BlockSpec((G, width), lambda t: (jnp.maximum(t - 1, 0), 0))
    return pl.pallas_call(
        functools.partial(_gdn_kernel, groups_per_seq=ng),
        grid=(n + 1,),
        in_specs=[cur_spec(GDN_WIDTH)] * 3 + [cur_spec(LANES), prev_spec(GDN_WIDTH),
                                              _const_spec((1, GDN_HEAD_DIM))],
        out_specs=prev_spec(GDN_WIDTH),
        out_shape=jax.ShapeDtypeStruct((T, GDN_WIDTH), BF16),
        scratch_shapes=[pltpu.VMEM((GDN_HEADS, GDN_HEAD_DIM, GDN_HEAD_DIM), F32),
                        pltpu.VMEM((2, G, GDN_WIDTH), F32), pltpu.VMEM((2, G, GDN_WIDTH), BF16),
                        pltpu.VMEM((2, G, GDN_WIDTH), BF16), pltpu.VMEM((2, G, GDN_WIDTH), BF16),
                        pltpu.VMEM((2, GDN_HEADS, GDN_HEAD_DIM, G), BF16), pltpu.VMEM((2, G, LANES), F32)],
        compiler_params=_params(1),
        name="gdn",
    )(gq, gk, gv, scol, z, og)


def _mix_out_kernel(x_ref, oa_ref, ob_ref, wout_ref, gx_ref, wcq_ref, cqg_ref, ck_ref, cv_ref,
                    wco_ref, o_ref):
    o = jnp.concatenate([oa_ref[...], ob_ref[...]], axis=-1)
    x1 = x_ref[...] + jnp.dot(o, wout_ref[...], preferred_element_type=F32)
    hq = _rms(x1, gx_ref[...])
    cq = _bdot(hq, wcq_ref[...])
    heads = []
    for h in range(XATTN_HEADS):
        sl = slice(h * XATTN_HEAD_DIM, (h + 1) * XATTN_HEAD_DIM)
        s = _bdot_nt(_rms(cq[:, sl], cqg_ref[...]), ck_ref[0, :, sl])
        p = jnp.exp(s - jnp.max(s, axis=-1, keepdims=True))
        pv = jnp.dot(p.astype(BF16), cv_ref[0, :, sl], preferred_element_type=F32)
        heads.append(pv / jnp.sum(p, axis=-1, keepdims=True))
    co = jnp.concatenate(heads, axis=-1)
    o_ref[...] = x1 + _bdot(co, wco_ref[...])


def _mix_out(x2d, oa, ob, w_out, gx, w_cq, cqg, ck, cv, w_co, *, seq, tm):
    T, D = x2d.shape
    M = ck.shape[1]
    tiles_per_seq = seq // tm
    row_spec = lambda width: pl.BlockSpec((tm, width), lambda i: (i, 0))
    mem_spec = pl.BlockSpec((1, M, XATTN_WIDTH), lambda i: (i // tiles_per_seq, 0, 0))
    return pl.pallas_call(
        _mix_out_kernel,
        grid=(T // tm,),
        in_specs=[row_spec(D), row_spec(FOX_WIDTH), row_spec(GDN_WIDTH), _const_spec(w_out.shape),
                  _const_spec((1, D)), _const_spec(w_cq.shape), _const_spec((1, XATTN_HEAD_DIM)),
                  mem_spec, mem_spec, _const_spec(w_co.shape)],
        out_specs=row_spec(D),
        out_shape=jax.ShapeDtypeStruct((T, D), F32),
        compiler_params=_params(1),
        name="mix_out",
    )(x2d, oa, ob, w_out, gx, w_cq, cqg, ck, cv, w_co)


def _mlp_kernel(x_ref, g_ref, w1_ref, w2_ref, o_ref, *, ff_block):
    x = x_ref[...]
    hb = _rms(x, g_ref[...]).astype(BF16)
    acc = x
    for c in range(w1_ref.shape[1] // ff_block):
        sl = slice(c * ff_block, (c + 1) * ff_block)
        a = jnp.maximum(jnp.dot(hb, w1_ref[:, sl], preferred_element_type=F32), 0.0)
        acc = acc + jnp.dot((a * a).astype(BF16), w2_ref[sl, :], preferred_element_type=F32)
    o_ref[...] = acc


def _mlp(x2d, g, w1, w2, *, tm, ff_block):
    T, D = x2d.shape
    row_spec = pl.BlockSpec((tm, D), lambda i: (i, 0))
    return pl.pallas_call(
        functools.partial(_mlp_kernel, ff_block=ff_block),
        grid=(T // tm,),
        in_specs=[row_spec, _const_spec((1, D)), _const_spec(w1.shape), _const_spec(w2.shape)],
        out_specs=row_spec,
        out_shape=jax.ShapeDtypeStruct((T, D), F32),
        compiler_params=_params(1),
        name="mlp",
    )(x2d, g, w1, w2)


def _row_tile(seq):
    return min(seq, 512)


def _layer(x, mem, norm_mix_g, w_in, fox_qnorm_g, fox_knorm_g, fox_f_bias, fox_onorm_g,
           gdn_conv_w, gdn_A_log, gdn_dt_bias, gdn_onorm_g, w_out,
           norm_xattn_g, mem_norm_g, w_cq, w_ckv, xattn_qnorm_g, xattn_knorm_g, w_co,
           norm_mlp_g, w_mlp1, w_mlp2):
    B, S, D = x.shape
    T = B * S
    tm = _row_tile(S)
    row = lambda v: v.reshape(1, -1).astype(F32)

    o_f = 3 * FOX_WIDTH
    o_g = o_f + FOX_HEADS
    o_b = o_g + 3 * GDN_WIDTH
    o_a = o_b + GDN_HEADS
    o_z = o_a + GDN_HEADS
    w_small = jnp.concatenate([w_in[:, o_f:o_g], w_in[:, o_b:o_z], w_in[:, o_a:o_z]], axis=1)
    w_small = jnp.pad(w_small, ((0, 0), (0, LANES - w_small.shape[1])))
    w_all = jnp.concatenate([w_in[:, :o_f], w_in[:, o_g:o_b], w_in[:, o_z:], w_small], axis=1).astype(BF16)
    pad4 = jnp.zeros((GDN_HEADS,), F32)
    bias_col = jnp.concatenate([fox_f_bias, pad4, gdn_dt_bias, gdn_dt_bias, pad4]).reshape(-1, 1)
    alog_col = jnp.concatenate([jnp.zeros((DECAY_ROW,), F32), gdn_A_log, gdn_A_log, pad4]).reshape(-1, 1)
    qg = row(jnp.tile(fox_qnorm_g, 2)) * (FOX_HEAD_DIM ** -0.5 * LOG2E)
    kg = row(jnp.tile(fox_knorm_g, 2))
    og = row(jnp.tile(fox_onorm_g, 2))

    x2d = x.reshape(T, D)
    ck, cv = _mem_kv(mem, row(mem_norm_g), w_ckv.astype(BF16), row(xattn_knorm_g))
    fq, fk, fvt, gq, gk, gv, z, scol, kx, qx = _in_proj(
        x2d, row(norm_mix_g), w_all, qg, kg, gdn_conv_w, bias_col, alog_col, seq=S, tm=tm)
    o_fox = _fox_attn(fq, qx, fk, kx, fvt, og, batch=B, seq=S)
    o_gdn = _gdn(gq, gk, gv, z, scol, row(gdn_onorm_g), batch=B, seq=S)
    x2 = _mix_out(x2d, o_fox, o_gdn, w_out.astype(BF16), row(norm_xattn_g), w_cq.astype(BF16),
                  row(xattn_qnorm_g) * (XATTN_HEAD_DIM ** -0.5), ck, cv, w_co.astype(BF16),
                  seq=S, tm=tm)
    out = _mlp(x2, row(norm_mlp_g), w_mlp1.astype(BF16), w_mlp2.astype(BF16), tm=tm, ff_block=1024)
    return out.reshape(B, S, D)


def kernel(x, mem, norm_mix_g, w_in, fox_qnorm_g, fox_knorm_g, fox_f_bias, fox_onorm_g, gdn_conv_w, gdn_A_log, gdn_dt_bias, gdn_onorm_g, w_out, norm_xattn_g, mem_norm_g, w_cq, w_ckv, xattn_qnorm_g, xattn_knorm_g, w_co, norm_mlp_g, w_mlp1, w_mlp2):
    for l in range(w_in.shape[0]):
        x = _layer(x, mem, norm_mix_g[l], w_in[l], fox_qnorm_g[l], fox_knorm_g[l], fox_f_bias[l],
                   fox_onorm_g[l], gdn_conv_w[l], gdn_A_log[l], gdn_dt_bias[l], gdn_onorm_g[l],
                   w_out[l], norm_xattn_g[l], mem_norm_g[l], w_cq[l], w_ckv[l], xattn_qnorm_g[l],
                   xattn_knorm_g[l], w_co[l], norm_mlp_g[l], w_mlp1[l], w_mlp2[l])
    return x
```

```python
import functools

import jax
import jax.numpy as jnp
from jax import lax
from jax.experimental import pallas as pl
from jax.experimental.pallas import tpu as pltpu

EPS = 1e-6
NEG_INF = -1e30
LOG2E = 1.4426950408889634

FOX_HEADS = 8
FOX_HEAD_DIM = 64
FOX_WIDTH = FOX_HEADS * FOX_HEAD_DIM
FOX_BLOCK = 256
FOX_PAIRS_PER_STEP = 4
GDN_HEADS = 4
GDN_HEAD_DIM = 128
GDN_WIDTH = GDN_HEADS * GDN_HEAD_DIM
CONV_WIDTH = 4
GDN_CHUNK = 64
GDN_GROUP = 256
XATTN_HEADS = 4
XATTN_HEAD_DIM = 128
XATTN_WIDTH = XATTN_HEADS * XATTN_HEAD_DIM

LANES = 128
SUBLANES = 8
SMALL_ROWS = 24
BETA_ROW = FOX_HEADS
DECAY_ROW = BETA_ROW + GDN_HEADS
REM_ROW = DECAY_ROW + GDN_HEADS
VMEM_LIMIT_BYTES = 56 * 1024 * 1024

F32 = jnp.float32
BF16 = jnp.bfloat16


def _bdot(a, b):
    return jnp.dot(a.astype(BF16), b.astype(BF16), preferred_element_type=F32)


def _bdot_nt(a, b):
    return lax.dot_general(a.astype(BF16), b.astype(BF16), (((1,), (1,)), ((), ())),
                           preferred_element_type=F32)


def _rms(x, gain):
    return x * lax.rsqrt(jnp.mean(x * x, axis=-1, keepdims=True) + EPS) * gain


def _half_lane_rms(x, gain, lane_lo):
    ss = x * x
    s_lo = jnp.sum(jnp.where(lane_lo, ss, 0.0), axis=-1, keepdims=True)
    s_hi = jnp.sum(jnp.where(lane_lo, 0.0, ss), axis=-1, keepdims=True)
    ms = jnp.where(lane_lo, s_lo, s_hi) * (1.0 / FOX_HEAD_DIM)
    return x * lax.rsqrt(ms + EPS) * gain


def _softplus(y):
    return jnp.maximum(y, 0.0) + jnp.log1p(jnp.exp(-jnp.abs(y)))


def _silu(y):
    return y / (1.0 + jnp.exp(-y))


def _split3_bf16(v):
    p1 = v.astype(BF16)
    r1 = v - p1.astype(F32)
    p2 = r1.astype(BF16)
    r2 = r1 - p2.astype(F32)
    return p1, p2, r2.astype(BF16)


def _const_spec(shape):
    return pl.BlockSpec(shape, lambda *_: (0,) * len(shape), pipeline_mode=pl.Buffered(1))


def _params(n_grid):
    return pltpu.CompilerParams(dimension_semantics=("arbitrary",) * n_grid,
                                vmem_limit_bytes=VMEM_LIMIT_BYTES)


def _mem_kv_kernel(mem_ref, g_ref, w_ref, kg_ref, ck_ref, cv_ref):
    hm = _rms(mem_ref[0], g_ref[...])
    kv = _bdot(hm, w_ref[...])
    for h in range(XATTN_HEADS):
        sl = slice(h * XATTN_HEAD_DIM, (h + 1) * XATTN_HEAD_DIM)
        ck_ref[0, :, sl] = _rms(kv[:, sl], kg_ref[...]).astype(BF16)
    cv_ref[0] = kv[:, XATTN_WIDTH:].astype(BF16)


def _mem_kv(mem, g, w_ckv, kg):
    B, M, D = mem.shape
    return pl.pallas_call(
        _mem_kv_kernel,
        grid=(B,),
        in_specs=[pl.BlockSpec((1, M, D), lambda b: (b, 0, 0)),
                  _const_spec((1, D)), _const_spec(w_ckv.shape), _const_spec((1, XATTN_HEAD_DIM))],
        out_specs=[pl.BlockSpec((1, M, XATTN_WIDTH), lambda b: (b, 0, 0))] * 2,
        out_shape=[jax.ShapeDtypeStruct((B, M, XATTN_WIDTH), BF16)] * 2,
        compiler_params=_params(1),
        name="mem_kv",
    )(mem, g, w_ckv, kg)


def _in_proj_kernel(x_ref, g_ref, w_ref, qg_ref, kg_ref, convw_ref, bias_ref, alog_ref,
                    fq_ref, fk_ref, fvt_ref, gq_ref, gk_ref, gv_ref, z_ref, scol_ref, kx_ref, qx_ref,
                    gbuf_ref, carry_ref, *, tiles_per_seq):
    tm = x_ref.shape[0]
    i = pl.program_id(0)

    @pl.when(i % tiles_per_seq == 0)
    def _():
        gbuf_ref[0:SUBLANES, :] = jnp.zeros((SUBLANES, 3 * GDN_WIDTH), F32)
        carry_ref[...] = jnp.zeros(carry_ref.shape, F32)

    hb = _rms(x_ref[...], g_ref[...]).astype(BF16)
    lane_lo = lax.broadcasted_iota(jnp.int32, (1, LANES), 1) < FOX_HEAD_DIM

    for part, (out_ref, gain_ref) in enumerate(((fq_ref, qg_ref), (fk_ref, kg_ref))):
        y = jnp.dot(hb, w_ref[:, part * FOX_WIDTH:(part + 1) * FOX_WIDTH], preferred_element_type=F32)
        for j in range(FOX_WIDTH // LANES):
            sl = slice(j * LANES, (j + 1) * LANES)
            out_ref[:, sl] = _half_lane_rms(y[:, sl], gain_ref[...], lane_lo).astype(BF16)
    fv = jnp.dot(hb, w_ref[:, 2 * FOX_WIDTH:3 * FOX_WIDTH], preferred_element_type=F32)
    for r in range(tm // FOX_BLOCK):
        fvt_ref[r] = fv[r * FOX_BLOCK:(r + 1) * FOX_BLOCK, :].T.astype(BF16)

    off = 3 * FOX_WIDTH
    gbuf_ref[SUBLANES:SUBLANES + tm, :] = jnp.dot(
        hb, w_ref[:, off:off + 3 * GDN_WIDTH], preferred_element_type=F32)
    for part, out_ref in enumerate((gq_ref, gk_ref, gv_ref)):
        for h in range(GDN_HEADS):
            c0 = part * GDN_WIDTH + h * GDN_HEAD_DIM
            cs = slice(c0, c0 + GDN_HEAD_DIM)
            acc = convw_ref[CONV_WIDTH - 1:CONV_WIDTH, cs] * gbuf_ref[SUBLANES:SUBLANES + tm, cs]
            for d in range(1, CONV_WIDTH):
                acc = acc + (convw_ref[CONV_WIDTH - 1 - d:CONV_WIDTH - d, cs]
                             * gbuf_ref[SUBLANES - d:SUBLANES - d + tm, cs])
            y = _silu(acc)
            if part == 0:
                y = y * lax.rsqrt(jnp.sum(y * y, axis=-1, keepdims=True) + EPS) * (GDN_HEAD_DIM ** -0.5)
            elif part == 1:
                y = y * lax.rsqrt(jnp.sum(y * y, axis=-1, keepdims=True) + EPS)
            out_ref[:, h * GDN_HEAD_DIM:(h + 1) * GDN_HEAD_DIM] = y
    gbuf_ref[0:SUBLANES, :] = gbuf_ref[tm:tm + SUBLANES, :]

    off += 3 * GDN_WIDTH
    z_ref[...] = _silu(jnp.dot(hb, w_ref[:, off:off + GDN_WIDTH], preferred_element_type=F32))

    off += GDN_WIDTH
    ps = jnp.dot(hb, w_ref[:, off:off + LANES], preferred_element_type=F32)
    xb = ps.T[0:SMALL_ROWS, :] + bias_ref[...]
    row = lax.broadcasted_iota(jnp.int32, (SMALL_ROWS, tm), 0)
    is_fox = row < BETA_ROW
    is_beta = row < DECAY_ROW
    sp = _softplus(jnp.where(is_fox, -xb, xb))
    val = jnp.where(is_fox, -sp,
                    jnp.where(is_beta, 1.0 / (1.0 + jnp.exp(-xb)), -jnp.exp(alog_ref[...]) * sp))
    ti = lax.broadcasted_iota(jnp.int32, (tm, tm), 0)
    tj = lax.broadcasted_iota(jnp.int32, (tm, tm), 1)
    same_chunk = (ti // GDN_CHUNK) == (tj // GDN_CHUNK)
    one_hot = lambda m: jnp.where(m, 1.0, 0.0).astype(BF16)
    scan_mats = jnp.concatenate([one_hot(ti <= tj), one_hot((ti <= tj) & same_chunk),
                                 one_hot((ti > tj) & same_chunk)], axis=1)
    sums = jnp.zeros((SMALL_ROWS, 3 * tm), F32)
    for part in _split3_bf16(val):
        sums = sums + jnp.dot(part, scan_mats, preferred_element_type=F32)
    cs_full = sums[:, 0:tm] + carry_ref[:, 0:1]
    carry_ref[...] = jnp.broadcast_to(cs_full[:, tm - 1:tm], carry_ref.shape)
    rows = jnp.where(is_fox, cs_full,
                     jnp.where(is_beta, val, jnp.where(row < REM_ROW, sums[:, tm:2 * tm], sums[:, 2 * tm:])))
    padded = jnp.concatenate([rows, jnp.zeros((LANES - SMALL_ROWS, tm), F32)], axis=0)
    scol_ref[...] = padded.T
    c_parts = [p.astype(F32) for p in _split3_bf16(rows[0:FOX_HEADS, :] * LOG2E)]
    zeros = lambda n: jnp.zeros((n, tm), F32)
    ones = jnp.ones((3 * FOX_HEADS, tm), F32)
    kx = jnp.concatenate([-p for p in c_parts] + [zeros(FOX_HEADS), ones, zeros(LANES - 7 * FOX_HEADS)], axis=0)
    qx = jnp.concatenate([ones, zeros(FOX_HEADS)] + c_parts + [zeros(LANES - 7 * FOX_HEADS)], axis=0)
    kx_ref[...] = kx.T.astype(BF16)
    qx_ref[...] = qx.T.astype(BF16)


def _in_proj(x2d, g, w_all, qg, kg, conv_w, bias_col, alog_col, *, seq, tm):
    T, D = x2d.shape
    n = T // tm
    row_spec = lambda width: pl.BlockSpec((tm, width), lambda i: (i, 0))
    kernel = functools.partial(_in_proj_kernel, tiles_per_seq=seq // tm)
    return pl.pallas_call(
        kernel,
        grid=(n,),
        in_specs=[row_spec(D), _const_spec((1, D)), _const_spec(w_all.shape),
                  _const_spec((1, LANES)), _const_spec((1, LANES)), _const_spec(conv_w.shape),
                  _const_spec((SMALL_ROWS, 1)), _const_spec((SMALL_ROWS, 1))],
        out_specs=[row_spec(FOX_WIDTH)] * 2
        + [pl.BlockSpec((tm // FOX_BLOCK, FOX_WIDTH, FOX_BLOCK), lambda i: (i, 0, 0))]
        + [row_spec(GDN_WIDTH)] * 4 + [row_spec(LANES)] * 3,
        out_shape=[jax.ShapeDtypeStruct((T, FOX_WIDTH), BF16)] * 2
        + [jax.ShapeDtypeStruct((T // FOX_BLOCK, FOX_WIDTH, FOX_BLOCK), BF16)]
        + [jax.ShapeDtypeStruct((T, GDN_WIDTH), F32)] * 4
        + [jax.ShapeDtypeStruct((T, LANES), F32)] + [jax.ShapeDtypeStruct((T, LANES), BF16)] * 2,
        scratch_shapes=[pltpu.VMEM((SUBLANES + tm, 3 * GDN_WIDTH), F32),
                        pltpu.VMEM((SMALL_ROWS, LANES), F32)],
        compiler_params=_params(1),
        name="in_proj",
    )(x2d, g, w_all, qg, kg, conv_w, bias_col, alog_col)


def _fox_kernel(q_ref, qx_ref, k_ref, kx_ref, vt_ref, og_ref, o_ref, sa_ref, sb_ref, *, pairs):
    t = q_ref.shape[0]
    Dh = FOX_HEAD_DIM
    qi = pl.program_id(2)
    lane = lax.broadcasted_iota(jnp.int32, (1, LANES), 1)
    lane_lo = lane < Dh
    in_bias_lanes = (lane < 3 * FOX_HEADS) | ((lane >= 4 * FOX_HEADS) & (lane < 7 * FOX_HEADS))
    qx = qx_ref[...]
    chains = [(p, hh) for p in range(pairs) for hh in range(2)]
    q_aug = []
    for p, hh in chains:
        h = (pl.program_id(1) * pairs + p) * 2 + hh
        q = q_ref[:, p * LANES:(p + 1) * LANES]
        q = jnp.where(lane_lo if hh == 0 else jnp.logical_not(lane_lo), q, jnp.zeros_like(q))
        qx_h = jnp.where(in_bias_lanes & (lane % FOX_HEADS == h), qx, jnp.zeros_like(qx))
        q_aug.append(jnp.concatenate([q, qx_h], axis=1))

    n_chains = len(chains)
    lookahead = 2

    def stage(j_new, keep_all, new_ref, j_old, old_ref, maxima, carry):
        if j_new is not None:
            r0 = pl.multiple_of(j_new * t, t)
            kx = kx_ref[pl.ds(r0, t), :]
            if keep_all is not True:
                visible = (lax.broadcasted_iota(jnp.int32, (t, t), 0)
                           <= lax.broadcasted_iota(jnp.int32, (t, t), 1))
                if keep_all is not False:
                    visible = visible | keep_all
        new_maxima = [None] * n_chains
        out = [None] * n_chains

        def score(i):
            p = chains[i][0]
            k_aug = jnp.concatenate([k_ref[pl.ds(r0, t), p * LANES:(p + 1) * LANES], kx], axis=1)
            s = _bdot_nt(k_aug, q_aug[i])
            if keep_all is not True:
                s = jnp.where(visible, s, NEG_INF)
            new_ref[i] = s
            new_maxima[i] = jnp.max(s, axis=0, keepdims=True)

        def fold(i):
            p, hh = chains[i]
            m, l, acc = carry[i]
            m_new = jnp.maximum(m, maxima[i])
            alpha = jnp.exp2(m - m_new)
            prob = jnp.exp2(old_ref[i] - m_new)
            r = (p * 2 + hh) * Dh
            pv = jnp.dot(vt_ref[j_old, r:r + Dh, :], prob.astype(BF16), preferred_element_type=F32)
            out[i] = (m_new, alpha * l + jnp.sum(prob, axis=0, keepdims=True), alpha * acc + pv)

        if j_new is not None:
            for i in range(min(lookahead, n_chains)):
                score(i)
        for i in range(n_chains):
            if j_old is not None:
                fold(i)
            if j_new is not None and i + lookahead < n_chains:
                score(i + lookahead)
        return (tuple(new_maxima) if j_new is not None else None,
                tuple(out) if j_old is not None else carry)

    def steady(i, state):
        maxima, carry = state
        maxima, carry = stage(2 * i + 1, True, sb_ref, 2 * i, sa_ref, maxima, carry)
        return stage(2 * i + 2, True, sa_ref, 2 * i + 1, sb_ref, maxima, carry)

    def tail0(state):
        return stage(None, None, None, qi, sa_ref, *state)[1]

    def tail1(state):
        maxima, carry = stage(qi, False, sb_ref, qi - 1, sa_ref, *state)
        return stage(None, None, None, qi, sb_ref, maxima, carry)[1]

    def tail2(state):
        maxima, carry = stage(qi - 1, True, sb_ref, qi - 2, sa_ref, *state)
        maxima, carry = stage(qi, False, sa_ref, qi - 1, sb_ref, maxima, carry)
        return stage(None, None, None, qi, sa_ref, maxima, carry)[1]

    init = tuple((jnp.full((1, t), NEG_INF, F32), jnp.zeros((1, t), F32), jnp.zeros((Dh, t), F32))
                 for _ in chains)
    maxima, _ = stage(0, qi > 0, sa_ref, None, None, None, init)
    trips = jnp.maximum(qi - 1, 0) // 2
    state = lax.fori_loop(0, trips, steady, (maxima, init))
    carry = lax.switch(qi - 2 * trips, [tail0, tail1, tail2], state)
    for p in range(pairs):
        (_, l0, a0), (_, l1, a1) = carry[2 * p], carry[2 * p + 1]
        o = jnp.concatenate([a0 / l0, a1 / l1], axis=0).T
        o_ref[:, p * LANES:(p + 1) * LANES] = _half_lane_rms(o, og_ref[...], lane_lo).astype(BF16)


def _fox_attn(fq, qx, fk, kx, fvt, og, *, batch, seq):
    T = fq.shape[0]
    t = FOX_BLOCK
    nq = seq // t
    pairs = FOX_PAIRS_PER_STEP
    width = pairs * LANES
    q_spec = lambda w: pl.BlockSpec((t, w), lambda b, g, qi: (b * nq + qi, g))
    return pl.pallas_call(
        functools.partial(_fox_kernel, pairs=pairs),
        grid=(batch, FOX_WIDTH // width, nq),
        in_specs=[q_spec(width),
                  pl.BlockSpec((t, LANES), lambda b, g, qi: (b * nq + qi, 0)),
                  pl.BlockSpec((seq, width), lambda b, g, qi: (b, g)),
                  pl.BlockSpec((seq, LANES), lambda b, g, qi: (b, 0)),
                  pl.BlockSpec((nq, width, t), lambda b, g, qi: (b, g, 0)),
                  _const_spec((1, LANES))],
        out_specs=q_spec(width),
        out_shape=jax.ShapeDtypeStruct((T, FOX_WIDTH), BF16),
        scratch_shapes=[pltpu.VMEM((2 * pairs, t, t), F32)] * 2,
        compiler_params=_params(3),
        name="fox_attn",
    )(fq, qx, fk, kx, fvt, og)


def _gdn_kernel(q_ref, k_ref, v_ref, scol_ref, z_ref, og_ref, o_ref,
                state_ref, u_ref, w_ref, qe_ref, intra_ref, kdt_ref, sm_ref, *, groups_per_seq):
    C = GDN_CHUNK
    G = q_ref.shape[0]
    Dh = GDN_HEAD_DIM
    t = pl.program_id(0)
    heads = range(GDN_HEADS)
    col = lambda h: slice(h * Dh, (h + 1) * Dh)

    @pl.when(t == 0)
    def _():
        for ref in (u_ref, w_ref, qe_ref, intra_ref, kdt_ref, sm_ref):
            ref[...] = jnp.zeros(ref.shape, ref.dtype)

    @pl.when((t == 0) | ((t - 1) % groups_per_seq == 0))
    def _():
        state_ref[...] = jnp.zeros(state_ref.shape, F32)

    cur = t % 2
    prev = 1 - cur
    us = [u_ref[prev, :, col(h)] for h in heads]
    ws16 = [w_ref[prev, :, col(h)] for h in heads]
    qes_prev = [qe_ref[prev, :, col(h)] for h in heads]
    intras_prev = [intra_ref[prev, :, col(h)] for h in heads]
    kd_ts_prev = [kdt_ref[prev, h] for h in heads]
    sm_prev = sm_ref[prev]
    states = [state_ref[h] for h in heads]
    outs = [[] for _ in heads]
    pending = {}

    def recur_first(c):
        rs = slice(c * C, (c + 1) * C)
        for h in heads:
            ws = _bdot(jnp.concatenate([ws16[h][rs], qes_prev[h][rs]], axis=0), states[h])
            pending[h] = (ws[C:2 * C], (us[h][rs] - ws[0:C]).astype(BF16))

    def recur_second(c):
        rs = slice(c * C, (c + 1) * C)
        pair = (c * C) // LANES
        for h in heads:
            qs, vb = pending[h]
            zero = jnp.zeros_like(vb)
            outs[h].append(qs + jnp.dot(intras_prev[h][rs], jnp.concatenate([vb, vb], axis=0),
                                        preferred_element_type=F32))
            vpad = jnp.concatenate([vb, zero] if (c * C) % LANES == 0 else [zero, vb], axis=0)
            g_last = sm_prev[(c + 1) * C - 1:(c + 1) * C, DECAY_ROW + h:DECAY_ROW + h + 1]
            states[h] = states[h] * jnp.exp(g_last) + jnp.dot(
                kd_ts_prev[h][:, pair * LANES:(pair + 1) * LANES], vpad, preferred_element_type=F32)

    recur = [f for c in range(G // C) for f in (functools.partial(recur_first, c),
                                                functools.partial(recur_second, c))]

    ri = lax.broadcasted_iota(jnp.int32, (G, G), 0)
    ci = lax.broadcasted_iota(jnp.int32, (G, G), 1)
    same_chunk = (ri // C) == (ci // C)
    lower = same_chunk & (ri >= ci)
    strict = same_chunk & (ri > ci)
    sm = scol_ref[...]
    sm_t = sm.T
    lmats, intras, ys, qes, kd_ts = [], [], [], [], []
    for h in heads:
        q = q_ref[:, col(h)]
        k = k_ref[:, col(h)]
        beta = sm[:, BETA_ROW + h:BETA_ROW + h + 1]
        gc = sm[:, DECAY_ROW + h:DECAY_ROW + h + 1]
        rem = sm[:, REM_ROW + h:REM_ROW + h + 1]
        gc_row = sm_t[DECAY_ROW + h:DECAY_ROW + h + 1, :]
        eg = jnp.exp(gc)
        decay = jnp.exp(jnp.where(lower, gc - gc_row, NEG_INF))
        kb = k * beta
        a = _bdot_nt(jnp.concatenate([kb, q], axis=0), k)
        lmats.append(jnp.where(strict, a[0:G] * decay, 0.0).astype(BF16))
        intra = (a[G:2 * G] * decay).astype(BF16)
        intra_fold = intra[:, 0:LANES]
        for j in range(1, G // LANES):
            intra_fold = intra_fold + intra[:, j * LANES:(j + 1) * LANES]
        intras.append(intra_fold)
        ys.append(jnp.concatenate([v_ref[:, col(h)] * beta, kb * eg], axis=1))
        qes.append((q * eg).astype(BF16))
        kd_ts.append((k * jnp.exp(rem)).T.astype(BF16))
    recur.pop(0)()
    ys = [ys[h] - _bdot(lmats[h], ys[h]) for h in heads]
    pws = [_bdot(lmats[h], lmats[h]).astype(BF16) for h in heads]
    recur.pop(0)()
    for step in range(5):
        ys = [ys[h] + _bdot(pws[h], ys[h]) for h in heads]
        if step < 4:
            pws = [_bdot(pws[h], pws[h]).astype(BF16) for h in heads]
        recur.pop(0)()
    while recur:
        recur.pop(0)()

    for h in heads:
        state_ref[h] = states[h]
        o = jnp.concatenate(outs[h], axis=0)
        o_ref[:, col(h)] = (_rms(o, og_ref[...]) * z_ref[:, col(h)]).astype(BF16)
        u_ref[cur, :, col(h)] = ys[h][:, 0:Dh]
        w_ref[cur, :, col(h)] = ys[h][:, Dh:2 * Dh].astype(BF16)
        qe_ref[cur, :, col(h)] = qes[h]
        intra_ref[cur, :, col(h)] = intras[h]
        kdt_ref[cur, h] = kd_ts[h]
    sm_ref[cur] = sm


def _gdn(gq, gk, gv, z, scol, og, *, batch, seq):
    T = gq.shape[0]
    G = min(seq, GDN_GROUP)
    ng = seq // G
    n = batch * ng
    cur_spec = lambda width: pl.BlockSpec((G, width), lambda t: (jnp.minimum(t, n - 1), 0))
    prev_spec = lambda width: pl.BlockSpec((G, width), lambda t: (jnp.maximum(t - 1, 0), 0))
    return pl.pallas_call(
        functools.partial(_gdn_kernel, groups_per_seq=ng),
        grid=(n + 1,),
        in_specs=[cur_spec(GDN_WIDTH)] * 3 + [cur_spec(LANES), prev_spec(GDN_WIDTH),
                                              _const_spec((1, GDN_HEAD_DIM))],
        out_specs=prev_spec(GDN_WIDTH),
        out_shape=jax.ShapeDtypeStruct((T, GDN_WIDTH), BF16),
        scratch_shapes=[pltpu.VMEM((GDN_HEADS, GDN_HEAD_DIM, GDN_HEAD_DIM), F32),
                        pltpu.VMEM((2, G, GDN_WIDTH), F32), pltpu.VMEM((2, G, GDN_WIDTH), BF16),
                        pltpu.VMEM((2, G, GDN_WIDTH), BF16), pltpu.VMEM((2, G, GDN_WIDTH), BF16),
                        pltpu.VMEM((2, GDN_HEADS, GDN_HEAD_DIM, G), BF16), pltpu.VMEM((2, G, LANES), F32)],
        compiler_params=_params(1),
        name="gdn",
    )(gq, gk, gv, scol, z, og)


def _mix_out_kernel(x_ref, oa_ref, ob_ref, wout_ref, gx_ref, wcq_ref, cqg_ref, ck_ref, cv_ref,
                    wco_ref, o_ref):
    o = jnp.concatenate([oa_ref[...], ob_ref[...]], axis=-1)
    x1 = x_ref[...] + jnp.dot(o, wout_ref[...], preferred_element_type=F32)
    hq = _rms(x1, gx_ref[...])
    cq = _bdot(hq, wcq_ref[...])
    heads = []
    for h in range(XATTN_HEADS):
        sl = slice(h * XATTN_HEAD_DIM, (h + 1) * XATTN_HEAD_DIM)
        s = _bdot_nt(_rms(cq[:, sl], cqg_ref[...]), ck_ref[0, :, sl])
        p = jnp.exp(s - jnp.max(s, axis=-1, keepdims=True))
        pv = jnp.dot(p.astype(BF16), cv_ref[0, :, sl], preferred_element_type=F32)
        heads.append(pv / jnp.sum(p, axis=-1, keepdims=True))
    co = jnp.concatenate(heads, axis=-1)
    o_ref[...] = x1 + _bdot(co, wco_ref[...])


def _mix_out(x2d, oa, ob, w_out, gx, w_cq, cqg, ck, cv, w_co, *, seq, tm):
    T, D = x2d.shape
    M = ck.shape[1]
    tiles_per_seq = seq // tm
    row_spec = lambda width: pl.BlockSpec((tm, width), lambda i: (i, 0))
    mem_spec = pl.BlockSpec((1, M, XATTN_WIDTH), lambda i: (i // tiles_per_seq, 0, 0))
    return pl.pallas_call(
        _mix_out_kernel,
        grid=(T // tm,),
        in_specs=[row_spec(D), row_spec(FOX_WIDTH), row_spec(GDN_WIDTH), _const_spec(w_out.shape),
                  _const_spec((1, D)), _const_spec(w_cq.shape), _const_spec((1, XATTN_HEAD_DIM)),
                  mem_spec, mem_spec, _const_spec(w_co.shape)],
        out_specs=row_spec(D),
        out_shape=jax.ShapeDtypeStruct((T, D), F32),
        compiler_params=_params(1),
        name="mix_out",
    )(x2d, oa, ob, w_out, gx, w_cq, cqg, ck, cv, w_co)


def _mlp_kernel(x_ref, g_ref, w1_ref, w2_ref, o_ref, *, ff_block):
    x = x_ref[...]
    hb = _rms(x, g_ref[...]).astype(BF16)
    acc = x
    for c in range(w1_ref.shape[1] // ff_block):
        sl = slice(c * ff_block, (c + 1) * ff_block)
        a = jnp.maximum(jnp.dot(hb, w1_ref[:, sl], preferred_element_type=F32), 0.0)
        acc = acc + jnp.dot((a * a).astype(BF16), w2_ref[sl, :], preferred_element_type=F32)
    o_ref[...] = acc


def _mlp(x2d, g, w1, w2, *, tm, ff_block):
    T, D = x2d.shape
    row_spec = pl.BlockSpec((tm, D), lambda i: (i, 0))
    return pl.pallas_call(
        functools.partial(_mlp_kernel, ff_block=ff_block),
        grid=(T // tm,),
        in_specs=[row_spec, _const_spec((1, D)), _const_spec(w1.shape), _const_spec(w2.shape)],
        out_specs=row_spec,
        out_shape=jax.ShapeDtypeStruct((T, D), F32),
        compiler_params=_params(1),
        name="mlp",
    )(x2d, g, w1, w2)


def _row_tile(seq):
    return min(seq, 512)


def _layer(x, mem, norm_mix_g, w_in, fox_qnorm_g, fox_knorm_g, fox_f_bias, fox_onorm_g,
           gdn_conv_w, gdn_A_log, gdn_dt_bias, gdn_onorm_g, w_out,
           norm_xattn_g, mem_norm_g, w_cq, w_ckv, xattn_qnorm_g, xattn_knorm_g, w_co,
           norm_mlp_g, w_mlp1, w_mlp2):
    B, S, D = x.shape
    T = B * S
    tm = _row_tile(S)
    row = lambda v: v.reshape(1, -1).astype(F32)

    o_f = 3 * FOX_WIDTH
    o_g = o_f + FOX_HEADS
    o_b = o_g + 3 * GDN_WIDTH
    o_a = o_b + GDN_HEADS
    o_z = o_a + GDN_HEADS
    w_small = jnp.concatenate([w_in[:, o_f:o_g], w_in[:, o_b:o_z], w_in[:, o_a:o_z]], axis=1)
    w_small = jnp.pad(w_small, ((0, 0), (0, LANES - w_small.shape[1])))
    w_all = jnp.concatenate([w_in[:, :o_f], w_in[:, o_g:o_b], w_in[:, o_z:], w_small], axis=1).astype(BF16)
    pad4 = jnp.zeros((GDN_HEADS,), F32)
    bias_col = jnp.concatenate([fox_f_bias, pad4, gdn_dt_bias, gdn_dt_bias, pad4]).reshape(-1, 1)
    alog_col = jnp.concatenate([jnp.zeros((DECAY_ROW,), F32), gdn_A_log, gdn_A_log, pad4]).reshape(-1, 1)
    qg = row(jnp.tile(fox_qnorm_g, 2)) * (FOX_HEAD_DIM ** -0.5 * LOG2E)
    kg = row(jnp.tile(fox_knorm_g, 2))
    og = row(jnp.tile(fox_onorm_g, 2))

    x2d = x.reshape(T, D)
    ck, cv = _mem_kv(mem, row(mem_norm_g), w_ckv.astype(BF16), row(xattn_knorm_g))
    fq, fk, fvt, gq, gk, gv, z, scol, kx, qx = _in_proj(
        x2d, row(norm_mix_g), w_all, qg, kg, gdn_conv_w, bias_col, alog_col, seq=S, tm=tm)
    o_fox = _fox_attn(fq, qx, fk, kx, fvt, og, batch=B, seq=S)
    o_gdn = _gdn(gq, gk, gv, z, scol, row(gdn_onorm_g), batch=B, seq=S)
    x2 = _mix_out(x2d, o_fox, o_gdn, w_out.astype(BF16), row(norm_xattn_g), w_cq.astype(BF16),
                  row(xattn_qnorm_g) * (XATTN_HEAD_DIM ** -0.5), ck, cv, w_co.astype(BF16),
                  seq=S, tm=tm)
    out = _mlp(x2, row(norm_mlp_g), w_mlp1.astype(BF16), w_mlp2.astype(BF16), tm=tm, ff_block=1024)
    return out.reshape(B, S, D)


def kernel(x, mem, norm_mix_g, w_in, fox_qnorm_g, fox_knorm_g, fox_f_bias, fox_onorm_g, gdn_conv_w, gdn_A_log, gdn_dt_bias, gdn_onorm_g, w_out, norm_xattn_g, mem_norm_g, w_cq, w_ckv, xattn_qnorm_g, xattn_knorm_g, w_co, norm_mlp_g, w_mlp1, w_mlp2):
    for l in range(w_in.shape[0]):
        x = _layer(x, mem, norm_mix_g[l], w_in[l], fox_qnorm_g[l], fox_knorm_g[l], fox_f_bias[l],
                   fox_onorm_g[l], gdn_conv_w[l], gdn_A_log[l], gdn_dt_bias[l], gdn_onorm_g[l],
                   w_out[l], norm_xattn_g[l], mem_norm_g[l], w_cq[l], w_ckv[l], xattn_qnorm_g[l],
                   xattn_knorm_g[l], w_co[l], norm_mlp_g[l], w_mlp1[l], w_mlp2[l])
    return x
```

```python
import functools

import jax
import jax.numpy as jnp
from jax import lax
from jax.experimental import pallas as pl
from jax.experimental.pallas import tpu as pltpu

EPS = 1e-6
NEG_INF = -1e30
LOG2E = 1.4426950408889634

FOX_HEADS = 8
FOX_HEAD_DIM = 64
FOX_WIDTH = FOX_HEADS * FOX_HEAD_DIM
FOX_BLOCK = 256
FOX_PAIRS_PER_STEP = 4
GDN_HEADS = 4
GDN_HEAD_DIM = 128
GDN_WIDTH = GDN_HEADS * GDN_HEAD_DIM
CONV_WIDTH = 4
GDN_CHUNK = 64
GDN_GROUP = 256
XATTN_HEADS = 4
XATTN_HEAD_DIM = 128
XATTN_WIDTH = XATTN_HEADS * XATTN_HEAD_DIM

LANES = 128
SUBLANES = 8
SMALL_ROWS = 24
BETA_ROW = FOX_HEADS
DECAY_ROW = BETA_ROW + GDN_HEADS
REM_ROW = DECAY_ROW + GDN_HEADS
VMEM_LIMIT_BYTES = 56 * 1024 * 1024

F32 = jnp.float32
BF16 = jnp.bfloat16


def _bdot(a, b):
    return jnp.dot(a.astype(BF16), b.astype(BF16), preferred_element_type=F32)


def _bdot_nt(a, b):
    return lax.dot_general(a.astype(BF16), b.astype(BF16), (((1,), (1,)), ((), ())),
                           preferred_element_type=F32)


def _rms(x, gain):
    return x * lax.rsqrt(jnp.mean(x * x, axis=-1, keepdims=True) + EPS) * gain


def _half_lane_rms(x, gain, lane_lo):
    ss = x * x
    s_lo = jnp.sum(jnp.where(lane_lo, ss, 0.0), axis=-1, keepdims=True)
    s_hi = jnp.sum(jnp.where(lane_lo, 0.0, ss), axis=-1, keepdims=True)
    ms = jnp.where(lane_lo, s_lo, s_hi) * (1.0 / FOX_HEAD_DIM)
    return x * lax.rsqrt(ms + EPS) * gain


def _softplus(y):
    return jnp.maximum(y, 0.0) + jnp.log1p(jnp.exp(-jnp.abs(y)))


def _silu(y):
    return y / (1.0 + jnp.exp2(y * (-LOG2E)))


def _split3_bf16(v):
    p1 = v.astype(BF16)
    r1 = v - p1.astype(F32)
    p2 = r1.astype(BF16)
    r2 = r1 - p2.astype(F32)
    return p1, p2, r2.astype(BF16)


def _const_spec(shape):
    return pl.BlockSpec(shape, lambda *_: (0,) * len(shape), pipeline_mode=pl.Buffered(1))


def _params(n_grid):
    return pltpu.CompilerParams(dimension_semantics=("arbitrary",) * n_grid,
                                vmem_limit_bytes=VMEM_LIMIT_BYTES)


def _mem_kv_kernel(mem_ref, g_ref, w_ref, kg_ref, ck_ref, cv_ref):
    hm = _rms(mem_ref[0], g_ref[...])
    kv = _bdot(hm, w_ref[...])
    for h in range(XATTN_HEADS):
        sl = slice(h * XATTN_HEAD_DIM, (h + 1) * XATTN_HEAD_DIM)
        ck_ref[0, :, sl] = _rms(kv[:, sl], kg_ref[...]).astype(BF16)
    cv_ref[0] = kv[:, XATTN_WIDTH:].astype(BF16)


def _mem_kv(mem, g, w_ckv, kg):
    B, M, D = mem.shape
    return pl.pallas_call(
        _mem_kv_kernel,
        grid=(B,),
        in_specs=[pl.BlockSpec((1, M, D), lambda b: (b, 0, 0)),
                  _const_spec((1, D)), _const_spec(w_ckv.shape), _const_spec((1, XATTN_HEAD_DIM))],
        out_specs=[pl.BlockSpec((1, M, XATTN_WIDTH), lambda b: (b, 0, 0))] * 2,
        out_shape=[jax.ShapeDtypeStruct((B, M, XATTN_WIDTH), BF16)] * 2,
        compiler_params=_params(1),
        name="mem_kv",
    )(mem, g, w_ckv, kg)


def _in_proj_kernel(x_ref, g_ref, w_ref, qg_ref, kg_ref, convw_ref, bias_ref, alog_ref,
                    fq_ref, fk_ref, fvt_ref, gq_ref, gk_ref, gv_ref, z_ref, scol_ref, kx_ref, qx_ref,
                    gbuf_q, gbuf_k, gbuf_v, carry_ref, *, tiles_per_seq):
    tm = x_ref.shape[0]
    i = pl.program_id(0)
    gbufs = (gbuf_q, gbuf_k, gbuf_v)

    @pl.when(i % tiles_per_seq == 0)
    def _():
        for gbuf in gbufs:
            gbuf[0:SUBLANES, :] = jnp.zeros((SUBLANES, GDN_WIDTH), F32)
        carry_ref[...] = jnp.zeros(carry_ref.shape, F32)

    hb = _rms(x_ref[...], g_ref[...]).astype(BF16)
    lane_lo = lax.broadcasted_iota(jnp.int32, (1, LANES), 1) < FOX_HEAD_DIM
    off_gdn = 3 * FOX_WIDTH
    off_z = off_gdn + 3 * GDN_WIDTH
    off_small = off_z + GDN_WIDTH
    proj = lambda off, width: jnp.dot(hb, w_ref[:, off:off + width], preferred_element_type=F32)

    def gdn_proj(part):
        gbufs[part][SUBLANES:SUBLANES + tm, :] = proj(off_gdn + part * GDN_WIDTH, GDN_WIDTH)

    def gdn_head(part, h):
        gbuf, out_ref = gbufs[part], (gq_ref, gk_ref, gv_ref)[part]
        cs = slice(h * GDN_HEAD_DIM, (h + 1) * GDN_HEAD_DIM)
        ws = slice(part * GDN_WIDTH + h * GDN_HEAD_DIM, part * GDN_WIDTH + (h + 1) * GDN_HEAD_DIM)
        acc = convw_ref[CONV_WIDTH - 1:CONV_WIDTH, ws] * gbuf[SUBLANES:SUBLANES + tm, cs]
        for d in range(1, CONV_WIDTH):
            acc = acc + (convw_ref[CONV_WIDTH - 1 - d:CONV_WIDTH - d, ws]
                         * gbuf[SUBLANES - d:SUBLANES - d + tm, cs])
        y = _silu(acc)
        if part == 0:
            y = y * lax.rsqrt(jnp.sum(y * y, axis=-1, keepdims=True) + EPS) * (GDN_HEAD_DIM ** -0.5)
        elif part == 1:
            y = y * lax.rsqrt(jnp.sum(y * y, axis=-1, keepdims=True) + EPS)
        out_ref[:, cs] = y
        gbuf[0:SUBLANES, cs] = gbuf[tm:tm + SUBLANES, cs]

    MXU_COLS = 2 * LANES

    def fox_qk(out_ref, gain_ref, off, c0):
        y = proj(off + c0, MXU_COLS)
        for j in range(MXU_COLS // LANES):
            sl = slice(c0 + j * LANES, c0 + (j + 1) * LANES)
            out_ref[:, sl] = _half_lane_rms(y[:, j * LANES:(j + 1) * LANES], gain_ref[...],
                                            lane_lo).astype(BF16)

    def fox_v(c0):
        y = proj(2 * FOX_WIDTH + c0, MXU_COLS)
        for r in range(tm // FOX_BLOCK):
            fvt_ref[r, c0:c0 + MXU_COLS, :] = y[r * FOX_BLOCK:(r + 1) * FOX_BLOCK, :].T.astype(BF16)

    def gate(c0):
        z_ref[:, c0:c0 + MXU_COLS] = _silu(proj(off_z + c0, MXU_COLS))

    row = lax.broadcasted_iota(jnp.int32, (SMALL_ROWS, tm), 0)
    is_fox = row < BETA_ROW
    is_beta = row < DECAY_ROW

    def scalars_gates():
        xb = proj(off_small, LANES).T[0:SMALL_ROWS, :] + bias_ref[...]
        sp = _softplus(jnp.where(is_fox, -xb, xb))
        return jnp.where(is_fox, -sp,
                         jnp.where(is_beta, 1.0 / (1.0 + jnp.exp(-xb)), -jnp.exp(alog_ref[...]) * sp))

    def scalars_scans(val):
        ti = lax.broadcasted_iota(jnp.int32, (tm, tm), 0)
        tj = lax.broadcasted_iota(jnp.int32, (tm, tm), 1)
        same_chunk = (ti // GDN_CHUNK) == (tj // GDN_CHUNK)
        one_hot = lambda m: jnp.where(m, 1.0, 0.0).astype(BF16)
        scan_mats = jnp.concatenate([one_hot(ti <= tj), one_hot((ti <= tj) & same_chunk),
                                     one_hot((ti > tj) & same_chunk)], axis=1)
        sums = jnp.zeros((SMALL_ROWS, 3 * tm), F32)
        for part in _split3_bf16(val):
            sums = sums + jnp.dot(part, scan_mats, preferred_element_type=F32)
        cs_full = sums[:, 0:tm] + carry_ref[:, 0:1]
        carry_ref[...] = jnp.broadcast_to(cs_full[:, tm - 1:tm], carry_ref.shape)
        return jnp.where(is_fox, cs_full, jnp.where(is_beta, val, jnp.where(
            row < REM_ROW, sums[:, tm:2 * tm], sums[:, 2 * tm:])))

    def scalars_columns(rows):
        padded = jnp.concatenate([rows, jnp.zeros((LANES - SMALL_ROWS, tm), F32)], axis=0)
        scol_ref[...] = padded.T

    def scalars_bias_lanes(rows):
        c_parts = [p.astype(F32) for p in _split3_bf16(rows[0:FOX_HEADS, :] * LOG2E)]
        zeros = lambda n: jnp.zeros((n, tm), F32)
        ones = jnp.ones((3 * FOX_HEADS, tm), F32)
        kx = jnp.concatenate([-p for p in c_parts] + [zeros(FOX_HEADS), ones, zeros(LANES - 7 * FOX_HEADS)],
                             axis=0)
        qx = jnp.concatenate([ones, zeros(FOX_HEADS)] + c_parts + [zeros(LANES - 7 * FOX_HEADS)], axis=0)
        kx_ref[...] = kx.T.astype(BF16)
        qx_ref[...] = qx.T.astype(BF16)

    val = scalars_gates()
    gdn_proj(0)
    gdn_proj(1)
    gdn_proj(2)
    rows = scalars_scans(val)
    light = ([functools.partial(fox_qk, fq_ref, qg_ref, 0, c) for c in range(0, FOX_WIDTH, MXU_COLS)]
             + [functools.partial(fox_qk, fk_ref, kg_ref, FOX_WIDTH, c) for c in range(0, FOX_WIDTH, MXU_COLS)]
             + [functools.partial(scalars_columns, rows)]
             + [functools.partial(fox_v, c) for c in range(0, FOX_WIDTH, MXU_COLS)]
             + [functools.partial(scalars_bias_lanes, rows)]
             + [functools.partial(gate, c) for c in range(0, GDN_WIDTH, MXU_COLS)])
    for part in range(3):
        for h in range(GDN_HEADS):
            gdn_head(part, h)
            if light:
                light.pop(0)()
    while light:
        light.pop(0)()


def _in_proj(x2d, g, w_all, qg, kg, conv_w, bias_col, alog_col, *, seq, tm):
    T, D = x2d.shape
    n = T // tm
    row_spec = lambda width: pl.BlockSpec((tm, width), lambda i: (i, 0))
    kernel = functools.partial(_in_proj_kernel, tiles_per_seq=seq // tm)
    return pl.pallas_call(
        kernel,
        grid=(n,),
        in_specs=[row_spec(D), _const_spec((1, D)), _const_spec(w_all.shape),
                  _const_spec((1, LANES)), _const_spec((1, LANES)), _const_spec(conv_w.shape),
                  _const_spec((SMALL_ROWS, 1)), _const_spec((SMALL_ROWS, 1))],
        out_specs=[row_spec(FOX_WIDTH)] * 2
        + [pl.BlockSpec((tm // FOX_BLOCK, FOX_WIDTH, FOX_BLOCK), lambda i: (i, 0, 0))]
        + [row_spec(GDN_WIDTH)] * 4 + [row_spec(LANES)] * 3,
        out_shape=[jax.ShapeDtypeStruct((T, FOX_WIDTH), BF16)] * 2
        + [jax.ShapeDtypeStruct((T // FOX_BLOCK, FOX_WIDTH, FOX_BLOCK), BF16)]
        + [jax.ShapeDtypeStruct((T, GDN_WIDTH), F32)] * 4
        + [jax.ShapeDtypeStruct((T, LANES), F32)] + [jax.ShapeDtypeStruct((T, LANES), BF16)] * 2,
        scratch_shapes=[pltpu.VMEM((SUBLANES + tm, GDN_WIDTH), F32)] * 3
        + [pltpu.VMEM((SMALL_ROWS, LANES), F32)],
        compiler_params=_params(1),
        name="in_proj",
    )(x2d, g, w_all, qg, kg, conv_w, bias_col, alog_col)


def _fox_kernel(q_ref, qx_ref, k_ref, kx_ref, vt_ref, og_ref, o_ref, sa_ref, sb_ref, *, pairs):
    t = q_ref.shape[0]
    Dh = FOX_HEAD_DIM
    qi = pl.program_id(2)
    lane = lax.broadcasted_iota(jnp.int32, (1, LANES), 1)
    lane_lo = lane < Dh
    in_bias_lanes = (lane < 3 * FOX_HEADS) | ((lane >= 4 * FOX_HEADS) & (lane < 7 * FOX_HEADS))
    qx = qx_ref[...]
    chains = [(p, hh) for p in range(pairs) for hh in range(2)]
    q_aug = []
    for p, hh in chains:
        h = (pl.program_id(1) * pairs + p) * 2 + hh
        q = q_ref[:, p * LANES:(p + 1) * LANES]
        q = jnp.where(lane_lo if hh == 0 else jnp.logical_not(lane_lo), q, jnp.zeros_like(q))
        qx_h = jnp.where(in_bias_lanes & (lane % FOX_HEADS == h), qx, jnp.zeros_like(qx))
        q_aug.append(jnp.concatenate([q, qx_h], axis=1))

    n_chains = len(chains)
    lookahead = 2

    def stage(j_new, keep_all, new_ref, j_old, old_ref, maxima, carry):
        if j_new is not None:
            r0 = pl.multiple_of(j_new * t, t)
            kx = kx_ref[pl.ds(r0, t), :]
            if keep_all is not True:
                visible = (lax.broadcasted_iota(jnp.int32, (t, t), 0)
                           <= lax.broadcasted_iota(jnp.int32, (t, t), 1))
                if keep_all is not False:
                    visible = visible | keep_all
        new_maxima = [None] * n_chains
        out = [None] * n_chains

        def score(i):
            p = chains[i][0]
            k_aug = jnp.concatenate([k_ref[pl.ds(r0, t), p * LANES:(p + 1) * LANES], kx], axis=1)
            s = _bdot_nt(k_aug, q_aug[i])
            if keep_all is not True:
                s = jnp.where(visible, s, NEG_INF)
            new_ref[i] = s
            new_maxima[i] = jnp.max(s, axis=0, keepdims=True)

        def fold(i):
            p, hh = chains[i]
            m, l, acc = carry[i]
            m_new = jnp.maximum(m, maxima[i])
            alpha = jnp.exp2(m - m_new)
            prob = jnp.exp2(old_ref[i] - m_new)
            r = (p * 2 + hh) * Dh
            pv = jnp.dot(vt_ref[j_old, r:r + Dh, :], prob.astype(BF16), preferred_element_type=F32)
            out[i] = (m_new, alpha * l + jnp.sum(prob, axis=0, keepdims=True), alpha * acc + pv)

        if j_new is not None:
            for i in range(min(lookahead, n_chains)):
                score(i)
        for i in range(n_chains):
            if j_old is not None:
                fold(i)
            if j_new is not None and i + lookahead < n_chains:
                score(i + lookahead)
        return (tuple(new_maxima) if j_new is not None else None,
                tuple(out) if j_old is not None else carry)

    def steady(i, state):
        maxima, carry = state
        maxima, carry = stage(2 * i + 1, True, sb_ref, 2 * i, sa_ref, maxima, carry)
        return stage(2 * i + 2, True, sa_ref, 2 * i + 1, sb_ref, maxima, carry)

    def tail0(state):
        return stage(None, None, None, qi, sa_ref, *state)[1]

    def tail1(state):
        maxima, carry = stage(qi, False, sb_ref, qi - 1, sa_ref, *state)
        return stage(None, None, None, qi, sb_ref, maxima, carry)[1]

    def tail2(state):
        maxima, carry = stage(qi - 1, True, sb_ref, qi - 2, sa_ref, *state)
        maxima, carry = stage(qi, False, sa_ref, qi - 1, sb_ref, maxima, carry)
        return stage(None, None, None, qi, sa_ref, maxima, carry)[1]

    init = tuple((jnp.full((1, t), NEG_INF, F32), jnp.zeros((1, t), F32), jnp.zeros((Dh, t), F32))
                 for _ in chains)
    maxima, _ = stage(0, qi > 0, sa_ref, None, None, None, init)
    trips = jnp.maximum(qi - 1, 0) // 2
    state = lax.fori_loop(0, trips, steady, (maxima, init))
    carry = lax.switch(qi - 2 * trips, [tail0, tail1, tail2], state)
    for p in range(pairs):
        (_, l0, a0), (_, l1, a1) = carry[2 * p], carry[2 * p + 1]
        o = jnp.concatenate([a0 / l0, a1 / l1], axis=0).T
        o_ref[:, p * LANES:(p + 1) * LANES] = _half_lane_rms(o, og_ref[...], lane_lo).astype(BF16)


def _fox_attn(fq, qx, fk, kx, fvt, og, *, batch, seq):
    T = fq.shape[0]
    t = FOX_BLOCK
    nq = seq // t
    pairs = FOX_PAIRS_PER_STEP
    width = pairs * LANES
    q_spec = lambda w: pl.BlockSpec((t, w), lambda b, g, qi: (b * nq + qi, g))
    return pl.pallas_call(
        functools.partial(_fox_kernel, pairs=pairs),
        grid=(batch, FOX_WIDTH // width, nq),
        in_specs=[q_spec(width),
                  pl.BlockSpec((t, LANES), lambda b, g, qi: (b * nq + qi, 0)),
                  pl.BlockSpec((seq, width), lambda b, g, qi: (b, g)),
                  pl.BlockSpec((seq, LANES), lambda b, g, qi: (b, 0)),
                  pl.BlockSpec((nq, width, t), lambda b, g, qi: (b, g, 0)),
                  _const_spec((1, LANES))],
        out_specs=q_spec(width),
        out_shape=jax.ShapeDtypeStruct((T, FOX_WIDTH), BF16),
        scratch_shapes=[pltpu.VMEM((2 * pairs, t, t), F32)] * 2,
        compiler_params=_params(3),
        name="fox_attn",
    )(fq, qx, fk, kx, fvt, og)


def _gdn_kernel(q_ref, k_ref, v_ref, scol_ref, z_ref, og_ref, o_ref,
                state_ref, u_ref, w_ref, qe_ref, intra_ref, kdt_ref, sm_ref, *, groups_per_seq):
    C = GDN_CHUNK
    G = q_ref.shape[0]
    Dh = GDN_HEAD_DIM
    t = pl.program_id(0)
    heads = range(GDN_HEADS)
    col = lambda h: slice(h * Dh, (h + 1) * Dh)

    @pl.when(t == 0)
    def _():
        for ref in (u_ref, w_ref, qe_ref, intra_ref, kdt_ref, sm_ref):
            ref[...] = jnp.zeros(ref.shape, ref.dtype)

    @pl.when((t == 0) | ((t - 1) % groups_per_seq == 0))
    def _():
        state_ref[...] = jnp.zeros(state_ref.shape, F32)

    cur = t % 2
    prev = 1 - cur
    us = [u_ref[prev, :, col(h)] for h in heads]
    ws16 = [w_ref[prev, :, col(h)] for h in heads]
    qes_prev = [qe_ref[prev, :, col(h)] for h in heads]
    intras_prev = [intra_ref[prev, :, col(h)] for h in heads]
    kd_ts_prev = [kdt_ref[prev, h] for h in heads]
    sm_prev = sm_ref[prev]
    states = [state_ref[h] for h in heads]
    outs = [[] for _ in heads]
    pending = {}

    def recur_first(c):
        rs = slice(c * C, (c + 1) * C)
        for h in heads:
            ws = _bdot(jnp.concatenate([ws16[h][rs], qes_prev[h][rs]], axis=0), states[h])
            pending[h] = (ws[C:2 * C], (us[h][rs] - ws[0:C]).astype(BF16))

    def recur_second(c):
        rs = slice(c * C, (c + 1) * C)
        pair = (c * C) // LANES
        for h in heads:
            qs, vb = pending[h]
            zero = jnp.zeros_like(vb)
            outs[h].append(qs + jnp.dot(intras_prev[h][rs], jnp.concatenate([vb, vb], axis=0),
                                        preferred_element_type=F32))
            vpad = jnp.concatenate([vb, zero] if (c * C) % LANES == 0 else [zero, vb], axis=0)
            g_last = sm_prev[(c + 1) * C - 1:(c + 1) * C, DECAY_ROW + h:DECAY_ROW + h + 1]
            states[h] = states[h] * jnp.exp(g_last) + jnp.dot(
                kd_ts_prev[h][:, pair * LANES:(pair + 1) * LANES], vpad, preferred_element_type=F32)

    recur = [f for c in range(G // C) for f in (functools.partial(recur_first, c),
                                                functools.partial(recur_second, c))]

    ri = lax.broadcasted_iota(jnp.int32, (G, G), 0)
    ci = lax.broadcasted_iota(jnp.int32, (G, G), 1)
    same_chunk = (ri // C) == (ci // C)
    lower = same_chunk & (ri >= ci)
    strict = same_chunk & (ri > ci)
    sm = scol_ref[...]
    sm_t = sm.T
    lmats, intras, ys, qes, kd_ts = [], [], [], [], []
    for h in heads:
        q = q_ref[:, col(h)]
        k = k_ref[:, col(h)]
        beta = sm[:, BETA_ROW + h:BETA_ROW + h + 1]
        gc = sm[:, DECAY_ROW + h:DECAY_ROW + h + 1]
        rem = sm[:, REM_ROW + h:REM_ROW + h + 1]
        gc_row = sm_t[DECAY_ROW + h:DECAY_ROW + h + 1, :]
        eg = jnp.exp(gc)
        decay = jnp.exp(jnp.where(lower, gc - gc_row, NEG_INF))
        kb = k * beta
        a = _bdot_nt(jnp.concatenate([kb, q], axis=0), k)
        lmats.append(jnp.where(strict, a[0:G] * decay, 0.0).astype(BF16))
        intra = (a[G:2 * G] * decay).astype(BF16)
        intra_fold = intra[:, 0:LANES]
        for j in range(1, G // LANES):
            intra_fold = intra_fold + intra[:, j * LANES:(j + 1) * LANES]
        intras.append(intra_fold)
        ys.append(jnp.concatenate([v_ref[:, col(h)] * beta, kb * eg], axis=1))
        qes.append((q * eg).astype(BF16))
        kd_ts.append((k * jnp.exp(rem)).T.astype(BF16))
    recur.pop(0)()
    ys = [ys[h] - _bdot(lmats[h], ys[h]) for h in heads]
    pws = [_bdot(lmats[h], lmats[h]).astype(BF16) for h in heads]
    recur.pop(0)()
    for step in range(5):
        ys = [ys[h] + _bdot(pws[h], ys[h]) for h in heads]
        if step < 4:
            pws = [_bdot(pws[h], pws[h]).astype(BF16) for h in heads]
        recur.pop(0)()
    while recur:
        recur.pop(0)()

    for h in heads:
        state_ref[h] = states[h]
        o = jnp.concatenate(outs[h], axis=0)
        o_ref[:, col(h)] = (_rms(o, og_ref[...]) * z_ref[:, col(h)]).astype(BF16)
        u_ref[cur, :, col(h)] = ys[h][:, 0:Dh]
        w_ref[cur, :, col(h)] = ys[h][:, Dh:2 * Dh].astype(BF16)
        qe_ref[cur, :, col(h)] = qes[h]
        intra_ref[cur, :, col(h)] = intras[h]
        kdt_ref[cur, h] = kd_ts[h]
    sm_ref[cur] = sm


def _gdn(gq, gk, gv, z, scol, og, *, batch, seq):
    T = gq.shape[0]
    G = min(seq, GDN_GROUP)
    ng = seq // G
    n = batch * ng
    cur_spec = lambda width: pl.BlockSpec((G, width), lambda t: (jnp.minimum(t, n - 1), 0))
    prev_spec = lambda width: pl.BlockSpec((G, width), lambda t: (jnp.maximum(t - 1, 0), 0))
    return pl.pallas_call(
        functools.partial(_gdn_kernel, groups_per_seq=ng),
        grid=(n + 1,),
        in_specs=[cur_spec(GDN_WIDTH)] * 3 + [cur_spec(LANES), prev_spec(GDN_WIDTH),
                                              _const_spec((1, GDN_HEAD_DIM))],
        out_specs=prev_spec(GDN_WIDTH),
        out_shape=jax.ShapeDtypeStruct((T, GDN_WIDTH), BF16),
        scratch_shapes=[pltpu.VMEM((GDN_HEADS, GDN_HEAD_DIM, GDN_HEAD_DIM), F32),
                        pltpu.VMEM((2, G, GDN_WIDTH), F32), pltpu.VMEM((2, G, GDN_WIDTH), BF16),
                        pltpu.VMEM((2, G, GDN_WIDTH), BF16), pltpu.VMEM((2, G, GDN_WIDTH), BF16),
                        pltpu.VMEM((2, GDN_HEADS, GDN_HEAD_DIM, G), BF16), pltpu.VMEM((2, G, LANES), F32)],
        compiler_params=_params(1),
        name="gdn",
    )(gq, gk, gv, scol, z, og)


def _mix_out_kernel(x_ref, oa_ref, ob_ref, wout_ref, gx_ref, wcq_ref, cqg_ref, ck_ref, cv_ref,
                    wco_ref, o_ref):
    o = jnp.concatenate([oa_ref[...], ob_ref[...]], axis=-1)
    x1 = x_ref[...] + jnp.dot(o, wout_ref[...], preferred_element_type=F32)
    hq = _rms(x1, gx_ref[...])
    cq = _bdot(hq, wcq_ref[...])
    heads = []
    for h in range(XATTN_HEADS):
        sl = slice(h * XATTN_HEAD_DIM, (h + 1) * XATTN_HEAD_DIM)
        s = _bdot_nt(_rms(cq[:, sl], cqg_ref[...]), ck_ref[0, :, sl])
        p = jnp.exp(s - jnp.max(s, axis=-1, keepdims=True))
        pv = jnp.dot(p.astype(BF16), cv_ref[0, :, sl], preferred_element_type=F32)
        heads.append(pv / jnp.sum(p, axis=-1, keepdims=True))
    co = jnp.concatenate(heads, axis=-1)
    o_ref[...] = x1 + _bdot(co, wco_ref[...])


def _mix_out(x2d, oa, ob, w_out, gx, w_cq, cqg, ck, cv, w_co, *, seq, tm):
    T, D = x2d.shape
    M = ck.shape[1]
    tiles_per_seq = seq // tm
    row_spec = lambda width: pl.BlockSpec((tm, width), lambda i: (i, 0))
    mem_spec = pl.BlockSpec((1, M, XATTN_WIDTH), lambda i: (i // tiles_per_seq, 0, 0))
    return pl.pallas_call(
        _mix_out_kernel,
        grid=(T // tm,),
        in_specs=[row_spec(D), row_spec(FOX_WIDTH), row_spec(GDN_WIDTH), _const_spec(w_out.shape),
                  _const_spec((1, D)), _const_spec(w_cq.shape), _const_spec((1, XATTN_HEAD_DIM)),
                  mem_spec, mem_spec, _const_spec(w_co.shape)],
        out_specs=row_spec(D),
        out_shape=jax.ShapeDtypeStruct((T, D), F32),
        compiler_params=_params(1),
        name="mix_out",
    )(x2d, oa, ob, w_out, gx, w_cq, cqg, ck, cv, w_co)


def _mlp_kernel(x_ref, g_ref, w1_ref, w2_ref, o_ref, *, ff_block):
    x = x_ref[...]
    hb = _rms(x, g_ref[...]).astype(BF16)
    acc = x
    for c in range(w1_ref.shape[1] // ff_block):
        sl = slice(c * ff_block, (c + 1) * ff_block)
        a = jnp.maximum(jnp.dot(hb, w1_ref[:, sl], preferred_element_type=F32), 0.0)
        acc = acc + jnp.dot((a * a).astype(BF16), w2_ref[sl, :], preferred_element_type=F32)
    o_ref[...] = acc


def _mlp(x2d, g, w1, w2, *, tm, ff_block):
    T, D = x2d.shape
    row_spec = pl.BlockSpec((tm, D), lambda i: (i, 0))
    return pl.pallas_call(
        functools.partial(_mlp_kernel, ff_block=ff_block),
        grid=(T // tm,),
        in_specs=[row_spec, _const_spec((1, D)), _const_spec(w1.shape), _const_spec(w2.shape)],
        out_specs=row_spec,
        out_shape=jax.ShapeDtypeStruct((T, D), F32),
        compiler_params=_params(1),
        name="mlp",
    )(x2d, g, w1, w2)


def _row_tile(seq):
    return min(seq, 512)


def _layer(x, mem, norm_mix_g, w_in, fox_qnorm_g, fox_knorm_g, fox_f_bias, fox_onorm_g,
           gdn_conv_w, gdn_A_log, gdn_dt_bias, gdn_onorm_g, w_out,
           norm_xattn_g, mem_norm_g, w_cq, w_ckv, xattn_qnorm_g, xattn_knorm_g, w_co,
           norm_mlp_g, w_mlp1, w_mlp2):
    B, S, D = x.shape
    T = B * S
    tm = _row_tile(S)
    row = lambda v: v.reshape(1, -1).astype(F32)

    o_f = 3 * FOX_WIDTH
    o_g = o_f + FOX_HEADS
    o_b = o_g + 3 * GDN_WIDTH
    o_a = o_b + GDN_HEADS
    o_z = o_a + GDN_HEADS
    w_small = jnp.concatenate([w_in[:, o_f:o_g], w_in[:, o_b:o_z], w_in[:, o_a:o_z]], axis=1)
    w_small = jnp.pad(w_small, ((0, 0), (0, LANES - w_small.shape[1])))
    w_all = jnp.concatenate([w_in[:, :o_f], w_in[:, o_g:o_b], w_in[:, o_z:], w_small], axis=1).astype(BF16)
    pad4 = jnp.zeros((GDN_HEADS,), F32)
    bias_col = jnp.concatenate([fox_f_bias, pad4, gdn_dt_bias, gdn_dt_bias, pad4]).reshape(-1, 1)
    alog_col = jnp.concatenate([jnp.zeros((DECAY_ROW,), F32), gdn_A_log, gdn_A_log, pad4]).reshape(-1, 1)
    qg = row(jnp.tile(fox_qnorm_g, 2)) * (FOX_HEAD_DIM ** -0.5 * LOG2E)
    kg = row(jnp.tile(fox_knorm_g, 2))
    og = row(jnp.tile(fox_onorm_g, 2))

    x2d = x.reshape(T, D)
    ck, cv = _mem_kv(mem, row(mem_norm_g), w_ckv.astype(BF16), row(xattn_knorm_g))
    fq, fk, fvt, gq, gk, gv, z, scol, kx, qx = _in_proj(
        x2d, row(norm_mix_g), w_all, qg, kg, gdn_conv_w, bias_col, alog_col, seq=S, tm=tm)
    o_fox = _fox_attn(fq, qx, fk, kx, fvt, og, batch=B, seq=S)
    o_gdn = _gdn(gq, gk, gv, z, scol, row(gdn_onorm_g), batch=B, seq=S)
    x2 = _mix_out(x2d, o_fox, o_gdn, w_out.astype(BF16), row(norm_xattn_g), w_cq.astype(BF16),
                  row(xattn_qnorm_g) * (XATTN_HEAD_DIM ** -0.5), ck, cv, w_co.astype(BF16),
                  seq=S, tm=tm)
    out = _mlp(x2, row(norm_mlp_g), w_mlp1.astype(BF16), w_mlp2.astype(BF16), tm=tm, ff_block=1024)
    return out.reshape(B, S, D)


def kernel(x, mem, norm_mix_g, w_in, fox_qnorm_g, fox_knorm_g, fox_f_bias, fox_onorm_g, gdn_conv_w, gdn_A_log, gdn_dt_bias, gdn_onorm_g, w_out, norm_xattn_g, mem_norm_g, w_cq, w_ckv, xattn_qnorm_g, xattn_knorm_g, w_co, norm_mlp_g, w_mlp1, w_mlp2):
    for l in range(w_in.shape[0]):
        x = _layer(x, mem, norm_mix_g[l], w_in[l], fox_qnorm_g[l], fox_knorm_g[l], fox_f_bias[l],
                   fox_onorm_g[l], gdn_conv_w[l], gdn_A_log[l], gdn_dt_bias[l], gdn_onorm_g[l],
                   w_out[l], norm_xattn_g[l], mem_norm_g[l], w_cq[l], w_ckv[l], xattn_qnorm_g[l],
                   xattn_knorm_g[l], w_co[l], norm_mlp_g[l], w_mlp1[l], w_mlp2[l])
    return x
```

```python
import functools

import jax
import jax.numpy as jnp
from jax import lax
from jax.experimental import pallas as pl
from jax.experimental.pallas import tpu as pltpu

EPS = 1e-6
NEG_INF = -1e30
LOG2E = 1.4426950408889634

FOX_HEADS = 8
FOX_HEAD_DIM = 64
FOX_WIDTH = FOX_HEADS * FOX_HEAD_DIM
FOX_BLOCK = 256
FOX_PAIRS_PER_STEP = 4
GDN_HEADS = 4
GDN_HEAD_DIM = 128
GDN_WIDTH = GDN_HEADS * GDN_HEAD_DIM
CONV_WIDTH = 4
GDN_CHUNK = 64
GDN_GROUP = 256
XATTN_HEADS = 4
XATTN_HEAD_DIM = 128
XATTN_WIDTH = XATTN_HEADS * XATTN_HEAD_DIM

LANES = 128
SUBLANES = 8
SMALL_ROWS = 24
BETA_ROW = FOX_HEADS
DECAY_ROW = BETA_ROW + GDN_HEADS
REM_ROW = DECAY_ROW + GDN_HEADS
VMEM_LIMIT_BYTES = 56 * 1024 * 1024

F32 = jnp.float32
BF16 = jnp.bfloat16


def _bdot(a, b):
    return jnp.dot(a.astype(BF16), b.astype(BF16), preferred_element_type=F32)


def _bdot_nt(a, b):
    return lax.dot_general(a.astype(BF16), b.astype(BF16), (((1,), (1,)), ((), ())),
                           preferred_element_type=F32)


def _rms(x, gain):
    return x * lax.rsqrt(jnp.mean(x * x, axis=-1, keepdims=True) + EPS) * gain


def _half_lane_rms(x, gain, lane_lo):
    ss = x * x
    s_lo = jnp.sum(jnp.where(lane_lo, ss, 0.0), axis=-1, keepdims=True)
    s_hi = jnp.sum(jnp.where(lane_lo, 0.0, ss), axis=-1, keepdims=True)
    ms = jnp.where(lane_lo, s_lo, s_hi) * (1.0 / FOX_HEAD_DIM)
    return x * lax.rsqrt(ms + EPS) * gain


def _softplus(y):
    return jnp.maximum(y, 0.0) + jnp.log1p(jnp.exp(-jnp.abs(y)))


def _silu(y):
    return y / (1.0 + jnp.exp2(y * (-LOG2E)))


def _split3_bf16(v):
    p1 = v.astype(BF16)
    r1 = v - p1.astype(F32)
    p2 = r1.astype(BF16)
    r2 = r1 - p2.astype(F32)
    return p1, p2, r2.astype(BF16)


def _const_spec(shape):
    return pl.BlockSpec(shape, lambda *_: (0,) * len(shape), pipeline_mode=pl.Buffered(1))


def _params(n_grid):
    return pltpu.CompilerParams(dimension_semantics=("arbitrary",) * n_grid,
                                vmem_limit_bytes=VMEM_LIMIT_BYTES)


def _mem_kv_kernel(mem_ref, g_ref, w_ref, kg_ref, ck_ref, cv_ref):
    hm = _rms(mem_ref[0], g_ref[...])
    kv = _bdot(hm, w_ref[...])
    for h in range(XATTN_HEADS):
        sl = slice(h * XATTN_HEAD_DIM, (h + 1) * XATTN_HEAD_DIM)
        ck_ref[0, :, sl] = _rms(kv[:, sl], kg_ref[...]).astype(BF16)
    cv_ref[0] = kv[:, XATTN_WIDTH:].astype(BF16)


def _mem_kv(mem, g, w_ckv, kg):
    B, M, D = mem.shape
    return pl.pallas_call(
        _mem_kv_kernel,
        grid=(B,),
        in_specs=[pl.BlockSpec((1, M, D), lambda b: (b, 0, 0)),
                  _const_spec((1, D)), _const_spec(w_ckv.shape), _const_spec((1, XATTN_HEAD_DIM))],
        out_specs=[pl.BlockSpec((1, M, XATTN_WIDTH), lambda b: (b, 0, 0))] * 2,
        out_shape=[jax.ShapeDtypeStruct((B, M, XATTN_WIDTH), BF16)] * 2,
        compiler_params=_params(1),
        name="mem_kv",
    )(mem, g, w_ckv, kg)


def _in_proj_kernel(x_ref, g_ref, w_ref, qg_ref, kg_ref, convw_ref, bias_ref, alog_ref,
                    fq_ref, fk_ref, fvt_ref, gq_ref, gk_ref, gv_ref, z_ref, scol_ref, kx_ref, qx_ref,
                    gbuf_q, gbuf_k, gbuf_v, carry_ref, *, tiles_per_seq):
    tm = x_ref.shape[0]
    i = pl.program_id(0)
    gbufs = (gbuf_q, gbuf_k, gbuf_v)

    @pl.when(i % tiles_per_seq == 0)
    def _():
        for gbuf in gbufs:
            gbuf[0:SUBLANES, :] = jnp.zeros((SUBLANES, GDN_WIDTH), F32)
        carry_ref[...] = jnp.zeros(carry_ref.shape, F32)

    hb = _rms(x_ref[...], g_ref[...]).astype(BF16)
    lane_lo = lax.broadcasted_iota(jnp.int32, (1, LANES), 1) < FOX_HEAD_DIM
    off_gdn = 3 * FOX_WIDTH
    off_z = off_gdn + 3 * GDN_WIDTH
    off_small = off_z + GDN_WIDTH
    proj = lambda off, width: jnp.dot(hb, w_ref[:, off:off + width], preferred_element_type=F32)

    def gdn_proj(part):
        gbufs[part][SUBLANES:SUBLANES + tm, :] = proj(off_gdn + part * GDN_WIDTH, GDN_WIDTH)

    def gdn_head(part, h):
        gbuf, out_ref = gbufs[part], (gq_ref, gk_ref, gv_ref)[part]
        cs = slice(h * GDN_HEAD_DIM, (h + 1) * GDN_HEAD_DIM)
        ws = slice(part * GDN_WIDTH + h * GDN_HEAD_DIM, part * GDN_WIDTH + (h + 1) * GDN_HEAD_DIM)
        acc = convw_ref[CONV_WIDTH - 1:CONV_WIDTH, ws] * gbuf[SUBLANES:SUBLANES + tm, cs]
        for d in range(1, CONV_WIDTH):
            acc = acc + (convw_ref[CONV_WIDTH - 1 - d:CONV_WIDTH - d, ws]
                         * gbuf[SUBLANES - d:SUBLANES - d + tm, cs])
        y = _silu(acc)
        if part == 0:
            y = y * lax.rsqrt(jnp.sum(y * y, axis=-1, keepdims=True) + EPS) * (GDN_HEAD_DIM ** -0.5)
        elif part == 1:
            y = y * lax.rsqrt(jnp.sum(y * y, axis=-1, keepdims=True) + EPS)
        out_ref[:, cs] = y
        gbuf[0:SUBLANES, cs] = gbuf[tm:tm + SUBLANES, cs]

    MXU_COLS = 2 * LANES

    def fox_qk(out_ref, gain_ref, off, c0):
        y = proj(off + c0, MXU_COLS)
        for j in range(MXU_COLS // LANES):
            sl = slice(c0 + j * LANES, c0 + (j + 1) * LANES)
            out_ref[:, sl] = _half_lane_rms(y[:, j * LANES:(j + 1) * LANES], gain_ref[...],
                                            lane_lo).astype(BF16)

    def fox_v(c0):
        y = proj(2 * FOX_WIDTH + c0, MXU_COLS)
        for r in range(tm // FOX_BLOCK):
            fvt_ref[r, c0:c0 + MXU_COLS, :] = y[r * FOX_BLOCK:(r + 1) * FOX_BLOCK, :].T.astype(BF16)

    def gate(c0):
        z_ref[:, c0:c0 + MXU_COLS] = _silu(proj(off_z + c0, MXU_COLS))

    row = lax.broadcasted_iota(jnp.int32, (SMALL_ROWS, tm), 0)
    is_fox = row < BETA_ROW
    is_beta = row < DECAY_ROW

    def scalars_gates():
        xb = proj(off_small, LANES).T[0:SMALL_ROWS, :] + bias_ref[...]
        sp = _softplus(jnp.where(is_fox, -xb, xb))
        return jnp.where(is_fox, -sp,
                         jnp.where(is_beta, 1.0 / (1.0 + jnp.exp(-xb)), -jnp.exp(alog_ref[...]) * sp))

    def scalars_scans(val):
        ti = lax.broadcasted_iota(jnp.int32, (tm, tm), 0)
        tj = lax.broadcasted_iota(jnp.int32, (tm, tm), 1)
        same_chunk = (ti // GDN_CHUNK) == (tj // GDN_CHUNK)
        one_hot = lambda m: jnp.where(m, 1.0, 0.0).astype(BF16)
        scan_mats = jnp.concatenate([one_hot(ti <= tj), one_hot((ti <= tj) & same_chunk),
                                     one_hot((ti > tj) & same_chunk)], axis=1)
        sums = jnp.zeros((SMALL_ROWS, 3 * tm), F32)
        for part in _split3_bf16(val):
            sums = sums + jnp.dot(part, scan_mats, preferred_element_type=F32)
        cs_full = sums[:, 0:tm] + carry_ref[:, 0:1]
        carry_ref[...] = jnp.broadcast_to(cs_full[:, tm - 1:tm], carry_ref.shape)
        return jnp.where(is_fox, cs_full, jnp.where(is_beta, val, jnp.where(
            row < REM_ROW, sums[:, tm:2 * tm], sums[:, 2 * tm:])))

    def scalars_columns(rows):
        padded = jnp.concatenate([rows, jnp.zeros((LANES - SMALL_ROWS, tm), F32)], axis=0)
        scol_ref[...] = padded.T

    def scalars_bias_lanes(rows):
        c_parts = [p.astype(F32) for p in _split3_bf16(rows[0:FOX_HEADS, :] * LOG2E)]
        zeros = lambda n: jnp.zeros((n, tm), F32)
        ones = jnp.ones((3 * FOX_HEADS, tm), F32)
        kx = jnp.concatenate([-p for p in c_parts] + [zeros(FOX_HEADS), ones, zeros(LANES - 7 * FOX_HEADS)],
                             axis=0)
        qx = jnp.concatenate([ones, zeros(FOX_HEADS)] + c_parts + [zeros(LANES - 7 * FOX_HEADS)], axis=0)
        kx_ref[...] = kx.T.astype(BF16)
        qx_ref[...] = qx.T.astype(BF16)

    val = scalars_gates()
    gdn_proj(0)
    gdn_proj(1)
    gdn_proj(2)
    rows = scalars_scans(val)
    light = ([functools.partial(fox_qk, fq_ref, qg_ref, 0, c) for c in range(0, FOX_WIDTH, MXU_COLS)]
             + [functools.partial(fox_qk, fk_ref, kg_ref, FOX_WIDTH, c) for c in range(0, FOX_WIDTH, MXU_COLS)]
             + [functools.partial(scalars_columns, rows)]
             + [functools.partial(fox_v, c) for c in range(0, FOX_WIDTH, MXU_COLS)]
             + [functools.partial(scalars_bias_lanes, rows)]
             + [functools.partial(gate, c) for c in range(0, GDN_WIDTH, MXU_COLS)])
    for part in range(3):
        for h in range(GDN_HEADS):
            gdn_head(part, h)
            if light:
                light.pop(0)()
    while light:
        light.pop(0)()


def _in_proj(x2d, g, w_all, qg, kg, conv_w, bias_col, alog_col, *, seq, tm):
    T, D = x2d.shape
    n = T // tm
    row_spec = lambda width: pl.BlockSpec((tm, width), lambda i: (i, 0))
    kernel = functools.partial(_in_proj_kernel, tiles_per_seq=seq // tm)
    return pl.pallas_call(
        kernel,
        grid=(n,),
        in_specs=[row_spec(D), _const_spec((1, D)), _const_spec(w_all.shape),
                  _const_spec((1, LANES)), _const_spec((1, LANES)), _const_spec(conv_w.shape),
                  _const_spec((SMALL_ROWS, 1)), _const_spec((SMALL_ROWS, 1))],
        out_specs=[row_spec(FOX_WIDTH)] * 2
        + [pl.BlockSpec((tm // FOX_BLOCK, FOX_WIDTH, FOX_BLOCK), lambda i: (i, 0, 0))]
        + [row_spec(GDN_WIDTH)] * 4 + [row_spec(LANES)] * 3,
        out_shape=[jax.ShapeDtypeStruct((T, FOX_WIDTH), BF16)] * 2
        + [jax.ShapeDtypeStruct((T // FOX_BLOCK, FOX_WIDTH, FOX_BLOCK), BF16)]
        + [jax.ShapeDtypeStruct((T, GDN_WIDTH), F32)] * 4
        + [jax.ShapeDtypeStruct((T, LANES), F32)] + [jax.ShapeDtypeStruct((T, LANES), BF16)] * 2,
        scratch_shapes=[pltpu.VMEM((SUBLANES + tm, GDN_WIDTH), F32)] * 3
        + [pltpu.VMEM((SMALL_ROWS, LANES), F32)],
        compiler_params=_params(1),
        name="in_proj",
    )(x2d, g, w_all, qg, kg, conv_w, bias_col, alog_col)


def _fox_kernel(q_ref, qx_ref, k_ref, kx_ref, vt_ref, og_ref, o_ref, sa_ref, sb_ref, *, pairs):
    t = q_ref.shape[0]
    Dh = FOX_HEAD_DIM
    qi = pl.program_id(2)
    lane = lax.broadcasted_iota(jnp.int32, (1, LANES), 1)
    lane_lo = lane < Dh
    in_bias_lanes = (lane < 3 * FOX_HEADS) | ((lane >= 4 * FOX_HEADS) & (lane < 7 * FOX_HEADS))
    qx = qx_ref[...]
    chains = [(p, hh) for p in range(pairs) for hh in range(2)]
    q_aug = []
    for p, hh in chains:
        h = (pl.program_id(1) * pairs + p) * 2 + hh
        q = q_ref[:, p * LANES:(p + 1) * LANES]
        q = jnp.where(lane_lo if hh == 0 else jnp.logical_not(lane_lo), q, jnp.zeros_like(q))
        qx_h = jnp.where(in_bias_lanes & (lane % FOX_HEADS == h), qx, jnp.zeros_like(qx))
        q_aug.append(jnp.concatenate([q, qx_h], axis=1))

    n_chains = len(chains)
    lookahead = 2

    def stage(j_new, keep_all, new_ref, j_old, old_ref, maxima, carry):
        if j_new is not None:
            r0 = pl.multiple_of(j_new * t, t)
            kx = kx_ref[pl.ds(r0, t), :]
            if keep_all is not True:
                visible = (lax.broadcasted_iota(jnp.int32, (t, t), 0)
                           <= lax.broadcasted_iota(jnp.int32, (t, t), 1))
                if keep_all is not False:
                    visible = visible | keep_all
        new_maxima = [None] * n_chains
        out = [None] * n_chains

        def score(i):
            p = chains[i][0]
            k_aug = jnp.concatenate([k_ref[pl.ds(r0, t), p * LANES:(p + 1) * LANES], kx], axis=1)
            s = _bdot_nt(k_aug, q_aug[i])
            if keep_all is not True:
                s = jnp.where(visible, s, NEG_INF)
            new_ref[i] = s
            new_maxima[i] = jnp.max(s, axis=0, keepdims=True)

        def fold(i):
            p, hh = chains[i]
            m, l, acc = carry[i]
            m_new = jnp.maximum(m, maxima[i])
            alpha = jnp.exp2(m - m_new)
            prob = jnp.exp2(old_ref[i] - m_new)
            r = (p * 2 + hh) * Dh
            pv = jnp.dot(vt_ref[j_old, r:r + Dh, :], prob.astype(BF16), preferred_element_type=F32)
            out[i] = (m_new, alpha * l + jnp.sum(prob, axis=0, keepdims=True), alpha * acc + pv)

        if j_new is not None:
            for i in range(min(lookahead, n_chains)):
                score(i)
        for i in range(n_chains):
            if j_old is not None:
                fold(i)
            if j_new is not None and i + lookahead < n_chains:
                score(i + lookahead)
        return (tuple(new_maxima) if j_new is not None else None,
                tuple(out) if j_old is not None else carry)

    def steady(i, state):
        maxima, carry = state
        maxima, carry = stage(2 * i + 1, True, sb_ref, 2 * i, sa_ref, maxima, carry)
        return stage(2 * i + 2, True, sa_ref, 2 * i + 1, sb_ref, maxima, carry)

    def tail0(state):
        return stage(None, None, None, qi, sa_ref, *state)[1]

    def tail1(state):
        maxima, carry = stage(qi, False, sb_ref, qi - 1, sa_ref, *state)
        return stage(None, None, None, qi, sb_ref, maxima, carry)[1]

    def tail2(state):
        maxima, carry = stage(qi - 1, True, sb_ref, qi - 2, sa_ref, *state)
        maxima, carry = stage(qi, False, sa_ref, qi - 1, sb_ref, maxima, carry)
        return stage(None, None, None, qi, sa_ref, maxima, carry)[1]

    init = tuple((jnp.full((1, t), NEG_INF, F32), jnp.zeros((1, t), F32), jnp.zeros((Dh, t), F32))
                 for _ in chains)
    maxima, _ = stage(0, qi > 0, sa_ref, None, None, None, init)
    trips = jnp.maximum(qi - 1, 0) // 2
    state = lax.fori_loop(0, trips, steady, (maxima, init))
    carry = lax.switch(qi - 2 * trips, [tail0, tail1, tail2], state)
    for p in range(pairs):
        (_, l0, a0), (_, l1, a1) = carry[2 * p], carry[2 * p + 1]
        o = jnp.concatenate([a0 / l0, a1 / l1], axis=0).T
        o_ref[:, p * LANES:(p + 1) * LANES] = _half_lane_rms(o, og_ref[...], lane_lo).astype(BF16)


def _fox_attn(fq, qx, fk, kx, fvt, og, *, batch, seq):
    T = fq.shape[0]
    t = FOX_BLOCK
    nq = seq // t
    pairs = FOX_PAIRS_PER_STEP
    width = pairs * LANES
    q_spec = lambda w: pl.BlockSpec((t, w), lambda b, g, qi: (b * nq + qi, g))
    return pl.pallas_call(
        functools.partial(_fox_kernel, pairs=pairs),
        grid=(batch, FOX_WIDTH // width, nq),
        in_specs=[q_spec(width),
                  pl.BlockSpec((t, LANES), lambda b, g, qi: (b * nq + qi, 0)),
                  pl.BlockSpec((seq, width), lambda b, g, qi: (b, g)),
                  pl.BlockSpec((seq, LANES), lambda b, g, qi: (b, 0)),
                  pl.BlockSpec((nq, width, t), lambda b, g, qi: (b, g, 0)),
                  _const_spec((1, LANES))],
        out_specs=q_spec(width),
        out_shape=jax.ShapeDtypeStruct((T, FOX_WIDTH), BF16),
        scratch_shapes=[pltpu.VMEM((2 * pairs, t, t), F32)] * 2,
        compiler_params=_params(3),
        name="fox_attn",
    )(fq, qx, fk, kx, fvt, og)


def _gdn_kernel(q_ref, k_ref, v_ref, scol_ref, z_ref, og_ref, o_ref,
                state_ref, u_ref, w_ref, qe_ref, intra_ref, kdt_ref, sm_ref, *, groups_per_seq):
    C = GDN_CHUNK
    G = q_ref.shape[0]
    Dh = GDN_HEAD_DIM
    t = pl.program_id(0)
    heads = range(GDN_HEADS)
    col = lambda h: slice(h * Dh, (h + 1) * Dh)

    @pl.when(t == 0)
    def _():
        for ref in (u_ref, w_ref, qe_ref, intra_ref, kdt_ref, sm_ref):
            ref[...] = jnp.zeros(ref.shape, ref.dtype)

    @pl.when((t == 0) | ((t - 1) % groups_per_seq == 0))
    def _():
        state_ref[...] = jnp.zeros(state_ref.shape, F32)

    cur = t % 2
    prev = 1 - cur
    us = [u_ref[prev, :, col(h)] for h in heads]
    ws16 = [w_ref[prev, :, col(h)] for h in heads]
    qes_prev = [qe_ref[prev, :, col(h)] for h in heads]
    intras_prev = [intra_ref[prev, :, col(h)] for h in heads]
    kd_ts_prev = [kdt_ref[prev, h] for h in heads]
    sm_prev = sm_ref[prev]
    states = [state_ref[h] for h in heads]
    outs = [[] for _ in heads]
    pending = {}

    def recur_first(c):
        rs = slice(c * C, (c + 1) * C)
        for h in heads:
            ws = _bdot(jnp.concatenate([ws16[h][rs], qes_prev[h][rs]], axis=0), states[h])
            pending[h] = (ws[C:2 * C], (us[h][rs] - ws[0:C]).astype(BF16))

    def recur_second(c):
        rs = slice(c * C, (c + 1) * C)
        pair = (c * C) // LANES
        for h in heads:
            qs, vb = pending[h]
            zero = jnp.zeros_like(vb)
            outs[h].append(qs + jnp.dot(intras_prev[h][rs], jnp.concatenate([vb, vb], axis=0),
                                        preferred_element_type=F32))
            vpad = jnp.concatenate([vb, zero] if (c * C) % LANES == 0 else [zero, vb], axis=0)
            g_last = sm_prev[(c + 1) * C - 1:(c + 1) * C, DECAY_ROW + h:DECAY_ROW + h + 1]
            states[h] = states[h] * jnp.exp(g_last) + jnp.dot(
                kd_ts_prev[h][:, pair * LANES:(pair + 1) * LANES], vpad, preferred_element_type=F32)

    recur = [f for c in range(G // C) for f in (functools.partial(recur_first, c),
                                                functools.partial(recur_second, c))]

    ri = lax.broadcasted_iota(jnp.int32, (G, G), 0)
    ci = lax.broadcasted_iota(jnp.int32, (G, G), 1)
    same_chunk = (ri // C) == (ci // C)
    lower = same_chunk & (ri >= ci)
    strict = same_chunk & (ri > ci)
    sm = scol_ref[...]
    sm_t = sm.T
    lmats, intras, ys, qes, kd_ts = [], [], [], [], []
    for h in heads:
        q = q_ref[:, col(h)]
        k = k_ref[:, col(h)]
        beta = sm[:, BETA_ROW + h:BETA_ROW + h + 1]
        gc = sm[:, DECAY_ROW + h:DECAY_ROW + h + 1]
        rem = sm[:, REM_ROW + h:REM_ROW + h + 1]
        gc_row = sm_t[DECAY_ROW + h:DECAY_ROW + h + 1, :]
        eg = jnp.exp(gc)
        decay = jnp.exp(jnp.where(lower, gc - gc_row, NEG_INF))
        kb = k * beta
        a = _bdot_nt(jnp.concatenate([kb, q], axis=0), k)
        lmats.append(jnp.where(strict, a[0:G] * decay, 0.0).astype(BF16))
        intra = (a[G:2 * G] * decay).astype(BF16)
        intra_fold = intra[:, 0:LANES]
        for j in range(1, G // LANES):
            intra_fold = intra_fold + intra[:, j * LANES:(j + 1) * LANES]
        intras.append(intra_fold)
        ys.append(jnp.concatenate([v_ref[:, col(h)] * beta, kb * eg], axis=1))
        qes.append((q * eg).astype(BF16))
        kd_ts.append((k * jnp.exp(rem)).T.astype(BF16))
    recur.pop(0)()
    ys = [ys[h] - _bdot(lmats[h], ys[h]) for h in heads]
    pws = [_bdot(lmats[h], lmats[h]).astype(BF16) for h in heads]
    recur.pop(0)()
    for step in range(5):
        ys = [ys[h] + _bdot(pws[h], ys[h]) for h in heads]
        if step < 4:
            pws = [_bdot(pws[h], pws[h]).astype(BF16) for h in heads]
        recur.pop(0)()
    while recur:
        recur.pop(0)()

    for h in heads:
        state_ref[h] = states[h]
        o = jnp.concatenate(outs[h], axis=0)
        o_ref[:, col(h)] = (_rms(o, og_ref[...]) * z_ref[:, col(h)]).astype(BF16)
        u_ref[cur, :, col(h)] = ys[h][:, 0:Dh]
        w_ref[cur, :, col(h)] = ys[h][:, Dh:2 * Dh].astype(BF16)
        qe_ref[cur, :, col(h)] = qes[h]
        intra_ref[cur, :, col(h)] = intras[h]
        kdt_ref[cur, h] = kd_ts[h]
    sm_ref[cur] = sm


def _gdn(gq, gk, gv, z, scol, og, *, batch, seq):
    T = gq.shape[0]
    G = min(seq, GDN_GROUP)
    ng = seq // G
    n = batch * ng
    cur_spec = lambda width: pl.BlockSpec((G, width), lambda t: (jnp.minimum(t, n - 1), 0))
    prev_spec = lambda width: pl.BlockSpec((G, width), lambda t: (jnp.maximum(t - 1, 0), 0))
    return pl.pallas_call(
        functools.partial(_gdn_kernel, groups_per_seq=ng),
        grid=(n + 1,),
        in_specs=[cur_spec(GDN_WIDTH)] * 3 + [cur_spec(LANES), prev_spec(GDN_WIDTH),
                                              _const_spec((1, GDN_HEAD_DIM))],
        out_specs=prev_spec(GDN_WIDTH),
        out_shape=jax.ShapeDtypeStruct((T, GDN_WIDTH), BF16),
        scratch_shapes=[pltpu.VMEM((GDN_HEADS, GDN_HEAD_DIM, GDN_HEAD_DIM), F32),
                        pltpu.VMEM((2, G, GDN_WIDTH), F32), pltpu.VMEM((2, G, GDN_WIDTH), BF16),
                        pltpu.VMEM((2, G, GDN_WIDTH), BF16), pltpu.VMEM((2, G, GDN_WIDTH), BF16),
                        pltpu.VMEM((2, GDN_HEADS, GDN_HEAD_DIM, G), BF16), pltpu.VMEM((2, G, LANES), F32)],
        compiler_params=_params(1),
        name="gdn",
    )(gq, gk, gv, scol, z, og)


def _mix_mlp_kernel(x_ref, oa_ref, ob_ref, wout_ref, gx_ref, wcq_ref, cqg_ref, ck_ref, cv_ref,
                    wco_ref, gm_ref, w1_ref, w2_ref, o_ref, *, ff_block):
    o = jnp.concatenate([oa_ref[...], ob_ref[...]], axis=-1)
    x1 = x_ref[...] + jnp.dot(o, wout_ref[...], preferred_element_type=F32)
    hq = _rms(x1, gx_ref[...])
    cq = _bdot(hq, wcq_ref[...])
    heads = []
    for h in range(XATTN_HEADS):
        sl = slice(h * XATTN_HEAD_DIM, (h + 1) * XATTN_HEAD_DIM)
        s = _bdot_nt(_rms(cq[:, sl], cqg_ref[...]), ck_ref[0, :, sl])
        p = jnp.exp(s - jnp.max(s, axis=-1, keepdims=True))
        pv = jnp.dot(p.astype(BF16), cv_ref[0, :, sl], preferred_element_type=F32)
        heads.append(pv / jnp.sum(p, axis=-1, keepdims=True))
    co = jnp.concatenate(heads, axis=-1)
    x2 = x1 + _bdot(co, wco_ref[...])
    hb = _rms(x2, gm_ref[...]).astype(BF16)
    acc = x2
    for c in range(w1_ref.shape[1] // ff_block):
        sl = slice(c * ff_block, (c + 1) * ff_block)
        a = jnp.maximum(jnp.dot(hb, w1_ref[:, sl], preferred_element_type=F32), 0.0)
        acc = acc + jnp.dot((a * a).astype(BF16), w2_ref[sl, :], preferred_element_type=F32)
    o_ref[...] = acc


def _mix_mlp(x2d, oa, ob, w_out, gx, w_cq, cqg, ck, cv, w_co, gm, w1, w2, *, seq, tm, ff_block):
    T, D = x2d.shape
    M = ck.shape[1]
    tiles_per_seq = seq // tm
    row_spec = lambda width: pl.BlockSpec((tm, width), lambda i: (i, 0))
    mem_spec = pl.BlockSpec((1, M, XATTN_WIDTH), lambda i: (i // tiles_per_seq, 0, 0))
    return pl.pallas_call(
        functools.partial(_mix_mlp_kernel, ff_block=ff_block),
        grid=(T // tm,),
        in_specs=[row_spec(D), row_spec(FOX_WIDTH), row_spec(GDN_WIDTH), _const_spec(w_out.shape),
                  _const_spec((1, D)), _const_spec(w_cq.shape), _const_spec((1, XATTN_HEAD_DIM)),
                  mem_spec, mem_spec, _const_spec(w_co.shape),
                  _const_spec((1, D)), _const_spec(w1.shape), _const_spec(w2.shape)],
        out_specs=row_spec(D),
        out_shape=jax.ShapeDtypeStruct((T, D), F32),
        compiler_params=_params(1),
        name="mix_mlp",
    )(x2d, oa, ob, w_out, gx, w_cq, cqg, ck, cv, w_co, gm, w1, w2)


def _row_tile(seq):
    return min(seq, 512)


def _layer(x, mem, norm_mix_g, w_in, fox_qnorm_g, fox_knorm_g, fox_f_bias, fox_onorm_g,
           gdn_conv_w, gdn_A_log, gdn_dt_bias, gdn_onorm_g, w_out,
           norm_xattn_g, mem_norm_g, w_cq, w_ckv, xattn_qnorm_g, xattn_knorm_g, w_co,
           norm_mlp_g, w_mlp1, w_mlp2):
    B, S, D = x.shape
    T = B * S
    tm = _row_tile(S)
    row = lambda v: v.reshape(1, -1).astype(F32)

    o_f = 3 * FOX_WIDTH
    o_g = o_f + FOX_HEADS
    o_b = o_g + 3 * GDN_WIDTH
    o_a = o_b + GDN_HEADS
    o_z = o_a + GDN_HEADS
    w_small = jnp.concatenate([w_in[:, o_f:o_g], w_in[:, o_b:o_z], w_in[:, o_a:o_z]], axis=1)
    w_small = jnp.pad(w_small, ((0, 0), (0, LANES - w_small.shape[1])))
    w_all = jnp.concatenate([w_in[:, :o_f], w_in[:, o_g:o_b], w_in[:, o_z:], w_small], axis=1).astype(BF16)
    pad4 = jnp.zeros((GDN_HEADS,), F32)
    bias_col = jnp.concatenate([fox_f_bias, pad4, gdn_dt_bias, gdn_dt_bias, pad4]).reshape(-1, 1)
    alog_col = jnp.concatenate([jnp.zeros((DECAY_ROW,), F32), gdn_A_log, gdn_A_log, pad4]).reshape(-1, 1)
    qg = row(jnp.tile(fox_qnorm_g, 2)) * (FOX_HEAD_DIM ** -0.5 * LOG2E)
    kg = row(jnp.tile(fox_knorm_g, 2))
    og = row(jnp.tile(fox_onorm_g, 2))

    x2d = x.reshape(T, D)
    ck, cv = _mem_kv(mem, row(mem_norm_g), w_ckv.astype(BF16), row(xattn_knorm_g))
    fq, fk, fvt, gq, gk, gv, z, scol, kx, qx = _in_proj(
        x2d, row(norm_mix_g), w_all, qg, kg, gdn_conv_w, bias_col, alog_col, seq=S, tm=tm)
    o_fox = _fox_attn(fq, qx, fk, kx, fvt, og, batch=B, seq=S)
    o_gdn = _gdn(gq, gk, gv, z, scol, row(gdn_onorm_g), batch=B, seq=S)
    out = _mix_mlp(x2d, o_fox, o_gdn, w_out.astype(BF16), row(norm_xattn_g), w_cq.astype(BF16),
                   row(xattn_qnorm_g) * (XATTN_HEAD_DIM ** -0.5), ck, cv, w_co.astype(BF16),
                   row(norm_mlp_g), w_mlp1.astype(BF16), w_mlp2.astype(BF16), seq=S, tm=tm, ff_block=1024)
    return out.reshape(B, S, D)


def kernel(x, mem, norm_mix_g, w_in, fox_qnorm_g, fox_knorm_g, fox_f_bias, fox_onorm_g, gdn_conv_w, gdn_A_log, gdn_dt_bias, gdn_onorm_g, w_out, norm_xattn_g, mem_norm_g, w_cq, w_ckv, xattn_qnorm_g, xattn_knorm_g, w_co, norm_mlp_g, w_mlp1, w_mlp2):
    for l in range(w_in.shape[0]):
        x = _layer(x, mem, norm_mix_g[l], w_in[l], fox_qnorm_g[l], fox_knorm_g[l], fox_f_bias[l],
                   fox_onorm_g[l], gdn_conv_w[l], gdn_A_log[l], gdn_dt_bias[l], gdn_onorm_g[l],
                   w_out[l], norm_xattn_g[l], mem_norm_g[l], w_cq[l], w_ckv[l], xattn_qnorm_g[l],
                   xattn_knorm_g[l], w_co[l], norm_mlp_g[l], w_mlp1[l], w_mlp2[l])
    return x
```

```python
import functools

import jax
import jax.numpy as jnp
from jax import lax
from jax.experimental import pallas as pl
from jax.experimental.pallas import tpu as pltpu

EPS = 1e-6
NEG_INF = -1e30
LOG2E = 1.4426950408889634

FOX_HEADS = 8
FOX_HEAD_DIM = 64
FOX_WIDTH = FOX_HEADS * FOX_HEAD_DIM
FOX_BLOCK = 256
FOX_PAIRS_PER_STEP = 4
GDN_HEADS = 4
GDN_HEAD_DIM = 128
GDN_WIDTH = GDN_HEADS * GDN_HEAD_DIM
CONV_WIDTH = 4
GDN_CHUNK = 64
GDN_GROUP = 256
XATTN_HEADS = 4
XATTN_HEAD_DIM = 128
XATTN_WIDTH = XATTN_HEADS * XATTN_HEAD_DIM

LANES = 128
SUBLANES = 8
SMALL_ROWS = 24
BETA_ROW = FOX_HEADS
DECAY_ROW = BETA_ROW + GDN_HEADS
REM_ROW = DECAY_ROW + GDN_HEADS
VMEM_LIMIT_BYTES = 56 * 1024 * 1024

F32 = jnp.float32
BF16 = jnp.bfloat16


def _bdot(a, b):
    return jnp.dot(a.astype(BF16), b.astype(BF16), preferred_element_type=F32)


def _bdot_nt(a, b):
    return lax.dot_general(a.astype(BF16), b.astype(BF16), (((1,), (1,)), ((), ())),
                           preferred_element_type=F32)


def _rms(x, gain):
    return x * lax.rsqrt(jnp.mean(x * x, axis=-1, keepdims=True) + EPS) * gain


def _half_lane_rms(x, gain, lane_lo):
    ss = x * x
    s_lo = jnp.sum(jnp.where(lane_lo, ss, 0.0), axis=-1, keepdims=True)
    s_hi = jnp.sum(jnp.where(lane_lo, 0.0, ss), axis=-1, keepdims=True)
    ms = jnp.where(lane_lo, s_lo, s_hi) * (1.0 / FOX_HEAD_DIM)
    return x * lax.rsqrt(ms + EPS) * gain


def _softplus(y):
    return jnp.maximum(y, 0.0) + jnp.log1p(jnp.exp(-jnp.abs(y)))


def _silu(y):
    return y / (1.0 + jnp.exp2(y * (-LOG2E)))


def _split3_bf16(v):
    p1 = v.astype(BF16)
    r1 = v - p1.astype(F32)
    p2 = r1.astype(BF16)
    r2 = r1 - p2.astype(F32)
    return p1, p2, r2.astype(BF16)


def _const_spec(shape):
    return pl.BlockSpec(shape, lambda *_: (0,) * len(shape), pipeline_mode=pl.Buffered(1))


def _params(n_grid):
    return pltpu.CompilerParams(dimension_semantics=("arbitrary",) * n_grid,
                                vmem_limit_bytes=VMEM_LIMIT_BYTES)


def _mem_kv_kernel(mem_ref, g_ref, w_ref, kg_ref, ck_ref, cv_ref):
    hm = _rms(mem_ref[0], g_ref[...])
    kv = _bdot(hm, w_ref[...])
    for h in range(XATTN_HEADS):
        sl = slice(h * XATTN_HEAD_DIM, (h + 1) * XATTN_HEAD_DIM)
        ck_ref[0, :, sl] = _rms(kv[:, sl], kg_ref[...]).astype(BF16)
    cv_ref[0] = kv[:, XATTN_WIDTH:].astype(BF16)


def _mem_kv(mem, g, w_ckv, kg):
    B, M, D = mem.shape
    return pl.pallas_call(
        _mem_kv_kernel,
        grid=(B,),
        in_specs=[pl.BlockSpec((1, M, D), lambda b: (b, 0, 0)),
                  _const_spec((1, D)), _const_spec(w_ckv.shape), _const_spec((1, XATTN_HEAD_DIM))],
        out_specs=[pl.BlockSpec((1, M, XATTN_WIDTH), lambda b: (b, 0, 0))] * 2,
        out_shape=[jax.ShapeDtypeStruct((B, M, XATTN_WIDTH), BF16)] * 2,
        compiler_params=_params(1),
        name="mem_kv",
    )(mem, g, w_ckv, kg)


def _in_proj_kernel(x_ref, g_ref, w_ref, qg_ref, kg_ref, convw_ref, bias_ref, alog_ref,
                    fq_ref, fk_ref, fvt_ref, gq_ref, gk_ref, gv_ref, z_ref, scol_ref, kx_ref, qx_ref,
                    gbuf_q, gbuf_k, gbuf_v, carry_ref, *, tiles_per_seq):
    tm = x_ref.shape[0]
    i = pl.program_id(0)
    gbufs = (gbuf_q, gbuf_k, gbuf_v)

    @pl.when(i % tiles_per_seq == 0)
    def _():
        for gbuf in gbufs:
            gbuf[0:SUBLANES, :] = jnp.zeros((SUBLANES, GDN_WIDTH), F32)
        carry_ref[...] = jnp.zeros(carry_ref.shape, F32)

    hb = _rms(x_ref[...], g_ref[...]).astype(BF16)
    lane_lo = lax.broadcasted_iota(jnp.int32, (1, LANES), 1) < FOX_HEAD_DIM
    off_gdn = 3 * FOX_WIDTH
    off_z = off_gdn + 3 * GDN_WIDTH
    off_small = off_z + GDN_WIDTH
    proj = lambda off, width: jnp.dot(hb, w_ref[:, off:off + width], preferred_element_type=F32)

    def gdn_proj(part):
        gbufs[part][SUBLANES:SUBLANES + tm, :] = proj(off_gdn + part * GDN_WIDTH, GDN_WIDTH)

    def gdn_head(part, h):
        gbuf, out_ref = gbufs[part], (gq_ref, gk_ref, gv_ref)[part]
        cs = slice(h * GDN_HEAD_DIM, (h + 1) * GDN_HEAD_DIM)
        ws = slice(part * GDN_WIDTH + h * GDN_HEAD_DIM, part * GDN_WIDTH + (h + 1) * GDN_HEAD_DIM)
        acc = convw_ref[CONV_WIDTH - 1:CONV_WIDTH, ws] * gbuf[SUBLANES:SUBLANES + tm, cs]
        for d in range(1, CONV_WIDTH):
            acc = acc + (convw_ref[CONV_WIDTH - 1 - d:CONV_WIDTH - d, ws]
                         * gbuf[SUBLANES - d:SUBLANES - d + tm, cs])
        y = _silu(acc)
        if part == 0:
            y = y * lax.rsqrt(jnp.sum(y * y, axis=-1, keepdims=True) + EPS) * (GDN_HEAD_DIM ** -0.5)
        elif part == 1:
            y = y * lax.rsqrt(jnp.sum(y * y, axis=-1, keepdims=True) + EPS)
        out_ref[:, cs] = y
        gbuf[0:SUBLANES, cs] = gbuf[tm:tm + SUBLANES, cs]

    MXU_COLS = 2 * LANES

    def fox_qk(out_ref, gain_ref, off, c0):
        y = proj(off + c0, MXU_COLS)
        for j in range(MXU_COLS // LANES):
            sl = slice(c0 + j * LANES, c0 + (j + 1) * LANES)
            out_ref[:, sl] = _half_lane_rms(y[:, j * LANES:(j + 1) * LANES], gain_ref[...],
                                            lane_lo).astype(BF16)

    def fox_v(c0):
        y = proj(2 * FOX_WIDTH + c0, MXU_COLS)
        for r in range(tm // FOX_BLOCK):
            fvt_ref[r, c0:c0 + MXU_COLS, :] = y[r * FOX_BLOCK:(r + 1) * FOX_BLOCK, :].T.astype(BF16)

    def gate(c0):
        z_ref[:, c0:c0 + MXU_COLS] = _silu(proj(off_z + c0, MXU_COLS))

    row = lax.broadcasted_iota(jnp.int32, (SMALL_ROWS, tm), 0)
    is_fox = row < BETA_ROW
    is_beta = row < DECAY_ROW

    def scalars_gates():
        xb = proj(off_small, LANES).T[0:SMALL_ROWS, :] + bias_ref[...]
        sp = _softplus(jnp.where(is_fox, -xb, xb))
        return jnp.where(is_fox, -sp,
                         jnp.where(is_beta, 1.0 / (1.0 + jnp.exp(-xb)), -jnp.exp(alog_ref[...]) * sp))

    def scalars_scans(val):
        ti = lax.broadcasted_iota(jnp.int32, (tm, tm), 0)
        tj = lax.broadcasted_iota(jnp.int32, (tm, tm), 1)
        same_chunk = (ti // GDN_CHUNK) == (tj // GDN_CHUNK)
        one_hot = lambda m: jnp.where(m, 1.0, 0.0).astype(BF16)
        scan_mats = jnp.concatenate([one_hot(ti <= tj), one_hot((ti <= tj) & same_chunk),
                                     one_hot((ti > tj) & same_chunk)], axis=1)
        sums = jnp.zeros((SMALL_ROWS, 3 * tm), F32)
        for part in _split3_bf16(val):
            sums = sums + jnp.dot(part, scan_mats, preferred_element_type=F32)
        cs_full = sums[:, 0:tm] + carry_ref[:, 0:1]
        carry_ref[...] = jnp.broadcast_to(cs_full[:, tm - 1:tm], carry_ref.shape)
        return jnp.where(is_fox, cs_full, jnp.where(is_beta, val, jnp.where(
            row < REM_ROW, sums[:, tm:2 * tm], sums[:, 2 * tm:])))

    def scalars_columns(rows):
        padded = jnp.concatenate([rows, jnp.zeros((LANES - SMALL_ROWS, tm), F32)], axis=0)
        scol_ref[...] = padded.T

    def scalars_bias_lanes(rows):
        c_parts = [p.astype(F32) for p in _split3_bf16(rows[0:FOX_HEADS, :] * LOG2E)]
        zeros = lambda n: jnp.zeros((n, tm), F32)
        ones = jnp.ones((3 * FOX_HEADS, tm), F32)
        kx = jnp.concatenate([-p for p in c_parts] + [zeros(FOX_HEADS), ones, zeros(LANES - 7 * FOX_HEADS)],
                             axis=0)
        qx = jnp.concatenate([ones, zeros(FOX_HEADS)] + c_parts + [zeros(LANES - 7 * FOX_HEADS)], axis=0)
        kx_ref[...] = kx.T.astype(BF16)
        qx_ref[...] = qx.T.astype(BF16)

    val = scalars_gates()
    gdn_proj(0)
    gdn_proj(1)
    gdn_proj(2)
    rows = scalars_scans(val)
    light = ([functools.partial(fox_qk, fq_ref, qg_ref, 0, c) for c in range(0, FOX_WIDTH, MXU_COLS)]
             + [functools.partial(fox_qk, fk_ref, kg_ref, FOX_WIDTH, c) for c in range(0, FOX_WIDTH, MXU_COLS)]
             + [functools.partial(scalars_columns, rows)]
             + [functools.partial(fox_v, c) for c in range(0, FOX_WIDTH, MXU_COLS)]
             + [functools.partial(scalars_bias_lanes, rows)]
             + [functools.partial(gate, c) for c in range(0, GDN_WIDTH, MXU_COLS)])
    for part in range(3):
        for h in range(GDN_HEADS):
            gdn_head(part, h)
            if light:
                light.pop(0)()
    while light:
        light.pop(0)()


def _in_proj(x2d, g, w_all, qg, kg, conv_w, bias_col, alog_col, *, seq, tm):
    T, D = x2d.shape
    n = T // tm
    row_spec = lambda width: pl.BlockSpec((tm, width), lambda i: (i, 0))
    kernel = functools.partial(_in_proj_kernel, tiles_per_seq=seq // tm)
    return pl.pallas_call(
        kernel,
        grid=(n,),
        in_specs=[row_spec(D), _const_spec((1, D)), _const_spec(w_all.shape),
                  _const_spec((1, LANES)), _const_spec((1, LANES)), _const_spec(conv_w.shape),
                  _const_spec((SMALL_ROWS, 1)), _const_spec((SMALL_ROWS, 1))],
        out_specs=[row_spec(FOX_WIDTH)] * 2
        + [pl.BlockSpec((tm // FOX_BLOCK, FOX_WIDTH, FOX_BLOCK), lambda i: (i, 0, 0))]
        + [row_spec(GDN_WIDTH)] * 4 + [row_spec(LANES)] * 3,
        out_shape=[jax.ShapeDtypeStruct((T, FOX_WIDTH), BF16)] * 2
        + [jax.ShapeDtypeStruct((T // FOX_BLOCK, FOX_WIDTH, FOX_BLOCK), BF16)]
        + [jax.ShapeDtypeStruct((T, GDN_WIDTH), F32)] * 4
        + [jax.ShapeDtypeStruct((T, LANES), F32)] + [jax.ShapeDtypeStruct((T, LANES), BF16)] * 2,
        scratch_shapes=[pltpu.VMEM((SUBLANES + tm, GDN_WIDTH), F32)] * 3
        + [pltpu.VMEM((SMALL_ROWS, LANES), F32)],
        compiler_params=_params(1),
        name="in_proj",
    )(x2d, g, w_all, qg, kg, conv_w, bias_col, alog_col)


def _fox_kernel(q_ref, qx_ref, k_ref, kx_ref, vt_ref, og_ref, o_ref, sa_ref, sb_ref, *, pairs):
    t = q_ref.shape[0]
    Dh = FOX_HEAD_DIM
    qi = pl.program_id(2)
    lane = lax.broadcasted_iota(jnp.int32, (1, LANES), 1)
    lane_lo = lane < Dh
    in_bias_lanes = (lane < 3 * FOX_HEADS) | ((lane >= 4 * FOX_HEADS) & (lane < 7 * FOX_HEADS))
    qx = qx_ref[...]
    chains = [(p, hh) for p in range(pairs) for hh in range(2)]
    q_aug = []
    for p, hh in chains:
        h = (pl.program_id(1) * pairs + p) * 2 + hh
        q = q_ref[:, p * LANES:(p + 1) * LANES]
        q = jnp.where(lane_lo if hh == 0 else jnp.logical_not(lane_lo), q, jnp.zeros_like(q))
        qx_h = jnp.where(in_bias_lanes & (lane % FOX_HEADS == h), qx, jnp.zeros_like(qx))
        q_aug.append(jnp.concatenate([q, qx_h], axis=1))

    n_chains = len(chains)
    lookahead = 2

    def stage(j_new, keep_all, new_ref, j_old, old_ref, maxima, carry):
        if j_new is not None:
            r0 = pl.multiple_of(j_new * t, t)
            kx = kx_ref[pl.ds(r0, t), :]
            if keep_all is not True:
                visible = (lax.broadcasted_iota(jnp.int32, (t, t), 0)
                           <= lax.broadcasted_iota(jnp.int32, (t, t), 1))
                if keep_all is not False:
                    visible = visible | keep_all
        new_maxima = [None] * n_chains
        out = [None] * n_chains

        def score(i):
            p = chains[i][0]
            k_aug = jnp.concatenate([k_ref[pl.ds(r0, t), p * LANES:(p + 1) * LANES], kx], axis=1)
            s = _bdot_nt(k_aug, q_aug[i])
            if keep_all is not True:
                s = jnp.where(visible, s, NEG_INF)
            new_ref[i] = s
            new_maxima[i] = jnp.max(s, axis=0, keepdims=True)

        def fold(i):
            p, hh = chains[i]
            m, l, acc = carry[i]
            m_new = jnp.maximum(m, maxima[i])
            alpha = jnp.exp2(m - m_new)
            prob = jnp.exp2(old_ref[i] - m_new)
            r = (p * 2 + hh) * Dh
            pv = jnp.dot(vt_ref[j_old, r:r + Dh, :], prob.astype(BF16), preferred_element_type=F32)
            out[i] = (m_new, alpha * l + jnp.sum(prob, axis=0, keepdims=True), alpha * acc + pv)

        if j_new is not None:
            for i in range(min(lookahead, n_chains)):
                score(i)
        for i in range(n_chains):
            if j_old is not None:
                fold(i)
            if j_new is not None and i + lookahead < n_chains:
                score(i + lookahead)
        return (tuple(new_maxima) if j_new is not None else None,
                tuple(out) if j_old is not None else carry)

    def steady(i, state):
        maxima, carry = state
        maxima, carry = stage(2 * i + 1, True, sb_ref, 2 * i, sa_ref, maxima, carry)
        return stage(2 * i + 2, True, sa_ref, 2 * i + 1, sb_ref, maxima, carry)

    def tail0(state):
        return stage(None, None, None, qi, sa_ref, *state)[1]

    def tail1(state):
        maxima, carry = stage(qi, False, sb_ref, qi - 1, sa_ref, *state)
        return stage(None, None, None, qi, sb_ref, maxima, carry)[1]

    def tail2(state):
        maxima, carry = stage(qi - 1, True, sb_ref, qi - 2, sa_ref, *state)
        maxima, carry = stage(qi, False, sa_ref, qi - 1, sb_ref, maxima, carry)
        return stage(None, None, None, qi, sa_ref, maxima, carry)[1]

    init = tuple((jnp.full((1, t), NEG_INF, F32), jnp.zeros((1, t), F32), jnp.zeros((Dh, t), F32))
                 for _ in chains)
    maxima, _ = stage(0, qi > 0, sa_ref, None, None, None, init)
    trips = jnp.maximum(qi - 1, 0) // 2
    state = lax.fori_loop(0, trips, steady, (maxima, init))
    carry = lax.switch(qi - 2 * trips, [tail0, tail1, tail2], state)
    for p in range(pairs):
        (_, l0, a0), (_, l1, a1) = carry[2 * p], carry[2 * p + 1]
        o = jnp.concatenate([a0 / l0, a1 / l1], axis=0).T
        o_ref[:, p * LANES:(p + 1) * LANES] = _half_lane_rms(o, og_ref[...], lane_lo).astype(BF16)


def _fox_attn(fq, qx, fk, kx, fvt, og, *, batch, seq):
    T = fq.shape[0]
    t = FOX_BLOCK
    nq = seq // t
    pairs = FOX_PAIRS_PER_STEP
    width = pairs * LANES
    q_spec = lambda w: pl.BlockSpec((t, w), lambda b, g, qi: (b * nq + qi, g))
    return pl.pallas_call(
        functools.partial(_fox_kernel, pairs=pairs),
        grid=(batch, FOX_WIDTH // width, nq),
        in_specs=[q_spec(width),
                  pl.BlockSpec((t, LANES), lambda b, g, qi: (b * nq + qi, 0)),
                  pl.BlockSpec((seq, width), lambda b, g, qi: (b, g)),
                  pl.BlockSpec((seq, LANES), lambda b, g, qi: (b, 0)),
                  pl.BlockSpec((nq, width, t), lambda b, g, qi: (b, g, 0)),
                  _const_spec((1, LANES))],
        out_specs=q_spec(width),
        out_shape=jax.ShapeDtypeStruct((T, FOX_WIDTH), BF16),
        scratch_shapes=[pltpu.VMEM((2 * pairs, t, t), F32)] * 2,
        compiler_params=_params(3),
        name="fox_attn",
    )(fq, qx, fk, kx, fvt, og)


def _gdn_kernel(q_ref, k_ref, v_ref, scol_ref, z_ref, og_ref, o_ref,
                state_ref, u_ref, w_ref, qe_ref, intra_ref, kdt_ref, sm_ref, *, groups_per_seq):
    C = GDN_CHUNK
    G = q_ref.shape[0]
    Dh = GDN_HEAD_DIM
    t = pl.program_id(0)
    heads = range(GDN_HEADS)
    col = lambda h: slice(h * Dh, (h + 1) * Dh)

    @pl.when(t == 0)
    def _():
        for ref in (u_ref, w_ref, qe_ref, intra_ref, kdt_ref, sm_ref):
            ref[...] = jnp.zeros(ref.shape, ref.dtype)

    @pl.when((t == 0) | ((t - 1) % groups_per_seq == 0))
    def _():
        state_ref[...] = jnp.zeros(state_ref.shape, F32)

    cur = t % 2
    prev = 1 - cur
    us = [u_ref[prev, :, col(h)] for h in heads]
    ws16 = [w_ref[prev, :, col(h)] for h in heads]
    qes_prev = [qe_ref[prev, :, col(h)] for h in heads]
    intras_prev = [intra_ref[prev, :, col(h)] for h in heads]
    kd_ts_prev = [kdt_ref[prev, h] for h in heads]
    sm_prev = sm_ref[prev]
    states = [state_ref[h] for h in heads]
    outs = [[] for _ in heads]
    pending = {}

    def recur_first(c):
        rs = slice(c * C, (c + 1) * C)
        for h in heads:
            ws = _bdot(jnp.concatenate([ws16[h][rs], qes_prev[h][rs]], axis=0), states[h])
            pending[h] = (ws[C:2 * C], (us[h][rs] - ws[0:C]).astype(BF16))

    def recur_second(c):
        rs = slice(c * C, (c + 1) * C)
        pair = (c * C) // LANES
        for h in heads:
            qs, vb = pending[h]
            zero = jnp.zeros_like(vb)
            outs[h].append(qs + jnp.dot(intras_prev[h][rs], jnp.concatenate([vb, vb], axis=0),
                                        preferred_element_type=F32))
            vpad = jnp.concatenate([vb, zero] if (c * C) % LANES == 0 else [zero, vb], axis=0)
            g_last = sm_prev[(c + 1) * C - 1:(c + 1) * C, DECAY_ROW + h:DECAY_ROW + h + 1]
            states[h] = states[h] * jnp.exp(g_last) + jnp.dot(
                kd_ts_prev[h][:, pair * LANES:(pair + 1) * LANES], vpad, preferred_element_type=F32)

    recur = [f for c in range(G // C) for f in (functools.partial(recur_first, c),
                                                functools.partial(recur_second, c))]

    ri = lax.broadcasted_iota(jnp.int32, (G, G), 0)
    ci = lax.broadcasted_iota(jnp.int32, (G, G), 1)
    same_chunk = (ri // C) == (ci // C)
    lower = same_chunk & (ri >= ci)
    strict = same_chunk & (ri > ci)
    sm = scol_ref[...]
    sm_t = sm.T
    lmats, intras, ys, qes, kd_ts = [], [], [], [], []
    for h in heads:
        recur.pop(0)()
        q = q_ref[:, col(h)]
        k = k_ref[:, col(h)]
        beta = sm[:, BETA_ROW + h:BETA_ROW + h + 1]
        gc = sm[:, DECAY_ROW + h:DECAY_ROW + h + 1]
        rem = sm[:, REM_ROW + h:REM_ROW + h + 1]
        gc_row = sm_t[DECAY_ROW + h:DECAY_ROW + h + 1, :]
        eg = jnp.exp(gc)
        decay = jnp.exp(jnp.where(lower, gc - gc_row, NEG_INF))
        kb = k * beta
        a = _bdot_nt(jnp.concatenate([kb, q], axis=0), k)
        lmats.append(jnp.where(strict, a[0:G] * decay, 0.0).astype(BF16))
        intra = (a[G:2 * G] * decay).astype(BF16)
        intra_fold = intra[:, 0:LANES]
        for j in range(1, G // LANES):
            intra_fold = intra_fold + intra[:, j * LANES:(j + 1) * LANES]
        intras.append(intra_fold)
        ys.append(jnp.concatenate([v_ref[:, col(h)] * beta, kb * eg], axis=1))
        qes.append((q * eg).astype(BF16))
        kd_ts.append((k * jnp.exp(rem)).T.astype(BF16))
    ys = [ys[h] - _bdot(lmats[h], ys[h]) for h in heads]
    pws = [_bdot(lmats[h], lmats[h]).astype(BF16) for h in heads]
    recur.pop(0)()
    for step in range(5):
        ys = [ys[h] + _bdot(pws[h], ys[h]) for h in heads]
        if step < 4:
            pws = [_bdot(pws[h], pws[h]).astype(BF16) for h in heads]
        if recur:
            recur.pop(0)()
    while recur:
        recur.pop(0)()

    for h in heads:
        state_ref[h] = states[h]
        o = jnp.concatenate(outs[h], axis=0)
        o_ref[:, col(h)] = (_rms(o, og_ref[...]) * z_ref[:, col(h)]).astype(BF16)
        u_ref[cur, :, col(h)] = ys[h][:, 0:Dh]
        w_ref[cur, :, col(h)] = ys[h][:, Dh:2 * Dh].astype(BF16)
        qe_ref[cur, :, col(h)] = qes[h]
        intra_ref[cur, :, col(h)] = intras[h]
        kdt_ref[cur, h] = kd_ts[h]
    sm_ref[cur] = sm


def _gdn(gq, gk, gv, z, scol, og, *, batch, seq):
    T = gq.shape[0]
    G = min(seq, GDN_GROUP)
    ng = seq // G
    n = batch * ng
    cur_spec = lambda width: pl.BlockSpec((G, width), lambda t: (jnp.minimum(t, n - 1), 0))
    prev_spec = lambda width: pl.BlockSpec((G, width), lambda t: (jnp.maximum(t - 1, 0), 0))
    return pl.pallas_call(
        functools.partial(_gdn_kernel, groups_per_seq=ng),
        grid=(n + 1,),
        in_specs=[cur_spec(GDN_WIDTH)] * 3 + [cur_spec(LANES), prev_spec(GDN_WIDTH),
                                              _const_spec((1, GDN_HEAD_DIM))],
        out_specs=prev_spec(GDN_WIDTH),
        out_shape=jax.ShapeDtypeStruct((T, GDN_WIDTH), BF16),
        scratch_shapes=[pltpu.VMEM((GDN_HEADS, GDN_HEAD_DIM, GDN_HEAD_DIM), F32),
                        pltpu.VMEM((2, G, GDN_WIDTH), F32), pltpu.VMEM((2, G, GDN_WIDTH), BF16),
                        pltpu.VMEM((2, G, GDN_WIDTH), BF16), pltpu.VMEM((2, G, GDN_WIDTH), BF16),
                        pltpu.VMEM((2, GDN_HEADS, GDN_HEAD_DIM, G), BF16), pltpu.VMEM((2, G, LANES), F32)],
        compiler_params=_params(1),
        name="gdn",
    )(gq, gk, gv, scol, z, og)


def _mix_mlp_kernel(x_ref, oa_ref, ob_ref, wout_ref, gx_ref, wcq_ref, cqg_ref, ck_ref, cv_ref,
                    wco_ref, gm_ref, w1_ref, w2_ref, o_ref, *, ff_block):
    o = jnp.concatenate([oa_ref[...], ob_ref[...]], axis=-1)
    x1 = x_ref[...] + jnp.dot(o, wout_ref[...], preferred_element_type=F32)
    hq = _rms(x1, gx_ref[...])
    cq = _bdot(hq, wcq_ref[...])
    heads = []
    for h in range(XATTN_HEADS):
        sl = slice(h * XATTN_HEAD_DIM, (h + 1) * XATTN_HEAD_DIM)
        s = _bdot_nt(_rms(cq[:, sl], cqg_ref[...]), ck_ref[0, :, sl])
        p = jnp.exp(s - jnp.max(s, axis=-1, keepdims=True))
        pv = jnp.dot(p.astype(BF16), cv_ref[0, :, sl], preferred_element_type=F32)
        heads.append(pv / jnp.sum(p, axis=-1, keepdims=True))
    co = jnp.concatenate(heads, axis=-1)
    x2 = x1 + _bdot(co, wco_ref[...])
    hb = _rms(x2, gm_ref[...]).astype(BF16)
    acc = x2
    for c in range(w1_ref.shape[1] // ff_block):
        sl = slice(c * ff_block, (c + 1) * ff_block)
        a = jnp.maximum(jnp.dot(hb, w1_ref[:, sl], preferred_element_type=F32), 0.0)
        acc = acc + jnp.dot((a * a).astype(BF16), w2_ref[sl, :], preferred_element_type=F32)
    o_ref[...] = acc


def _mix_mlp(x2d, oa, ob, w_out, gx, w_cq, cqg, ck, cv, w_co, gm, w1, w2, *, seq, tm, ff_block):
    T, D = x2d.shape
    M = ck.shape[1]
    tiles_per_seq = seq // tm
    row_spec = lambda width: pl.BlockSpec((tm, width), lambda i: (i, 0))
    mem_spec = pl.BlockSpec((1, M, XATTN_WIDTH), lambda i: (i // tiles_per_seq, 0, 0))
    return pl.pallas_call(
        functools.partial(_mix_mlp_kernel, ff_block=ff_block),
        grid=(T // tm,),
        in_specs=[row_spec(D), row_spec(FOX_WIDTH), row_spec(GDN_WIDTH), _const_spec(w_out.shape),
                  _const_spec((1, D)), _const_spec(w_cq.shape), _const_spec((1, XATTN_HEAD_DIM)),
                  mem_spec, mem_spec, _const_spec(w_co.shape),
                  _const_spec((1, D)), _const_spec(w1.shape), _const_spec(w2.shape)],
        out_specs=row_spec(D),
        out_shape=jax.ShapeDtypeStruct((T, D), F32),
        compiler_params=_params(1),
        name="mix_mlp",
    )(x2d, oa, ob, w_out, gx, w_cq, cqg, ck, cv, w_co, gm, w1, w2)


def _row_tile(seq):
    return min(seq, 512)


def _layer(x, mem, norm_mix_g, w_in, fox_qnorm_g, fox_knorm_g, fox_f_bias, fox_onorm_g,
           gdn_conv_w, gdn_A_log, gdn_dt_bias, gdn_onorm_g, w_out,
           norm_xattn_g, mem_norm_g, w_cq, w_ckv, xattn_qnorm_g, xattn_knorm_g, w_co,
           norm_mlp_g, w_mlp1, w_mlp2):
    B, S, D = x.shape
    T = B * S
    tm = _row_tile(S)
    row = lambda v: v.reshape(1, -1).astype(F32)

    o_f = 3 * FOX_WIDTH
    o_g = o_f + FOX_HEADS
    o_b = o_g + 3 * GDN_WIDTH
    o_a = o_b + GDN_HEADS
    o_z = o_a + GDN_HEADS
    w_small = jnp.concatenate([w_in[:, o_f:o_g], w_in[:, o_b:o_z], w_in[:, o_a:o_z]], axis=1)
    w_small = jnp.pad(w_small, ((0, 0), (0, LANES - w_small.shape[1])))
    w_all = jnp.concatenate([w_in[:, :o_f], w_in[:, o_g:o_b], w_in[:, o_z:], w_small], axis=1).astype(BF16)
    pad4 = jnp.zeros((GDN_HEADS,), F32)
    bias_col = jnp.concatenate([fox_f_bias, pad4, gdn_dt_bias, gdn_dt_bias, pad4]).reshape(-1, 1)
    alog_col = jnp.concatenate([jnp.zeros((DECAY_ROW,), F32), gdn_A_log, gdn_A_log, pad4]).reshape(-1, 1)
    qg = row(jnp.tile(fox_qnorm_g, 2)) * (FOX_HEAD_DIM ** -0.5 * LOG2E)
    kg = row(jnp.tile(fox_knorm_g, 2))
    og = row(jnp.tile(fox_onorm_g, 2))

    x2d = x.reshape(T, D)
    ck, cv = _mem_kv(mem, row(mem_norm_g), w_ckv.astype(BF16), row(xattn_knorm_g))
    fq, fk, fvt, gq, gk, gv, z, scol, kx, qx = _in_proj(
        x2d, row(norm_mix_g), w_all, qg, kg, gdn_conv_w, bias_col, alog_col, seq=S, tm=tm)
    o_fox = _fox_attn(fq, qx, fk, kx, fvt, og, batch=B, seq=S)
    o_gdn = _gdn(gq, gk, gv, z, scol, row(gdn_onorm_g), batch=B, seq=S)
    out = _mix_mlp(x2d, o_fox, o_gdn, w_out.astype(BF16), row(norm_xattn_g), w_cq.astype(BF16),
                   row(xattn_qnorm_g) * (XATTN_HEAD_DIM ** -0.5), ck, cv, w_co.astype(BF16),
                   row(norm_mlp_g), w_mlp1.astype(BF16), w_mlp2.astype(BF16), seq=S, tm=tm, ff_block=1024)
    return out.reshape(B, S, D)


def kernel(x, mem, norm_mix_g, w_in, fox_qnorm_g, fox_knorm_g, fox_f_bias, fox_onorm_g, gdn_conv_w, gdn_A_log, gdn_dt_bias, gdn_onorm_g, w_out, norm_xattn_g, mem_norm_g, w_cq, w_ckv, xattn_qnorm_g, xattn_knorm_g, w_co, norm_mlp_g, w_mlp1, w_mlp2):
    for l in range(w_in.shape[0]):
        x = _layer(x, mem, norm_mix_g[l], w_in[l], fox_qnorm_g[l], fox_knorm_g[l], fox_f_bias[l],
                   fox_onorm_g[l], gdn_conv_w[l], gdn_A_log[l], gdn_dt_bias[l], gdn_onorm_g[l],
                   w_out[l], norm_xattn_g[l], mem_norm_g[l], w_cq[l], w_ckv[l], xattn_qnorm_g[l],
                   xattn_knorm_g[l], w_co[l], norm_mlp_g[l], w_mlp1[l], w_mlp2[l])
    return x
```

```python
import functools

import jax
import jax.numpy as jnp
from jax import lax
from jax.experimental import pallas as pl
from jax.experimental.pallas import tpu as pltpu

EPS = 1e-6
NEG_INF = -1e30
LOG2E = 1.4426950408889634

FOX_HEADS = 8
FOX_HEAD_DIM = 64
FOX_WIDTH = FOX_HEADS * FOX_HEAD_DIM
FOX_BLOCK = 256
FOX_PAIRS_PER_STEP = 4
GDN_HEADS = 4
GDN_HEAD_DIM = 128
GDN_WIDTH = GDN_HEADS * GDN_HEAD_DIM
CONV_WIDTH = 4
GDN_CHUNK = 64
GDN_GROUP = 256
XATTN_HEADS = 4
XATTN_HEAD_DIM = 128
XATTN_WIDTH = XATTN_HEADS * XATTN_HEAD_DIM

LANES = 128
SUBLANES = 8
SMALL_ROWS = 24
BETA_ROW = FOX_HEADS
DECAY_ROW = BETA_ROW + GDN_HEADS
REM_ROW = DECAY_ROW + GDN_HEADS
VMEM_LIMIT_BYTES = 56 * 1024 * 1024

F32 = jnp.float32
BF16 = jnp.bfloat16


def _bdot(a, b):
    return jnp.dot(a.astype(BF16), b.astype(BF16), preferred_element_type=F32)


def _bdot_nt(a, b):
    return lax.dot_general(a.astype(BF16), b.astype(BF16), (((1,), (1,)), ((), ())),
                           preferred_element_type=F32)


def _rms(x, gain):
    return x * lax.rsqrt(jnp.mean(x * x, axis=-1, keepdims=True) + EPS) * gain


def _half_lane_rms(x, gain, lane_lo):
    ss = x * x
    s_lo = jnp.sum(jnp.where(lane_lo, ss, 0.0), axis=-1, keepdims=True)
    s_hi = jnp.sum(jnp.where(lane_lo, 0.0, ss), axis=-1, keepdims=True)
    ms = jnp.where(lane_lo, s_lo, s_hi) * (1.0 / FOX_HEAD_DIM)
    return x * lax.rsqrt(ms + EPS) * gain


def _softplus(y):
    return jnp.maximum(y, 0.0) + jnp.log1p(jnp.exp(-jnp.abs(y)))


def _silu(y):
    return y / (1.0 + jnp.exp2(y * (-LOG2E)))


def _split3_bf16(v):
    p1 = v.astype(BF16)
    r1 = v - p1.astype(F32)
    p2 = r1.astype(BF16)
    r2 = r1 - p2.astype(F32)
    return p1, p2, r2.astype(BF16)


def _const_spec(shape):
    return pl.BlockSpec(shape, lambda *_: (0,) * len(shape), pipeline_mode=pl.Buffered(1))


def _params(n_grid):
    return pltpu.CompilerParams(dimension_semantics=("arbitrary",) * n_grid,
                                vmem_limit_bytes=VMEM_LIMIT_BYTES)


def _mem_kv_kernel(mem_ref, g_ref, w_ref, kg_ref, ck_ref, cv_ref):
    hm = _rms(mem_ref[0], g_ref[...])
    kv = _bdot(hm, w_ref[...])
    for h in range(XATTN_HEADS):
        sl = slice(h * XATTN_HEAD_DIM, (h + 1) * XATTN_HEAD_DIM)
        ck_ref[0, :, sl] = _rms(kv[:, sl], kg_ref[...]).astype(BF16)
    cv_ref[0] = kv[:, XATTN_WIDTH:].astype(BF16)


def _mem_kv(mem, g, w_ckv, kg):
    B, M, D = mem.shape
    return pl.pallas_call(
        _mem_kv_kernel,
        grid=(B,),
        in_specs=[pl.BlockSpec((1, M, D), lambda b: (b, 0, 0)),
                  _const_spec((1, D)), _const_spec(w_ckv.shape), _const_spec((1, XATTN_HEAD_DIM))],
        out_specs=[pl.BlockSpec((1, M, XATTN_WIDTH), lambda b: (b, 0, 0))] * 2,
        out_shape=[jax.ShapeDtypeStruct((B, M, XATTN_WIDTH), BF16)] * 2,
        compiler_params=_params(1),
        name="mem_kv",
    )(mem, g, w_ckv, kg)


def _in_proj_kernel(x_ref, g_ref, w_ref, qg_ref, kg_ref, convw_ref, bias_ref, alog_ref,
                    fq_ref, fk_ref, fvt_ref, gq_ref, gk_ref, gv_ref, z_ref, scol_ref, kx_ref, qx_ref,
                    gbuf_q, gbuf_k, gbuf_v, carry_ref, *, tiles_per_seq):
    tm = x_ref.shape[0]
    i = pl.program_id(0)
    gbufs = (gbuf_q, gbuf_k, gbuf_v)

    @pl.when(i % tiles_per_seq == 0)
    def _():
        for gbuf in gbufs:
            gbuf[0:SUBLANES, :] = jnp.zeros((SUBLANES, GDN_WIDTH), F32)
        carry_ref[...] = jnp.zeros(carry_ref.shape, F32)

    hb = _rms(x_ref[...], g_ref[...]).astype(BF16)
    lane_lo = lax.broadcasted_iota(jnp.int32, (1, LANES), 1) < FOX_HEAD_DIM
    off_gdn = 3 * FOX_WIDTH
    off_z = off_gdn + 3 * GDN_WIDTH
    off_small = off_z + GDN_WIDTH
    proj = lambda off, width: jnp.dot(hb, w_ref[:, off:off + width], preferred_element_type=F32)

    def gdn_proj(part):
        gbufs[part][SUBLANES:SUBLANES + tm, :] = proj(off_gdn + part * GDN_WIDTH, GDN_WIDTH)

    def gdn_head(part, h):
        gbuf, out_ref = gbufs[part], (gq_ref, gk_ref, gv_ref)[part]
        cs = slice(h * GDN_HEAD_DIM, (h + 1) * GDN_HEAD_DIM)
        ws = slice(part * GDN_WIDTH + h * GDN_HEAD_DIM, part * GDN_WIDTH + (h + 1) * GDN_HEAD_DIM)
        acc = convw_ref[CONV_WIDTH - 1:CONV_WIDTH, ws] * gbuf[SUBLANES:SUBLANES + tm, cs]
        for d in range(1, CONV_WIDTH):
            acc = acc + (convw_ref[CONV_WIDTH - 1 - d:CONV_WIDTH - d, ws]
                         * gbuf[SUBLANES - d:SUBLANES - d + tm, cs])
        y = _silu(acc)
        if part == 0:
            y = y * lax.rsqrt(jnp.sum(y * y, axis=-1, keepdims=True) + EPS) * (GDN_HEAD_DIM ** -0.5)
        elif part == 1:
            y = y * lax.rsqrt(jnp.sum(y * y, axis=-1, keepdims=True) + EPS)
        out_ref[:, cs] = y
        gbuf[0:SUBLANES, cs] = gbuf[tm:tm + SUBLANES, cs]

    MXU_COLS = 2 * LANES

    def fox_qk(out_ref, gain_ref, off, c0):
        y = proj(off + c0, MXU_COLS)
        for j in range(MXU_COLS // LANES):
            sl = slice(c0 + j * LANES, c0 + (j + 1) * LANES)
            out_ref[:, sl] = _half_lane_rms(y[:, j * LANES:(j + 1) * LANES], gain_ref[...],
                                            lane_lo).astype(BF16)

    def fox_v(c0):
        y = proj(2 * FOX_WIDTH + c0, MXU_COLS)
        for r in range(tm // FOX_BLOCK):
            fvt_ref[r, c0:c0 + MXU_COLS, :] = y[r * FOX_BLOCK:(r + 1) * FOX_BLOCK, :].T.astype(BF16)

    def gate(c0):
        z_ref[:, c0:c0 + MXU_COLS] = _silu(proj(off_z + c0, MXU_COLS))

    row = lax.broadcasted_iota(jnp.int32, (SMALL_ROWS, tm), 0)
    is_fox = row < BETA_ROW
    is_beta = row < DECAY_ROW

    def scalars_gates():
        xb = proj(off_small, LANES).T[0:SMALL_ROWS, :] + bias_ref[...]
        sp = _softplus(jnp.where(is_fox, -xb, xb))
        return jnp.where(is_fox, -sp,
                         jnp.where(is_beta, 1.0 / (1.0 + jnp.exp(-xb)), -jnp.exp(alog_ref[...]) * sp))

    def scalars_scans(val):
        ti = lax.broadcasted_iota(jnp.int32, (tm, tm), 0)
        tj = lax.broadcasted_iota(jnp.int32, (tm, tm), 1)
        same_chunk = (ti // GDN_CHUNK) == (tj // GDN_CHUNK)
        one_hot = lambda m: jnp.where(m, 1.0, 0.0).astype(BF16)
        scan_mats = jnp.concatenate([one_hot(ti <= tj), one_hot((ti <= tj) & same_chunk),
                                     one_hot((ti > tj) & same_chunk)], axis=1)
        sums = jnp.zeros((SMALL_ROWS, 3 * tm), F32)
        for part in _split3_bf16(val):
            sums = sums + jnp.dot(part, scan_mats, preferred_element_type=F32)
        cs_full = sums[:, 0:tm] + carry_ref[:, 0:1]
        carry_ref[...] = jnp.broadcast_to(cs_full[:, tm - 1:tm], carry_ref.shape)
        return jnp.where(is_fox, cs_full, jnp.where(is_beta, val, jnp.where(
            row < REM_ROW, sums[:, tm:2 * tm], sums[:, 2 * tm:])))

    def scalars_columns(rows):
        padded = jnp.concatenate([rows, jnp.zeros((LANES - SMALL_ROWS, tm), F32)], axis=0)
        scol_ref[...] = padded.T

    def scalars_bias_lanes(rows):
        c_parts = [p.astype(F32) for p in _split3_bf16(rows[0:FOX_HEADS, :] * LOG2E)]
        zeros = lambda n: jnp.zeros((n, tm), F32)
        ones = jnp.ones((3 * FOX_HEADS, tm), F32)
        kx = jnp.concatenate([-p for p in c_parts] + [zeros(FOX_HEADS), ones, zeros(LANES - 7 * FOX_HEADS)],
                             axis=0)
        qx = jnp.concatenate([ones, zeros(FOX_HEADS)] + c_parts + [zeros(LANES - 7 * FOX_HEADS)], axis=0)
        kx_ref[...] = kx.T.astype(BF16)
        qx_ref[...] = qx.T.astype(BF16)

    val = scalars_gates()
    gdn_proj(0)
    gdn_proj(1)
    gdn_proj(2)
    rows = scalars_scans(val)
    light = ([functools.partial(fox_qk, fq_ref, qg_ref, 0, c) for c in range(0, FOX_WIDTH, MXU_COLS)]
             + [functools.partial(fox_qk, fk_ref, kg_ref, FOX_WIDTH, c) for c in range(0, FOX_WIDTH, MXU_COLS)]
             + [functools.partial(scalars_columns, rows)]
             + [functools.partial(fox_v, c) for c in range(0, FOX_WIDTH, MXU_COLS)]
             + [functools.partial(scalars_bias_lanes, rows)]
             + [functools.partial(gate, c) for c in range(0, GDN_WIDTH, MXU_COLS)])
    for part in range(3):
        for h in range(GDN_HEADS):
            gdn_head(part, h)
            if light:
                light.pop(0)()
    while light:
        light.pop(0)()


def _in_proj(x2d, g, w_all, qg, kg, conv_w, bias_col, alog_col, *, seq, tm):
    T, D = x2d.shape
    n = T // tm
    row_spec = lambda width: pl.BlockSpec((tm, width), lambda i: (i, 0))
    kernel = functools.partial(_in_proj_kernel, tiles_per_seq=seq // tm)
    return pl.pallas_call(
        kernel,
        grid=(n,),
        in_specs=[row_spec(D), _const_spec((1, D)), _const_spec(w_all.shape),
                  _const_spec((1, LANES)), _const_spec((1, LANES)), _const_spec(conv_w.shape),
                  _const_spec((SMALL_ROWS, 1)), _const_spec((SMALL_ROWS, 1))],
        out_specs=[row_spec(FOX_WIDTH)] * 2
        + [pl.BlockSpec((tm // FOX_BLOCK, FOX_WIDTH, FOX_BLOCK), lambda i: (i, 0, 0))]
        + [row_spec(GDN_WIDTH)] * 4 + [row_spec(LANES)] * 3,
        out_shape=[jax.ShapeDtypeStruct((T, FOX_WIDTH), BF16)] * 2
        + [jax.ShapeDtypeStruct((T // FOX_BLOCK, FOX_WIDTH, FOX_BLOCK), BF16)]
        + [jax.ShapeDtypeStruct((T, GDN_WIDTH), F32)] * 4
        + [jax.ShapeDtypeStruct((T, LANES), F32)] + [jax.ShapeDtypeStruct((T, LANES), BF16)] * 2,
        scratch_shapes=[pltpu.VMEM((SUBLANES + tm, GDN_WIDTH), F32)] * 3
        + [pltpu.VMEM((SMALL_ROWS, LANES), F32)],
        compiler_params=_params(1),
        name="in_proj",
    )(x2d, g, w_all, qg, kg, conv_w, bias_col, alog_col)


def _fox_kernel(q_ref, qx_ref, k_ref, kx_ref, vt_ref, og_ref, o_ref, sa_ref, sb_ref, *, pairs):
    t = FOX_BLOCK
    Dh = FOX_HEAD_DIM
    qq = pl.program_id(2)
    lane = lax.broadcasted_iota(jnp.int32, (1, LANES), 1)
    lane_lo = lane < Dh
    in_bias_lanes = (lane < 3 * FOX_HEADS) | ((lane >= 4 * FOX_HEADS) & (lane < 7 * FOX_HEADS))
    chains = [(p, hh) for p in range(pairs) for hh in range(2)]

    def augmented_queries(rows):
        qx = qx_ref[rows, :]
        out = []
        for p, hh in chains:
            h = (pl.program_id(1) * pairs + p) * 2 + hh
            q = q_ref[rows, p * LANES:(p + 1) * LANES]
            q = jnp.where(lane_lo if hh == 0 else jnp.logical_not(lane_lo), q, jnp.zeros_like(q))
            qx_h = jnp.where(in_bias_lanes & (lane % FOX_HEADS == h), qx, jnp.zeros_like(qx))
            out.append(jnp.concatenate([q, qx_h], axis=1))
        return out

    q_even = augmented_queries(slice(0, t))
    q_odd = augmented_queries(slice(t, 2 * t))
    n_chains = len(chains)
    lookahead = 2

    def stage(j_new, keep_all, new_ref, q_aug, j_old, old_ref, maxima, carry):
        if j_new is not None:
            r0 = pl.multiple_of(j_new * t, t)
            kx = kx_ref[pl.ds(r0, t), :]
            if keep_all is not True:
                visible = (lax.broadcasted_iota(jnp.int32, (t, t), 0)
                           <= lax.broadcasted_iota(jnp.int32, (t, t), 1))
                if keep_all is not False:
                    visible = visible | keep_all
        new_maxima = [None] * n_chains
        out = [None] * n_chains

        def score(i):
            p = chains[i][0]
            k_aug = jnp.concatenate([k_ref[pl.ds(r0, t), p * LANES:(p + 1) * LANES], kx], axis=1)
            s = _bdot_nt(k_aug, q_aug[i])
            if keep_all is not True:
                s = jnp.where(visible, s, NEG_INF)
            new_ref[i] = s
            new_maxima[i] = jnp.max(s, axis=0, keepdims=True)

        def fold(i):
            p, hh = chains[i]
            m, l, acc = carry[i]
            m_new = jnp.maximum(m, maxima[i])
            alpha = jnp.exp2(m - m_new)
            prob = jnp.exp2(old_ref[i] - m_new)
            r = (p * 2 + hh) * Dh
            pv = jnp.dot(vt_ref[j_old, r:r + Dh, :], prob.astype(BF16), preferred_element_type=F32)
            out[i] = (m_new, alpha * l + jnp.sum(prob, axis=0, keepdims=True), alpha * acc + pv)

        if j_new is not None:
            for i in range(min(lookahead, n_chains)):
                score(i)
        for i in range(n_chains):
            if j_old is not None:
                fold(i)
            if j_new is not None and i + lookahead < n_chains:
                score(i + lookahead)
        return (tuple(new_maxima) if j_new is not None else None,
                tuple(out) if j_old is not None else carry)

    def finish(rows, carry):
        for p in range(pairs):
            (_, l0, a0), (_, l1, a1) = carry[2 * p], carry[2 * p + 1]
            o = jnp.concatenate([a0 / l0, a1 / l1], axis=0).T
            o_ref[rows, p * LANES:(p + 1) * LANES] = _half_lane_rms(o, og_ref[...], lane_lo).astype(BF16)

    init = tuple((jnp.full((1, t), NEG_INF, F32), jnp.zeros((1, t), F32), jnp.zeros((Dh, t), F32))
                 for _ in chains)

    def even_pair(i, state):
        maxima, carry = state
        maxima, carry = stage(2 * i + 1, True, sb_ref, q_even, 2 * i, sa_ref, maxima, carry)
        return stage(2 * i + 2, i + 1 < qq, sa_ref, q_even, 2 * i + 1, sb_ref, maxima, carry)

    def odd_pair(i, state):
        maxima, carry = state
        maxima, carry = stage(2 * i + 1, True, sa_ref, q_odd, 2 * i, sb_ref, maxima, carry)
        return stage(2 * i + 2, True, sb_ref, q_odd, 2 * i + 1, sa_ref, maxima, carry)

    maxima, _ = stage(0, qq > 0, sa_ref, q_even, None, None, None, init)
    maxima, carry = lax.fori_loop(0, qq, even_pair, (maxima, init))
    maxima, carry = stage(0, True, sb_ref, q_odd, 2 * qq, sa_ref, maxima, carry)
    finish(slice(0, t), carry)
    maxima, carry = lax.fori_loop(0, qq, odd_pair, (maxima, init))
    maxima, carry = stage(2 * qq + 1, False, sa_ref, q_odd, 2 * qq, sb_ref, maxima, carry)
    _, carry = stage(None, None, None, None, 2 * qq + 1, sa_ref, maxima, carry)
    finish(slice(t, 2 * t), carry)


def _fox_attn(fq, qx, fk, kx, fvt, og, *, batch, seq):
    T = fq.shape[0]
    t = FOX_BLOCK
    nq = seq // t
    pairs = FOX_PAIRS_PER_STEP
    width = pairs * LANES
    nqq = nq // 2
    q_spec = lambda w: pl.BlockSpec((2 * t, w), lambda b, g, qq: (b * nqq + qq, g))
    return pl.pallas_call(
        functools.partial(_fox_kernel, pairs=pairs),
        grid=(batch, FOX_WIDTH // width, nqq),
        in_specs=[q_spec(width),
                  pl.BlockSpec((2 * t, LANES), lambda b, g, qq: (b * nqq + qq, 0)),
                  pl.BlockSpec((seq, width), lambda b, g, qi: (b, g)),
                  pl.BlockSpec((seq, LANES), lambda b, g, qi: (b, 0)),
                  pl.BlockSpec((nq, width, t), lambda b, g, qi: (b, g, 0)),
                  _const_spec((1, LANES))],
        out_specs=q_spec(width),
        out_shape=jax.ShapeDtypeStruct((T, FOX_WIDTH), BF16),
        scratch_shapes=[pltpu.VMEM((2 * pairs, t, t), F32)] * 2,
        compiler_params=_params(3),
        name="fox_attn",
    )(fq, qx, fk, kx, fvt, og)


def _gdn_kernel(q_ref, k_ref, v_ref, scol_ref, z_ref, og_ref, o_ref,
                state_ref, u_ref, w_ref, qe_ref, intra_ref, kdt_ref, sm_ref, *, groups_per_seq):
    C = GDN_CHUNK
    G = q_ref.shape[0]
    Dh = GDN_HEAD_DIM
    t = pl.program_id(0)
    heads = range(GDN_HEADS)
    col = lambda h: slice(h * Dh, (h + 1) * Dh)

    @pl.when(t == 0)
    def _():
        for ref in (u_ref, w_ref, qe_ref, intra_ref, kdt_ref, sm_ref):
            ref[...] = jnp.zeros(ref.shape, ref.dtype)

    @pl.when((t == 0) | ((t - 1) % groups_per_seq == 0))
    def _():
        state_ref[...] = jnp.zeros(state_ref.shape, F32)

    cur = t % 2
    prev = 1 - cur
    us = [u_ref[prev, :, col(h)] for h in heads]
    ws16 = [w_ref[prev, :, col(h)] for h in heads]
    qes_prev = [qe_ref[prev, :, col(h)] for h in heads]
    intras_prev = [intra_ref[prev, :, col(h)] for h in heads]
    kd_ts_prev = [kdt_ref[prev, h] for h in heads]
    sm_prev = sm_ref[prev]
    states = [state_ref[h] for h in heads]
    outs = [[] for _ in heads]
    pending = {}

    def recur_first(c):
        rs = slice(c * C, (c + 1) * C)
        for h in heads:
            ws = _bdot(jnp.concatenate([ws16[h][rs], qes_prev[h][rs]], axis=0), states[h])
            pending[h] = (ws[C:2 * C], (us[h][rs] - ws[0:C]).astype(BF16))

    def recur_second(c):
        rs = slice(c * C, (c + 1) * C)
        pair = (c * C) // LANES
        for h in heads:
            qs, vb = pending[h]
            zero = jnp.zeros_like(vb)
            outs[h].append(qs + jnp.dot(intras_prev[h][rs], jnp.concatenate([vb, vb], axis=0),
                                        preferred_element_type=F32))
            vpad = jnp.concatenate([vb, zero] if (c * C) % LANES == 0 else [zero, vb], axis=0)
            g_last = sm_prev[(c + 1) * C - 1:(c + 1) * C, DECAY_ROW + h:DECAY_ROW + h + 1]
            states[h] = states[h] * jnp.exp(g_last) + jnp.dot(
                kd_ts_prev[h][:, pair * LANES:(pair + 1) * LANES], vpad, preferred_element_type=F32)

    recur = [f for c in range(G // C) for f in (functools.partial(recur_first, c),
                                                functools.partial(recur_second, c))]

    ri = lax.broadcasted_iota(jnp.int32, (G, G), 0)
    ci = lax.broadcasted_iota(jnp.int32, (G, G), 1)
    same_chunk = (ri // C) == (ci // C)
    lower = same_chunk & (ri >= ci)
    strict = same_chunk & (ri > ci)
    sm = scol_ref[...]
    sm_t = sm.T
    lmats, intras, ys, qes, kd_ts = [], [], [], [], []
    for h in heads:
        recur.pop(0)()
        q = q_ref[:, col(h)]
        k = k_ref[:, col(h)]
        beta = sm[:, BETA_ROW + h:BETA_ROW + h + 1]
        gc = sm[:, DECAY_ROW + h:DECAY_ROW + h + 1]
        rem = sm[:, REM_ROW + h:REM_ROW + h + 1]
        gc_row = sm_t[DECAY_ROW + h:DECAY_ROW + h + 1, :]
        eg = jnp.exp(gc)
        decay = jnp.exp(jnp.where(lower, gc - gc_row, NEG_INF))
        kb = k * beta
        a = _bdot_nt(jnp.concatenate([kb, q], axis=0), k)
        lmats.append(jnp.where(strict, a[0:G] * decay, 0.0).astype(BF16))
        intra = (a[G:2 * G] * decay).astype(BF16)
        intra_fold = intra[:, 0:LANES]
        for j in range(1, G // LANES):
            intra_fold = intra_fold + intra[:, j * LANES:(j + 1) * LANES]
        intras.append(intra_fold)
        ys.append(jnp.concatenate([v_ref[:, col(h)] * beta, kb * eg], axis=1))
        qes.append((q * eg).astype(BF16))
        kd_ts.append((k * jnp.exp(rem)).T.astype(BF16))
    ys = [ys[h] - _bdot(lmats[h], ys[h]) for h in heads]
    pws = [_bdot(lmats[h], lmats[h]).astype(BF16) for h in heads]
    recur.pop(0)()
    for step in range(5):
        ys = [ys[h] + _bdot(pws[h], ys[h]) for h in heads]
        if step < 4:
            pws = [_bdot(pws[h], pws[h]).astype(BF16) for h in heads]
        if recur:
            recur.pop(0)()
    while recur:
        recur.pop(0)()

    for h in heads:
        state_ref[h] = states[h]
        o = jnp.concatenate(outs[h], axis=0)
        o_ref[:, col(h)] = (_rms(o, og_ref[...]) * z_ref[:, col(h)]).astype(BF16)
        u_ref[cur, :, col(h)] = ys[h][:, 0:Dh]
        w_ref[cur, :, col(h)] = ys[h][:, Dh:2 * Dh].astype(BF16)
        qe_ref[cur, :, col(h)] = qes[h]
        intra_ref[cur, :, col(h)] = intras[h]
        kdt_ref[cur, h] = kd_ts[h]
    sm_ref[cur] = sm


def _gdn(gq, gk, gv, z, scol, og, *, batch, seq):
    T = gq.shape[0]
    G = min(seq, GDN_GROUP)
    ng = seq // G
    n = batch * ng
    cur_spec = lambda width: pl.BlockSpec((G, width), lambda t: (jnp.minimum(t, n - 1), 0))
    prev_spec = lambda width: pl.BlockSpec((G, width), lambda t: (jnp.maximum(t - 1, 0), 0))
    return pl.pallas_call(
        functools.partial(_gdn_kernel, groups_per_seq=ng),
        grid=(n + 1,),
        in_specs=[cur_spec(GDN_WIDTH)] * 3 + [cur_spec(LANES), prev_spec(GDN_WIDTH),
                                              _const_spec((1, GDN_HEAD_DIM))],
        out_specs=prev_spec(GDN_WIDTH),
        out_shape=jax.ShapeDtypeStruct((T, GDN_WIDTH), BF16),
        scratch_shapes=[pltpu.VMEM((GDN_HEADS, GDN_HEAD_DIM, GDN_HEAD_DIM), F32),
                        pltpu.VMEM((2, G, GDN_WIDTH), F32), pltpu.VMEM((2, G, GDN_WIDTH), BF16),
                        pltpu.VMEM((2, G, GDN_WIDTH), BF16), pltpu.VMEM((2, G, GDN_WIDTH), BF16),
                        pltpu.VMEM((2, GDN_HEADS, GDN_HEAD_DIM, G), BF16), pltpu.VMEM((2, G, LANES), F32)],
        compiler_params=_params(1),
        name="gdn",
    )(gq, gk, gv, scol, z, og)


def _mix_mlp_kernel(x_ref, oa_ref, ob_ref, wout_ref, gx_ref, wcq_ref, cqg_ref, ck_ref, cv_ref,
                    wco_ref, gm_ref, w1_ref, w2_ref, o_ref, *, ff_block):
    o = jnp.concatenate([oa_ref[...], ob_ref[...]], axis=-1)
    x1 = x_ref[...] + jnp.dot(o, wout_ref[...], preferred_element_type=F32)
    hq = _rms(x1, gx_ref[...])
    cq = _bdot(hq, wcq_ref[...])
    heads = []
    for h in range(XATTN_HEADS):
        sl = slice(h * XATTN_HEAD_DIM, (h + 1) * XATTN_HEAD_DIM)
        s = _bdot_nt(_rms(cq[:, sl], cqg_ref[...]), ck_ref[0, :, sl])
        p = jnp.exp(s - jnp.max(s, axis=-1, keepdims=True))
        pv = jnp.dot(p.astype(BF16), cv_ref[0, :, sl], preferred_element_type=F32)
        heads.append(pv / jnp.sum(p, axis=-1, keepdims=True))
    co = jnp.concatenate(heads, axis=-1)
    x2 = x1 + _bdot(co, wco_ref[...])
    hb = _rms(x2, gm_ref[...]).astype(BF16)
    acc = x2
    for c in range(w1_ref.shape[1] // ff_block):
        sl = slice(c * ff_block, (c + 1) * ff_block)
        a = jnp.maximum(jnp.dot(hb, w1_ref[:, sl], preferred_element_type=F32), 0.0)
        acc = acc + jnp.dot((a * a).astype(BF16), w2_ref[sl, :], preferred_element_type=F32)
    o_ref[...] = acc


def _mix_mlp(x2d, oa, ob, w_out, gx, w_cq, cqg, ck, cv, w_co, gm, w1, w2, *, seq, tm, ff_block):
    T, D = x2d.shape
    M = ck.shape[1]
    tiles_per_seq = seq // tm
    row_spec = lambda width: pl.BlockSpec((tm, width), lambda i: (i, 0))
    mem_spec = pl.BlockSpec((1, M, XATTN_WIDTH), lambda i: (i // tiles_per_seq, 0, 0))
    return pl.pallas_call(
        functools.partial(_mix_mlp_kernel, ff_block=ff_block),
        grid=(T // tm,),
        in_specs=[row_spec(D), row_spec(FOX_WIDTH), row_spec(GDN_WIDTH), _const_spec(w_out.shape),
                  _const_spec((1, D)), _const_spec(w_cq.shape), _const_spec((1, XATTN_HEAD_DIM)),
                  mem_spec, mem_spec, _const_spec(w_co.shape),
                  _const_spec((1, D)), _const_spec(w1.shape), _const_spec(w2.shape)],
        out_specs=row_spec(D),
        out_shape=jax.ShapeDtypeStruct((T, D), F32),
        compiler_params=_params(1),
        name="mix_mlp",
    )(x2d, oa, ob, w_out, gx, w_cq, cqg, ck, cv, w_co, gm, w1, w2)


def _row_tile(seq):
    return min(seq, 512)


def _layer(x, mem, norm_mix_g, w_in, fox_qnorm_g, fox_knorm_g, fox_f_bias, fox_onorm_g,
           gdn_conv_w, gdn_A_log, gdn_dt_bias, gdn_onorm_g, w_out,
           norm_xattn_g, mem_norm_g, w_cq, w_ckv, xattn_qnorm_g, xattn_knorm_g, w_co,
           norm_mlp_g, w_mlp1, w_mlp2):
    B, S, D = x.shape
    T = B * S
    tm = _row_tile(S)
    row = lambda v: v.reshape(1, -1).astype(F32)

    o_f = 3 * FOX_WIDTH
    o_g = o_f + FOX_HEADS
    o_b = o_g + 3 * GDN_WIDTH
    o_a = o_b + GDN_HEADS
    o_z = o_a + GDN_HEADS
    w_small = jnp.concatenate([w_in[:, o_f:o_g], w_in[:, o_b:o_z], w_in[:, o_a:o_z]], axis=1)
    w_small = jnp.pad(w_small, ((0, 0), (0, LANES - w_small.shape[1])))
    w_all = jnp.concatenate([w_in[:, :o_f], w_in[:, o_g:o_b], w_in[:, o_z:], w_small], axis=1).astype(BF16)
    pad4 = jnp.zeros((GDN_HEADS,), F32)
    bias_col = jnp.concatenate([fox_f_bias, pad4, gdn_dt_bias, gdn_dt_bias, pad4]).reshape(-1, 1)
    alog_col = jnp.concatenate([jnp.zeros((DECAY_ROW,), F32), gdn_A_log, gdn_A_log, pad4]).reshape(-1, 1)
    qg = row(jnp.tile(fox_qnorm_g, 2)) * (FOX_HEAD_DIM ** -0.5 * LOG2E)
    kg = row(jnp.tile(fox_knorm_g, 2))
    og = row(jnp.tile(fox_onorm_g, 2))

    x2d = x.reshape(T, D)
    ck, cv = _mem_kv(mem, row(mem_norm_g), w_ckv.astype(BF16), row(xattn_knorm_g))
    fq, fk, fvt, gq, gk, gv, z, scol, kx, qx = _in_proj(
        x2d, row(norm_mix_g), w_all, qg, kg, gdn_conv_w, bias_col, alog_col, seq=S, tm=tm)
    o_fox = _fox_attn(fq, qx, fk, kx, fvt, og, batch=B, seq=S)
    o_gdn = _gdn(gq, gk, gv, z, scol, row(gdn_onorm_g), batch=B, seq=S)
    out = _mix_mlp(x2d, o_fox, o_gdn, w_out.astype(BF16), row(norm_xattn_g), w_cq.astype(BF16),
                   row(xattn_qnorm_g) * (XATTN_HEAD_DIM ** -0.5), ck, cv, w_co.astype(BF16),
                   row(norm_mlp_g), w_mlp1.astype(BF16), w_mlp2.astype(BF16), seq=S, tm=tm, ff_block=1024)
    return out.reshape(B, S, D)


def kernel(x, mem, norm_mix_g, w_in, fox_qnorm_g, fox_knorm_g, fox_f_bias, fox_onorm_g, gdn_conv_w, gdn_A_log, gdn_dt_bias, gdn_onorm_g, w_out, norm_xattn_g, mem_norm_g, w_cq, w_ckv, xattn_qnorm_g, xattn_knorm_g, w_co, norm_mlp_g, w_mlp1, w_mlp2):
    for l in range(w_in.shape[0]):
        x = _layer(x, mem, norm_mix_g[l], w_in[l], fox_qnorm_g[l], fox_knorm_g[l], fox_f_bias[l],
                   fox_onorm_g[l], gdn_conv_w[l], gdn_A_log[l], gdn_dt_bias[l], gdn_onorm_g[l],
                   w_out[l], norm_xattn_g[l], mem_norm_g[l], w_cq[l], w_ckv[l], xattn_qnorm_g[l],
                   xattn_knorm_g[l], w_co[l], norm_mlp_g[l], w_mlp1[l], w_mlp2[l])
    return x
```

```python
import functools

import jax
import jax.numpy as jnp
from jax import lax
from jax.experimental import pallas as pl
from jax.experimental.pallas import tpu as pltpu

EPS = 1e-6
NEG_INF = -1e30
LOG2E = 1.4426950408889634

FOX_HEADS = 8
FOX_HEAD_DIM = 64
FOX_WIDTH = FOX_HEADS * FOX_HEAD_DIM
FOX_BLOCK = 256
FOX_PAIRS_PER_STEP = 4
FOX_QBLOCKS_PER_STEP = 4
GDN_HEADS = 4
GDN_HEAD_DIM = 128
GDN_WIDTH = GDN_HEADS * GDN_HEAD_DIM
CONV_WIDTH = 4
GDN_CHUNK = 64
GDN_GROUP = 256
XATTN_HEADS = 4
XATTN_HEAD_DIM = 128
XATTN_WIDTH = XATTN_HEADS * XATTN_HEAD_DIM

LANES = 128
SUBLANES = 8
SMALL_ROWS = 24
BETA_ROW = FOX_HEADS
DECAY_ROW = BETA_ROW + GDN_HEADS
REM_ROW = DECAY_ROW + GDN_HEADS
VMEM_LIMIT_BYTES = 56 * 1024 * 1024

F32 = jnp.float32
BF16 = jnp.bfloat16


def _bdot(a, b):
    return jnp.dot(a.astype(BF16), b.astype(BF16), preferred_element_type=F32)


def _bdot_nt(a, b):
    return lax.dot_general(a.astype(BF16), b.astype(BF16), (((1,), (1,)), ((), ())),
                           preferred_element_type=F32)


def _rms(x, gain):
    return x * lax.rsqrt(jnp.mean(x * x, axis=-1, keepdims=True) + EPS) * gain


def _half_lane_rms(x, gain, lane_lo):
    ss = x * x
    s_lo = jnp.sum(jnp.where(lane_lo, ss, 0.0), axis=-1, keepdims=True)
    s_hi = jnp.sum(jnp.where(lane_lo, 0.0, ss), axis=-1, keepdims=True)
    ms = jnp.where(lane_lo, s_lo, s_hi) * (1.0 / FOX_HEAD_DIM)
    return x * lax.rsqrt(ms + EPS) * gain


def _softplus(y):
    return jnp.maximum(y, 0.0) + jnp.log1p(jnp.exp(-jnp.abs(y)))


def _silu(y):
    return y / (1.0 + jnp.exp2(y * (-LOG2E)))


def _split3_bf16(v):
    p1 = v.astype(BF16)
    r1 = v - p1.astype(F32)
    p2 = r1.astype(BF16)
    r2 = r1 - p2.astype(F32)
    return p1, p2, r2.astype(BF16)


def _const_spec(shape):
    return pl.BlockSpec(shape, lambda *_: (0,) * len(shape), pipeline_mode=pl.Buffered(1))


def _params(n_grid):
    return pltpu.CompilerParams(dimension_semantics=("arbitrary",) * n_grid,
                                vmem_limit_bytes=VMEM_LIMIT_BYTES)


def _mem_kv_kernel(mem_ref, g_ref, w_ref, kg_ref, ck_ref, cv_ref):
    hm = _rms(mem_ref[0], g_ref[...])
    kv = _bdot(hm, w_ref[...])
    for h in range(XATTN_HEADS):
        sl = slice(h * XATTN_HEAD_DIM, (h + 1) * XATTN_HEAD_DIM)
        ck_ref[0, :, sl] = _rms(kv[:, sl], kg_ref[...]).astype(BF16)
    cv_ref[0] = kv[:, XATTN_WIDTH:].astype(BF16)


def _mem_kv(mem, g, w_ckv, kg):
    B, M, D = mem.shape
    return pl.pallas_call(
        _mem_kv_kernel,
        grid=(B,),
        in_specs=[pl.BlockSpec((1, M, D), lambda b: (b, 0, 0)),
                  _const_spec((1, D)), _const_spec(w_ckv.shape), _const_spec((1, XATTN_HEAD_DIM))],
        out_specs=[pl.BlockSpec((1, M, XATTN_WIDTH), lambda b: (b, 0, 0))] * 2,
        out_shape=[jax.ShapeDtypeStruct((B, M, XATTN_WIDTH), BF16)] * 2,
        compiler_params=_params(1),
        name="mem_kv",
    )(mem, g, w_ckv, kg)


def _in_proj_kernel(x_ref, g_ref, w_ref, qg_ref, kg_ref, convw_ref, bias_ref, alog_ref,
                    fq_ref, fk_ref, fvt_ref, gq_ref, gk_ref, gv_ref, z_ref, scol_ref, kx_ref, qx_ref,
                    gbuf_q, gbuf_k, gbuf_v, carry_ref, *, tiles_per_seq):
    tm = x_ref.shape[0]
    i = pl.program_id(0)
    gbufs = (gbuf_q, gbuf_k, gbuf_v)

    @pl.when(i % tiles_per_seq == 0)
    def _():
        for gbuf in gbufs:
            gbuf[0:SUBLANES, :] = jnp.zeros((SUBLANES, GDN_WIDTH), F32)
        carry_ref[...] = jnp.zeros(carry_ref.shape, F32)

    hb = _rms(x_ref[...], g_ref[...]).astype(BF16)
    lane_lo = lax.broadcasted_iota(jnp.int32, (1, LANES), 1) < FOX_HEAD_DIM
    off_gdn = 3 * FOX_WIDTH
    off_z = off_gdn + 3 * GDN_WIDTH
    off_small = off_z + GDN_WIDTH
    proj = lambda off, width: jnp.dot(hb, w_ref[:, off:off + width], preferred_element_type=F32)

    def gdn_proj(part):
        gbufs[part][SUBLANES:SUBLANES + tm, :] = proj(off_gdn + part * GDN_WIDTH, GDN_WIDTH)

    def gdn_head(part, h):
        gbuf, out_ref = gbufs[part], (gq_ref, gk_ref, gv_ref)[part]
        cs = slice(h * GDN_HEAD_DIM, (h + 1) * GDN_HEAD_DIM)
        ws = slice(part * GDN_WIDTH + h * GDN_HEAD_DIM, part * GDN_WIDTH + (h + 1) * GDN_HEAD_DIM)
        acc = convw_ref[CONV_WIDTH - 1:CONV_WIDTH, ws] * gbuf[SUBLANES:SUBLANES + tm, cs]
        for d in range(1, CONV_WIDTH):
            acc = acc + (convw_ref[CONV_WIDTH - 1 - d:CONV_WIDTH - d, ws]
                         * gbuf[SUBLANES - d:SUBLANES - d + tm, cs])
        y = _silu(acc)
        if part == 0:
            y = y * lax.rsqrt(jnp.sum(y * y, axis=-1, keepdims=True) + EPS) * (GDN_HEAD_DIM ** -0.5)
        elif part == 1:
            y = y * lax.rsqrt(jnp.sum(y * y, axis=-1, keepdims=True) + EPS)
        out_ref[:, cs] = y
        gbuf[0:SUBLANES, cs] = gbuf[tm:tm + SUBLANES, cs]

    MXU_COLS = 2 * LANES

    def fox_qk(out_ref, gain_ref, off, c0):
        y = proj(off + c0, MXU_COLS)
        for j in range(MXU_COLS // LANES):
            sl = slice(c0 + j * LANES, c0 + (j + 1) * LANES)
            out_ref[:, sl] = _half_lane_rms(y[:, j * LANES:(j + 1) * LANES], gain_ref[...],
                                            lane_lo).astype(BF16)

    def fox_v(c0):
        y = proj(2 * FOX_WIDTH + c0, MXU_COLS)
        for r in range(tm // FOX_BLOCK):
            fvt_ref[r, c0:c0 + MXU_COLS, :] = y[r * FOX_BLOCK:(r + 1) * FOX_BLOCK, :].T.astype(BF16)

    def gate(c0):
        z_ref[:, c0:c0 + MXU_COLS] = _silu(proj(off_z + c0, MXU_COLS))

    row = lax.broadcasted_iota(jnp.int32, (SMALL_ROWS, tm), 0)
    is_fox = row < BETA_ROW
    is_beta = row < DECAY_ROW

    def scalars_gates():
        xb = proj(off_small, LANES).T[0:SMALL_ROWS, :] + bias_ref[...]
        sp = _softplus(jnp.where(is_fox, -xb, xb))
        return jnp.where(is_fox, -sp,
                         jnp.where(is_beta, 1.0 / (1.0 + jnp.exp(-xb)), -jnp.exp(alog_ref[...]) * sp))

    def scalars_scans(val):
        ti = lax.broadcasted_iota(jnp.int32, (tm, tm), 0)
        tj = lax.broadcasted_iota(jnp.int32, (tm, tm), 1)
        same_chunk = (ti // GDN_CHUNK) == (tj // GDN_CHUNK)
        one_hot = lambda m: jnp.where(m, 1.0, 0.0).astype(BF16)
        scan_mats = jnp.concatenate([one_hot(ti <= tj), one_hot((ti <= tj) & same_chunk),
                                     one_hot((ti > tj) & same_chunk)], axis=1)
        sums = jnp.zeros((SMALL_ROWS, 3 * tm), F32)
        for part in _split3_bf16(val):
            sums = sums + jnp.dot(part, scan_mats, preferred_element_type=F32)
        cs_full = sums[:, 0:tm] + carry_ref[:, 0:1]
        carry_ref[...] = jnp.broadcast_to(cs_full[:, tm - 1:tm], carry_ref.shape)
        return jnp.where(is_fox, cs_full, jnp.where(is_beta, val, jnp.where(
            row < REM_ROW, sums[:, tm:2 * tm], sums[:, 2 * tm:])))

    def scalars_columns(rows):
        padded = jnp.concatenate([rows, jnp.zeros((LANES - SMALL_ROWS, tm), F32)], axis=0)
        scol_ref[...] = padded.T

    def scalars_bias_lanes(rows):
        c_parts = [p.astype(F32) for p in _split3_bf16(rows[0:FOX_HEADS, :] * LOG2E)]
        zeros = lambda n: jnp.zeros((n, tm), F32)
        ones = jnp.ones((3 * FOX_HEADS, tm), F32)
        kx = jnp.concatenate([-p for p in c_parts] + [zeros(FOX_HEADS), ones, zeros(LANES - 7 * FOX_HEADS)],
                             axis=0)
        qx = jnp.concatenate([ones, zeros(FOX_HEADS)] + c_parts + [zeros(LANES - 7 * FOX_HEADS)], axis=0)
        kx_ref[...] = kx.T.astype(BF16)
        qx_ref[...] = qx.T.astype(BF16)

    val = scalars_gates()
    gdn_proj(0)
    gdn_proj(1)
    gdn_proj(2)
    rows = scalars_scans(val)
    light = ([functools.partial(fox_qk, fq_ref, qg_ref, 0, c) for c in range(0, FOX_WIDTH, MXU_COLS)]
             + [functools.partial(fox_qk, fk_ref, kg_ref, FOX_WIDTH, c) for c in range(0, FOX_WIDTH, MXU_COLS)]
             + [functools.partial(scalars_columns, rows)]
             + [functools.partial(fox_v, c) for c in range(0, FOX_WIDTH, MXU_COLS)]
             + [functools.partial(scalars_bias_lanes, rows)]
             + [functools.partial(gate, c) for c in range(0, GDN_WIDTH, MXU_COLS)])
    for part in range(3):
        for h in range(GDN_HEADS):
            gdn_head(part, h)
            if light:
                light.pop(0)()
    while light:
        light.pop(0)()


def _in_proj(x2d, g, w_all, qg, kg, conv_w, bias_col, alog_col, *, seq, tm):
    T, D = x2d.shape
    n = T // tm
    row_spec = lambda width: pl.BlockSpec((tm, width), lambda i: (i, 0))
    kernel = functools.partial(_in_proj_kernel, tiles_per_seq=seq // tm)
    return pl.pallas_call(
        kernel,
        grid=(n,),
        in_specs=[row_spec(D), _const_spec((1, D)), _const_spec(w_all.shape),
                  _const_spec((1, LANES)), _const_spec((1, LANES)), _const_spec(conv_w.shape),
                  _const_spec((SMALL_ROWS, 1)), _const_spec((SMALL_ROWS, 1))],
        out_specs=[row_spec(FOX_WIDTH)] * 2
        + [pl.BlockSpec((tm // FOX_BLOCK, FOX_WIDTH, FOX_BLOCK), lambda i: (i, 0, 0))]
        + [row_spec(GDN_WIDTH)] * 4 + [row_spec(LANES)] * 3,
        out_shape=[jax.ShapeDtypeStruct((T, FOX_WIDTH), BF16)] * 2
        + [jax.ShapeDtypeStruct((T // FOX_BLOCK, FOX_WIDTH, FOX_BLOCK), BF16)]
        + [jax.ShapeDtypeStruct((T, GDN_WIDTH), F32)] * 4
        + [jax.ShapeDtypeStruct((T, LANES), F32)] + [jax.ShapeDtypeStruct((T, LANES), BF16)] * 2,
        scratch_shapes=[pltpu.VMEM((SUBLANES + tm, GDN_WIDTH), F32)] * 3
        + [pltpu.VMEM((SMALL_ROWS, LANES), F32)],
        compiler_params=_params(1),
        name="in_proj",
    )(x2d, g, w_all, qg, kg, conv_w, bias_col, alog_col)


def _fox_kernel(q_ref, qx_ref, k_ref, kx_ref, vt_ref, og_ref, o_ref, sa_ref, sb_ref, *, pairs):
    t = FOX_BLOCK
    Dh = FOX_HEAD_DIM
    n_q = q_ref.shape[0] // t
    first_q = pl.program_id(2) * n_q
    lane = lax.broadcasted_iota(jnp.int32, (1, LANES), 1)
    lane_lo = lane < Dh
    in_bias_lanes = (lane < 3 * FOX_HEADS) | ((lane >= 4 * FOX_HEADS) & (lane < 7 * FOX_HEADS))
    chains = [(p, hh) for p in range(pairs) for hh in range(2)]

    def augmented_queries(rows):
        qx = qx_ref[rows, :]
        out = []
        for p, hh in chains:
            h = (pl.program_id(1) * pairs + p) * 2 + hh
            q = q_ref[rows, p * LANES:(p + 1) * LANES]
            q = jnp.where(lane_lo if hh == 0 else jnp.logical_not(lane_lo), q, jnp.zeros_like(q))
            qx_h = jnp.where(in_bias_lanes & (lane % FOX_HEADS == h), qx, jnp.zeros_like(qx))
            out.append(jnp.concatenate([q, qx_h], axis=1))
        return out

    q_augs = [augmented_queries(slice(n * t, (n + 1) * t)) for n in range(n_q)]
    n_chains = len(chains)
    lookahead = 2

    def stage(j_new, keep_all, new_ref, q_aug, j_old, old_ref, maxima, carry):
        if j_new is not None:
            r0 = pl.multiple_of(j_new * t, t)
            kx = kx_ref[pl.ds(r0, t), :]
            if keep_all is not True:
                visible = (lax.broadcasted_iota(jnp.int32, (t, t), 0)
                           <= lax.broadcasted_iota(jnp.int32, (t, t), 1))
                if keep_all is not False:
                    visible = visible | keep_all
        new_maxima = [None] * n_chains
        out = [None] * n_chains

        def score(i):
            p = chains[i][0]
            k_aug = jnp.concatenate([k_ref[pl.ds(r0, t), p * LANES:(p + 1) * LANES], kx], axis=1)
            s = _bdot_nt(k_aug, q_aug[i])
            if keep_all is not True:
                s = jnp.where(visible, s, NEG_INF)
            new_ref[i] = s
            new_maxima[i] = jnp.max(s, axis=0, keepdims=True)

        def fold(i):
            p, hh = chains[i]
            m, l, acc = carry[i]
            m_new = jnp.maximum(m, maxima[i])
            alpha = jnp.exp2(m - m_new)
            prob = jnp.exp2(old_ref[i] - m_new)
            r = (p * 2 + hh) * Dh
            pv = jnp.dot(vt_ref[j_old, r:r + Dh, :], prob.astype(BF16), preferred_element_type=F32)
            out[i] = (m_new, alpha * l + jnp.sum(prob, axis=0, keepdims=True), alpha * acc + pv)

        if j_new is not None:
            for i in range(min(lookahead, n_chains)):
                score(i)
        for i in range(n_chains):
            if j_old is not None:
                fold(i)
            if j_new is not None and i + lookahead < n_chains:
                score(i + lookahead)
        return (tuple(new_maxima) if j_new is not None else None,
                tuple(out) if j_old is not None else carry)

    def finish(rows, carry):
        for p in range(pairs):
            (_, l0, a0), (_, l1, a1) = carry[2 * p], carry[2 * p + 1]
            o = jnp.concatenate([a0 / l0, a1 / l1], axis=0).T
            o_ref[rows, p * LANES:(p + 1) * LANES] = _half_lane_rms(o, og_ref[...], lane_lo).astype(BF16)

    init = tuple((jnp.full((1, t), NEG_INF, F32), jnp.zeros((1, t), F32), jnp.zeros((Dh, t), F32))
                 for _ in chains)
    here, other = sa_ref, sb_ref
    maxima, _ = stage(0, first_q > 0, here, q_augs[0], None, None, None, init)
    for n in range(n_q):
        m = first_q + n
        q_aug = q_augs[n]
        half = m // 2

        def two_blocks(i, state, here=here, other=other, q_aug=q_aug, half=half, last_is_diagonal=(n % 2 == 0)):
            mx, carry = stage(2 * i + 1, True, other, q_aug, 2 * i, here, *state)
            keep_all = (i + 1 < half) if last_is_diagonal else True
            return stage(2 * i + 2, keep_all, here, q_aug, 2 * i + 1, other, mx, carry)

        maxima, carry = lax.fori_loop(0, half, two_blocks, (maxima, init))
        if n % 2 == 1:
            maxima, carry = stage(m, False, other, q_aug, m - 1, here, maxima, carry)
            here, other = other, here
        if n + 1 < n_q:
            maxima, carry = stage(0, True, other, q_augs[n + 1], m, here, maxima, carry)
            here, other = other, here
        else:
            _, carry = stage(None, None, None, None, m, here, maxima, carry)
        finish(slice(n * t, (n + 1) * t), carry)


def _fox_attn(fq, qx, fk, kx, fvt, og, *, batch, seq):
    T = fq.shape[0]
    t = FOX_BLOCK
    nq = seq // t
    pairs = FOX_PAIRS_PER_STEP
    width = pairs * LANES
    rows = min(nq, FOX_QBLOCKS_PER_STEP) * t
    steps = seq // rows
    q_spec = lambda w: pl.BlockSpec((rows, w), lambda b, g, s: (b * steps + s, g))
    return pl.pallas_call(
        functools.partial(_fox_kernel, pairs=pairs),
        grid=(batch, FOX_WIDTH // width, steps),
        in_specs=[q_spec(width),
                  pl.BlockSpec((rows, LANES), lambda b, g, s: (b * steps + s, 0)),
                  pl.BlockSpec((seq, width), lambda b, g, qi: (b, g)),
                  pl.BlockSpec((seq, LANES), lambda b, g, qi: (b, 0)),
                  pl.BlockSpec((nq, width, t), lambda b, g, qi: (b, g, 0)),
                  _const_spec((1, LANES))],
        out_specs=q_spec(width),
        out_shape=jax.ShapeDtypeStruct((T, FOX_WIDTH), BF16),
        scratch_shapes=[pltpu.VMEM((2 * pairs, t, t), F32)] * 2,
        compiler_params=_params(3),
        name="fox_attn",
    )(fq, qx, fk, kx, fvt, og)


def _gdn_kernel(q_ref, k_ref, v_ref, scol_ref, z_ref, og_ref, o_ref,
                state_ref, u_ref, w_ref, qe_ref, intra_ref, kdt_ref, sm_ref, *, groups_per_seq):
    C = GDN_CHUNK
    G = q_ref.shape[0]
    Dh = GDN_HEAD_DIM
    t = pl.program_id(0)
    heads = range(GDN_HEADS)
    col = lambda h: slice(h * Dh, (h + 1) * Dh)

    @pl.when(t == 0)
    def _():
        for ref in (u_ref, w_ref, qe_ref, intra_ref, kdt_ref, sm_ref):
            ref[...] = jnp.zeros(ref.shape, ref.dtype)

    @pl.when((t == 0) | ((t - 1) % groups_per_seq == 0))
    def _():
        state_ref[...] = jnp.zeros(state_ref.shape, F32)

    cur = t % 2
    prev = 1 - cur
    us = [u_ref[prev, :, col(h)] for h in heads]
    ws16 = [w_ref[prev, :, col(h)] for h in heads]
    qes_prev = [qe_ref[prev, :, col(h)] for h in heads]
    intras_prev = [intra_ref[prev, :, col(h)] for h in heads]
    kd_ts_prev = [kdt_ref[prev, h] for h in heads]
    sm_prev = sm_ref[prev]
    states = [state_ref[h] for h in heads]
    outs = [[] for _ in heads]
    pending = {}

    def recur_first(c):
        rs = slice(c * C, (c + 1) * C)
        for h in heads:
            ws = _bdot(jnp.concatenate([ws16[h][rs], qes_prev[h][rs]], axis=0), states[h])
            pending[h] = (ws[C:2 * C], (us[h][rs] - ws[0:C]).astype(BF16))

    def recur_second(c):
        rs = slice(c * C, (c + 1) * C)
        pair = (c * C) // LANES
        for h in heads:
            qs, vb = pending[h]
            zero = jnp.zeros_like(vb)
            outs[h].append(qs + jnp.dot(intras_prev[h][rs], jnp.concatenate([vb, vb], axis=0),
                                        preferred_element_type=F32))
            vpad = jnp.concatenate([vb, zero] if (c * C) % LANES == 0 else [zero, vb], axis=0)
            g_last = sm_prev[(c + 1) * C - 1:(c + 1) * C, DECAY_ROW + h:DECAY_ROW + h + 1]
            states[h] = states[h] * jnp.exp(g_last) + jnp.dot(
                kd_ts_prev[h][:, pair * LANES:(pair + 1) * LANES], vpad, preferred_element_type=F32)

    recur = [f for c in range(G // C) for f in (functools.partial(recur_first, c),
                                                functools.partial(recur_second, c))]

    ri = lax.broadcasted_iota(jnp.int32, (G, G), 0)
    ci = lax.broadcasted_iota(jnp.int32, (G, G), 1)
    same_chunk = (ri // C) == (ci // C)
    lower = same_chunk & (ri >= ci)
    strict = same_chunk & (ri > ci)
    sm = scol_ref[...]
    sm_t = sm.T
    lmats, intras, ys, pws, qes, kd_ts = ([None] * GDN_HEADS for _ in range(6))

    def prepare(h):
        q = q_ref[:, col(h)]
        k = k_ref[:, col(h)]
        beta = sm[:, BETA_ROW + h:BETA_ROW + h + 1]
        gc = sm[:, DECAY_ROW + h:DECAY_ROW + h + 1]
        rem = sm[:, REM_ROW + h:REM_ROW + h + 1]
        gc_row = sm_t[DECAY_ROW + h:DECAY_ROW + h + 1, :]
        eg = jnp.exp(gc)
        decay = jnp.exp(jnp.where(lower, gc - gc_row, NEG_INF))
        kb = k * beta
        a = _bdot_nt(jnp.concatenate([kb, q], axis=0), k)
        lmats[h] = jnp.where(strict, a[0:G] * decay, 0.0).astype(BF16)
        intra = (a[G:2 * G] * decay).astype(BF16)
        intra_fold = intra[:, 0:LANES]
        for j in range(1, G // LANES):
            intra_fold = intra_fold + intra[:, j * LANES:(j + 1) * LANES]
        intras[h] = intra_fold
        ys[h] = jnp.concatenate([v_ref[:, col(h)] * beta, kb * eg], axis=1)
        qes[h] = (q * eg).astype(BF16)
        kd_ts[h] = (k * jnp.exp(rem)).T.astype(BF16)

    def solve_first(h):
        ys[h] = ys[h] - _bdot(lmats[h], ys[h])
        pws[h] = _bdot(lmats[h], lmats[h]).astype(BF16)

    def solve_step(step, h):
        pw = pws[h]
        if step < 4:
            pws[h] = _bdot(pw, pw).astype(BF16)
        ys[h] = ys[h] + _bdot(pw, ys[h])

    for h in heads:
        recur.pop(0)()
        prepare(h)
    for h in heads:
        solve_first(h)
    recur.pop(0)()
    for step in range(5):
        for h in heads:
            solve_step(step, h)
        if recur:
            recur.pop(0)()
    while recur:
        recur.pop(0)()

    for h in heads:
        state_ref[h] = states[h]
        o = jnp.concatenate(outs[h], axis=0)
        o_ref[:, col(h)] = (_rms(o, og_ref[...]) * z_ref[:, col(h)]).astype(BF16)
        u_ref[cur, :, col(h)] = ys[h][:, 0:Dh]
        w_ref[cur, :, col(h)] = ys[h][:, Dh:2 * Dh].astype(BF16)
        qe_ref[cur, :, col(h)] = qes[h]
        intra_ref[cur, :, col(h)] = intras[h]
        kdt_ref[cur, h] = kd_ts[h]
    sm_ref[cur] = sm


def _gdn(gq, gk, gv, z, scol, og, *, batch, seq):
    T = gq.shape[0]
    G = min(seq, GDN_GROUP)
    ng = seq // G
    n = batch * ng
    cur_spec = lambda width: pl.BlockSpec((G, width), lambda t: (jnp.minimum(t, n - 1), 0))
    prev_spec = lambda width: pl.BlockSpec((G, width), lambda t: (jnp.maximum(t - 1, 0), 0))
    return pl.pallas_call(
        functools.partial(_gdn_kernel, groups_per_seq=ng),
        grid=(n + 1,),
        in_specs=[cur_spec(GDN_WIDTH)] * 3 + [cur_spec(LANES), prev_spec(GDN_WIDTH),
                                              _const_spec((1, GDN_HEAD_DIM))],
        out_specs=prev_spec(GDN_WIDTH),
        out_shape=jax.ShapeDtypeStruct((T, GDN_WIDTH), BF16),
        scratch_shapes=[pltpu.VMEM((GDN_HEADS, GDN_HEAD_DIM, GDN_HEAD_DIM), F32),
                        pltpu.VMEM((2, G, GDN_WIDTH), F32), pltpu.VMEM((2, G, GDN_WIDTH), BF16),
                        pltpu.VMEM((2, G, GDN_WIDTH), BF16), pltpu.VMEM((2, G, GDN_WIDTH), BF16),
                        pltpu.VMEM((2, GDN_HEADS, GDN_HEAD_DIM, G), BF16), pltpu.VMEM((2, G, LANES), F32)],
        compiler_params=_params(1),
        name="gdn",
    )(gq, gk, gv, scol, z, og)


def _mix_mlp_kernel(x_ref, oa_ref, ob_ref, wout_ref, gx_ref, wcq_ref, cqg_ref, ck_ref, cv_ref,
                    wco_ref, gm_ref, w1_ref, w2_ref, o_ref, *, ff_block):
    o = jnp.concatenate([oa_ref[...], ob_ref[...]], axis=-1)
    x1 = x_ref[...] + jnp.dot(o, wout_ref[...], preferred_element_type=F32)
    hq = _rms(x1, gx_ref[...])
    cq = _bdot(hq, wcq_ref[...])
    heads = []
    for h in range(XATTN_HEADS):
        sl = slice(h * XATTN_HEAD_DIM, (h + 1) * XATTN_HEAD_DIM)
        s = _bdot_nt(_rms(cq[:, sl], cqg_ref[...]), ck_ref[0, :, sl])
        p = jnp.exp(s - jnp.max(s, axis=-1, keepdims=True))
        pv = jnp.dot(p.astype(BF16), cv_ref[0, :, sl], preferred_element_type=F32)
        heads.append(pv / jnp.sum(p, axis=-1, keepdims=True))
    co = jnp.concatenate(heads, axis=-1)
    x2 = x1 + _bdot(co, wco_ref[...])
    hb = _rms(x2, gm_ref[...]).astype(BF16)
    acc = x2
    for c in range(w1_ref.shape[1] // ff_block):
        sl = slice(c * ff_block, (c + 1) * ff_block)
        a = jnp.maximum(jnp.dot(hb, w1_ref[:, sl], preferred_element_type=F32), 0.0)
        acc = acc + jnp.dot((a * a).astype(BF16), w2_ref[sl, :], preferred_element_type=F32)
    o_ref[...] = acc


def _mix_mlp(x2d, oa, ob, w_out, gx, w_cq, cqg, ck, cv, w_co, gm, w1, w2, *, seq, tm, ff_block):
    T, D = x2d.shape
    M = ck.shape[1]
    tiles_per_seq = seq // tm
    row_spec = lambda width: pl.BlockSpec((tm, width), lambda i: (i, 0))
    mem_spec = pl.BlockSpec((1, M, XATTN_WIDTH), lambda i: (i // tiles_per_seq, 0, 0))
    return pl.pallas_call(
        functools.partial(_mix_mlp_kernel, ff_block=ff_block),
        grid=(T // tm,),
        in_specs=[row_spec(D), row_spec(FOX_WIDTH), row_spec(GDN_WIDTH), _const_spec(w_out.shape),
                  _const_spec((1, D)), _const_spec(w_cq.shape), _const_spec((1, XATTN_HEAD_DIM)),
                  mem_spec, mem_spec, _const_spec(w_co.shape),
                  _const_spec((1, D)), _const_spec(w1.shape), _const_spec(w2.shape)],
        out_specs=row_spec(D),
        out_shape=jax.ShapeDtypeStruct((T, D), F32),
        compiler_params=_params(1),
        name="mix_mlp",
    )(x2d, oa, ob, w_out, gx, w_cq, cqg, ck, cv, w_co, gm, w1, w2)


def _row_tile(seq):
    return min(seq, 512)


def _layer(x, mem, norm_mix_g, w_in, fox_qnorm_g, fox_knorm_g, fox_f_bias, fox_onorm_g,
           gdn_conv_w, gdn_A_log, gdn_dt_bias, gdn_onorm_g, w_out,
           norm_xattn_g, mem_norm_g, w_cq, w_ckv, xattn_qnorm_g, xattn_knorm_g, w_co,
           norm_mlp_g, w_mlp1, w_mlp2):
    B, S, D = x.shape
    T = B * S
    tm = _row_tile(S)
    row = lambda v: v.reshape(1, -1).astype(F32)

    o_f = 3 * FOX_WIDTH
    o_g = o_f + FOX_HEADS
    o_b = o_g + 3 * GDN_WIDTH
    o_a = o_b + GDN_HEADS
    o_z = o_a + GDN_HEADS
    w_small = jnp.concatenate([w_in[:, o_f:o_g], w_in[:, o_b:o_z], w_in[:, o_a:o_z]], axis=1)
    w_small = jnp.pad(w_small, ((0, 0), (0, LANES - w_small.shape[1])))
    w_all = jnp.concatenate([w_in[:, :o_f], w_in[:, o_g:o_b], w_in[:, o_z:], w_small], axis=1).astype(BF16)
    pad4 = jnp.zeros((GDN_HEADS,), F32)
    bias_col = jnp.concatenate([fox_f_bias, pad4, gdn_dt_bias, gdn_dt_bias, pad4]).reshape(-1, 1)
    alog_col = jnp.concatenate([jnp.zeros((DECAY_ROW,), F32), gdn_A_log, gdn_A_log, pad4]).reshape(-1, 1)
    qg = row(jnp.tile(fox_qnorm_g, 2)) * (FOX_HEAD_DIM ** -0.5 * LOG2E)
    kg = row(jnp.tile(fox_knorm_g, 2))
    og = row(jnp.tile(fox_onorm_g, 2))

    x2d = x.reshape(T, D)
    ck, cv = _mem_kv(mem, row(mem_norm_g), w_ckv.astype(BF16), row(xattn_knorm_g))
    fq, fk, fvt, gq, gk, gv, z, scol, kx, qx = _in_proj(
        x2d, row(norm_mix_g), w_all, qg, kg, gdn_conv_w, bias_col, alog_col, seq=S, tm=tm)
    o_fox = _fox_attn(fq, qx, fk, kx, fvt, og, batch=B, seq=S)
    o_gdn = _gdn(gq, gk, gv, z, scol, row(gdn_onorm_g), batch=B, seq=S)
    out = _mix_mlp(x2d, o_fox, o_gdn, w_out.astype(BF16), row(norm_xattn_g), w_cq.astype(BF16),
                   row(xattn_qnorm_g) * (XATTN_HEAD_DIM ** -0.5), ck, cv, w_co.astype(BF16),
                   row(norm_mlp_g), w_mlp1.astype(BF16), w_mlp2.astype(BF16), seq=S, tm=tm, ff_block=1024)
    return out.reshape(B, S, D)


def kernel(x, mem, norm_mix_g, w_in, fox_qnorm_g, fox_knorm_g, fox_f_bias, fox_onorm_g, gdn_conv_w, gdn_A_log, gdn_dt_bias, gdn_onorm_g, w_out, norm_xattn_g, mem_norm_g, w_cq, w_ckv, xattn_qnorm_g, xattn_knorm_g, w_co, norm_mlp_g, w_mlp1, w_mlp2):
    for l in range(w_in.shape[0]):
        x = _layer(x, mem, norm_mix_g[l], w_in[l], fox_qnorm_g[l], fox_knorm_g[l], fox_f_bias[l],
                   fox_onorm_g[l], gdn_conv_w[l], gdn_A_log[l], gdn_dt_bias[l], gdn_onorm_g[l],
                   w_out[l], norm_xattn_g[l], mem_norm_g[l], w_cq[l], w_ckv[l], xattn_qnorm_g[l],
                   xattn_knorm_g[l], w_co[l], norm_mlp_g[l], w_mlp1[l], w_mlp2[l])
    return x
```

```python
import functools

import jax
import jax.numpy as jnp
from jax import lax
from jax.experimental import pallas as pl
from jax.experimental.pallas import tpu as pltpu

EPS = 1e-6
NEG_INF = -1e30
LOG2E = 1.4426950408889634

FOX_HEADS = 8
FOX_HEAD_DIM = 64
FOX_WIDTH = FOX_HEADS * FOX_HEAD_DIM
FOX_BLOCK = 256
FOX_PAIRS_PER_STEP = 4
FOX_QBLOCKS_PER_STEP = 4
GDN_HEADS = 4
GDN_HEAD_DIM = 128
GDN_WIDTH = GDN_HEADS * GDN_HEAD_DIM
CONV_WIDTH = 4
GDN_CHUNK = 64
GDN_GROUP = 256
XATTN_HEADS = 4
XATTN_HEAD_DIM = 128
XATTN_WIDTH = XATTN_HEADS * XATTN_HEAD_DIM

LANES = 128
SUBLANES = 8
SMALL_ROWS = 24
BETA_ROW = FOX_HEADS
DECAY_ROW = BETA_ROW + GDN_HEADS
REM_ROW = DECAY_ROW + GDN_HEADS
VMEM_LIMIT_BYTES = 56 * 1024 * 1024

F32 = jnp.float32
BF16 = jnp.bfloat16


def _bdot(a, b):
    return jnp.dot(a.astype(BF16), b.astype(BF16), preferred_element_type=F32)


def _bdot_nt(a, b):
    return lax.dot_general(a.astype(BF16), b.astype(BF16), (((1,), (1,)), ((), ())),
                           preferred_element_type=F32)


def _rms(x, gain):
    return x * lax.rsqrt(jnp.mean(x * x, axis=-1, keepdims=True) + EPS) * gain


def _half_lane_rms(x, gain, lane_lo):
    ss = x * x
    s_lo = jnp.sum(jnp.where(lane_lo, ss, 0.0), axis=-1, keepdims=True)
    s_hi = jnp.sum(jnp.where(lane_lo, 0.0, ss), axis=-1, keepdims=True)
    ms = jnp.where(lane_lo, s_lo, s_hi) * (1.0 / FOX_HEAD_DIM)
    return x * lax.rsqrt(ms + EPS) * gain


def _softplus(y):
    return jnp.maximum(y, 0.0) + jnp.log1p(jnp.exp(-jnp.abs(y)))


def _silu(y):
    return y / (1.0 + jnp.exp2(y * (-LOG2E)))


def _split3_bf16(v):
    p1 = v.astype(BF16)
    r1 = v - p1.astype(F32)
    p2 = r1.astype(BF16)
    r2 = r1 - p2.astype(F32)
    return p1, p2, r2.astype(BF16)


def _const_spec(shape):
    return pl.BlockSpec(shape, lambda *_: (0,) * len(shape), pipeline_mode=pl.Buffered(1))


def _params(n_grid):
    return pltpu.CompilerParams(dimension_semantics=("arbitrary",) * n_grid,
                                vmem_limit_bytes=VMEM_LIMIT_BYTES)


def _mem_kv_kernel(mem_ref, g_ref, w_ref, kg_ref, ck_ref, cv_ref):
    hm = _rms(mem_ref[0], g_ref[...])
    kv = _bdot(hm, w_ref[...])
    for h in range(XATTN_HEADS):
        sl = slice(h * XATTN_HEAD_DIM, (h + 1) * XATTN_HEAD_DIM)
        ck_ref[0, :, sl] = _rms(kv[:, sl], kg_ref[...]).astype(BF16)
    cv_ref[0] = kv[:, XATTN_WIDTH:].astype(BF16)


def _mem_kv(mem, g, w_ckv, kg):
    B, M, D = mem.shape
    return pl.pallas_call(
        _mem_kv_kernel,
        grid=(B,),
        in_specs=[pl.BlockSpec((1, M, D), lambda b: (b, 0, 0)),
                  _const_spec((1, D)), _const_spec(w_ckv.shape), _const_spec((1, XATTN_HEAD_DIM))],
        out_specs=[pl.BlockSpec((1, M, XATTN_WIDTH), lambda b: (b, 0, 0))] * 2,
        out_shape=[jax.ShapeDtypeStruct((B, M, XATTN_WIDTH), BF16)] * 2,
        compiler_params=_params(1),
        name="mem_kv",
    )(mem, g, w_ckv, kg)


def _in_proj_kernel(x_ref, g_ref, w_ref, qg_ref, kg_ref, convw_ref, bias_ref, alog_ref,
                    fq_ref, fk_ref, fvt_ref, gq_ref, gk_ref, gv_ref, z_ref, scol_ref, kx_ref, qx_ref,
                    gbuf_q, gbuf_k, gbuf_v, carry_ref, *, tiles_per_seq):
    tm = x_ref.shape[0]
    i = pl.program_id(0)
    gbufs = (gbuf_q, gbuf_k, gbuf_v)

    @pl.when(i % tiles_per_seq == 0)
    def _():
        for gbuf in gbufs:
            gbuf[0:SUBLANES, :] = jnp.zeros((SUBLANES, GDN_WIDTH), F32)
        carry_ref[...] = jnp.zeros(carry_ref.shape, F32)

    hb = _rms(x_ref[...], g_ref[...]).astype(BF16)
    lane_lo = lax.broadcasted_iota(jnp.int32, (1, LANES), 1) < FOX_HEAD_DIM
    off_gdn = 3 * FOX_WIDTH
    off_z = off_gdn + 3 * GDN_WIDTH
    off_small = off_z + GDN_WIDTH
    proj = lambda off, width: jnp.dot(hb, w_ref[:, off:off + width], preferred_element_type=F32)

    def gdn_proj(part):
        gbufs[part][SUBLANES:SUBLANES + tm, :] = proj(off_gdn + part * GDN_WIDTH, GDN_WIDTH)

    def gdn_head(part, h):
        gbuf, out_ref = gbufs[part], (gq_ref, gk_ref, gv_ref)[part]
        cs = slice(h * GDN_HEAD_DIM, (h + 1) * GDN_HEAD_DIM)
        ws = slice(part * GDN_WIDTH + h * GDN_HEAD_DIM, part * GDN_WIDTH + (h + 1) * GDN_HEAD_DIM)
        acc = convw_ref[CONV_WIDTH - 1:CONV_WIDTH, ws] * gbuf[SUBLANES:SUBLANES + tm, cs]
        for d in range(1, CONV_WIDTH):
            acc = acc + (convw_ref[CONV_WIDTH - 1 - d:CONV_WIDTH - d, ws]
                         * gbuf[SUBLANES - d:SUBLANES - d + tm, cs])
        y = _silu(acc)
        if part == 0:
            y = y * lax.rsqrt(jnp.sum(y * y, axis=-1, keepdims=True) + EPS) * (GDN_HEAD_DIM ** -0.5)
        elif part == 1:
            y = y * lax.rsqrt(jnp.sum(y * y, axis=-1, keepdims=True) + EPS)
        out_ref[:, cs] = y
        gbuf[0:SUBLANES, cs] = gbuf[tm:tm + SUBLANES, cs]

    MXU_COLS = 2 * LANES

    def fox_qk(out_ref, gain_ref, off, c0):
        y = proj(off + c0, MXU_COLS)
        for j in range(MXU_COLS // LANES):
            sl = slice(c0 + j * LANES, c0 + (j + 1) * LANES)
            out_ref[:, sl] = _half_lane_rms(y[:, j * LANES:(j + 1) * LANES], gain_ref[...],
                                            lane_lo).astype(BF16)

    def fox_v(c0):
        y = proj(2 * FOX_WIDTH + c0, MXU_COLS)
        for r in range(tm // FOX_BLOCK):
            fvt_ref[r, c0:c0 + MXU_COLS, :] = y[r * FOX_BLOCK:(r + 1) * FOX_BLOCK, :].T.astype(BF16)

    def gate(c0):
        z_ref[:, c0:c0 + MXU_COLS] = _silu(proj(off_z + c0, MXU_COLS))

    row = lax.broadcasted_iota(jnp.int32, (SMALL_ROWS, tm), 0)
    is_fox = row < BETA_ROW
    is_beta = row < DECAY_ROW

    def scalars_gates():
        xb = proj(off_small, LANES).T[0:SMALL_ROWS, :] + bias_ref[...]
        sp = _softplus(jnp.where(is_fox, -xb, xb))
        return jnp.where(is_fox, -sp,
                         jnp.where(is_beta, 1.0 / (1.0 + jnp.exp(-xb)), -jnp.exp(alog_ref[...]) * sp))

    def scalars_scans(val):
        ti = lax.broadcasted_iota(jnp.int32, (tm, tm), 0)
        tj = lax.broadcasted_iota(jnp.int32, (tm, tm), 1)
        same_chunk = (ti // GDN_CHUNK) == (tj // GDN_CHUNK)
        one_hot = lambda m: jnp.where(m, 1.0, 0.0).astype(BF16)
        scan_mats = jnp.concatenate([one_hot(ti <= tj), one_hot((ti <= tj) & same_chunk),
                                     one_hot((ti > tj) & same_chunk)], axis=1)
        sums = jnp.zeros((SMALL_ROWS, 3 * tm), F32)
        for part in _split3_bf16(val):
            sums = sums + jnp.dot(part, scan_mats, preferred_element_type=F32)
        cs_full = sums[:, 0:tm] + carry_ref[:, 0:1]
        carry_ref[...] = jnp.broadcast_to(cs_full[:, tm - 1:tm], carry_ref.shape)
        return jnp.where(is_fox, cs_full, jnp.where(is_beta, val, jnp.where(
            row < REM_ROW, sums[:, tm:2 * tm], sums[:, 2 * tm:])))

    def scalars_columns(rows):
        padded = jnp.concatenate([rows, jnp.zeros((LANES - SMALL_ROWS, tm), F32)], axis=0)
        scol_ref[...] = padded.T

    def scalars_bias_lanes(rows):
        c_parts = [p.astype(F32) for p in _split3_bf16(rows[0:FOX_HEADS, :] * LOG2E)]
        zeros = lambda n: jnp.zeros((n, tm), F32)
        ones = jnp.ones((3 * FOX_HEADS, tm), F32)
        kx = jnp.concatenate([-p for p in c_parts] + [zeros(FOX_HEADS), ones, zeros(LANES - 7 * FOX_HEADS)],
                             axis=0)
        qx = jnp.concatenate([ones, zeros(FOX_HEADS)] + c_parts + [zeros(LANES - 7 * FOX_HEADS)], axis=0)
        kx_ref[...] = kx.T.astype(BF16)
        qx_ref[...] = qx.T.astype(BF16)

    val = scalars_gates()
    gdn_proj(0)
    gdn_proj(1)
    gdn_proj(2)
    rows = scalars_scans(val)
    light = ([functools.partial(fox_qk, fq_ref, qg_ref, 0, c) for c in range(0, FOX_WIDTH, MXU_COLS)]
             + [functools.partial(fox_qk, fk_ref, kg_ref, FOX_WIDTH, c) for c in range(0, FOX_WIDTH, MXU_COLS)]
             + [functools.partial(scalars_columns, rows)]
             + [functools.partial(fox_v, c) for c in range(0, FOX_WIDTH, MXU_COLS)]
             + [functools.partial(scalars_bias_lanes, rows)]
             + [functools.partial(gate, c) for c in range(0, GDN_WIDTH, MXU_COLS)])
    for part in range(3):
        for h in range(GDN_HEADS):
            gdn_head(part, h)
            if light:
                light.pop(0)()
    while light:
        light.pop(0)()


def _in_proj(x2d, g, w_all, qg, kg, conv_w, bias_col, alog_col, *, seq, tm):
    T, D = x2d.shape
    n = T // tm
    row_spec = lambda width: pl.BlockSpec((tm, width), lambda i: (i, 0))
    kernel = functools.partial(_in_proj_kernel, tiles_per_seq=seq // tm)
    return pl.pallas_call(
        kernel,
        grid=(n,),
        in_specs=[row_spec(D), _const_spec((1, D)), _const_spec(w_all.shape),
                  _const_spec((1, LANES)), _const_spec((1, LANES)), _const_spec(conv_w.shape),
                  _const_spec((SMALL_ROWS, 1)), _const_spec((SMALL_ROWS, 1))],
        out_specs=[row_spec(FOX_WIDTH)] * 2
        + [pl.BlockSpec((tm // FOX_BLOCK, FOX_WIDTH, FOX_BLOCK), lambda i: (i, 0, 0))]
        + [row_spec(GDN_WIDTH)] * 4 + [row_spec(LANES)] * 3,
        out_shape=[jax.ShapeDtypeStruct((T, FOX_WIDTH), BF16)] * 2
        + [jax.ShapeDtypeStruct((T // FOX_BLOCK, FOX_WIDTH, FOX_BLOCK), BF16)]
        + [jax.ShapeDtypeStruct((T, GDN_WIDTH), F32)] * 4
        + [jax.ShapeDtypeStruct((T, LANES), F32)] + [jax.ShapeDtypeStruct((T, LANES), BF16)] * 2,
        scratch_shapes=[pltpu.VMEM((SUBLANES + tm, GDN_WIDTH), F32)] * 3
        + [pltpu.VMEM((SMALL_ROWS, LANES), F32)],
        compiler_params=_params(1),
        name="in_proj",
    )(x2d, g, w_all, qg, kg, conv_w, bias_col, alog_col)


def _fox_kernel(q_ref, qx_ref, k_ref, kx_ref, vt_ref, og_ref, o_ref, sa_ref, sb_ref, *, pairs):
    t = FOX_BLOCK
    Dh = FOX_HEAD_DIM
    n_q = q_ref.shape[0] // t
    first_q = pl.program_id(2) * n_q
    lane = lax.broadcasted_iota(jnp.int32, (1, LANES), 1)
    lane_lo = lane < Dh
    in_bias_lanes = (lane < 3 * FOX_HEADS) | ((lane >= 4 * FOX_HEADS) & (lane < 7 * FOX_HEADS))
    chains = [(p, hh) for p in range(pairs) for hh in range(2)]

    def augmented_queries(rows):
        qx = qx_ref[rows, :]
        out = []
        for p, hh in chains:
            h = (pl.program_id(1) * pairs + p) * 2 + hh
            q = q_ref[rows, p * LANES:(p + 1) * LANES]
            q = jnp.where(lane_lo if hh == 0 else jnp.logical_not(lane_lo), q, jnp.zeros_like(q))
            qx_h = jnp.where(in_bias_lanes & (lane % FOX_HEADS == h), qx, jnp.zeros_like(qx))
            out.append(jnp.concatenate([q, qx_h], axis=1))
        return out

    q_augs = [augmented_queries(slice(n * t, (n + 1) * t)) for n in range(n_q)]
    n_chains = len(chains)
    bf16_rows = 2 * SUBLANES
    ones_row = jnp.where(lax.broadcasted_iota(jnp.int32, (bf16_rows, t), 0) == 0, 1.0, 0.0).astype(BF16)
    lookahead = 2

    def stage(j_new, keep_all, new_ref, q_aug, j_old, old_ref, maxima, carry):
        if j_new is not None:
            r0 = pl.multiple_of(j_new * t, t)
            kx = kx_ref[pl.ds(r0, t), :]
            if keep_all is not True:
                visible = (lax.broadcasted_iota(jnp.int32, (t, t), 0)
                           <= lax.broadcasted_iota(jnp.int32, (t, t), 1))
                if keep_all is not False:
                    visible = visible | keep_all
        new_maxima = [None] * n_chains
        out = [None] * n_chains

        def score(i):
            p = chains[i][0]
            k_aug = jnp.concatenate([k_ref[pl.ds(r0, t), p * LANES:(p + 1) * LANES], kx], axis=1)
            s = _bdot_nt(k_aug, q_aug[i])
            if keep_all is not True:
                s = jnp.where(visible, s, NEG_INF)
            new_ref[i] = s
            new_maxima[i] = jnp.max(s, axis=0, keepdims=True)

        def fold(i):
            p, hh = chains[i]
            m, l, acc = carry[i]
            m_new = jnp.maximum(m, maxima[i])
            alpha = jnp.exp2(m - m_new)
            prob = jnp.exp2(old_ref[i] - m_new).astype(BF16)
            r = (p * 2 + hh) * Dh
            pv = jnp.dot(jnp.concatenate([vt_ref[j_old, r:r + Dh, :], ones_row], axis=0), prob,
                         preferred_element_type=F32)
            out[i] = (m_new, alpha * l + pv[Dh:Dh + 1, :], alpha * acc + pv[0:Dh, :])

        if j_new is not None:
            for i in range(min(lookahead, n_chains)):
                score(i)
        for i in range(n_chains):
            if j_old is not None:
                fold(i)
            if j_new is not None and i + lookahead < n_chains:
                score(i + lookahead)
        return (tuple(new_maxima) if j_new is not None else None,
                tuple(out) if j_old is not None else carry)

    def finish(rows, carry):
        for p in range(pairs):
            (_, l0, a0), (_, l1, a1) = carry[2 * p], carry[2 * p + 1]
            o = jnp.concatenate([a0 / l0, a1 / l1], axis=0).T
            o_ref[rows, p * LANES:(p + 1) * LANES] = _half_lane_rms(o, og_ref[...], lane_lo).astype(BF16)

    init = tuple((jnp.full((1, t), NEG_INF, F32), jnp.zeros((1, t), F32), jnp.zeros((Dh, t), F32))
                 for _ in chains)
    here, other = sa_ref, sb_ref
    maxima, _ = stage(0, first_q > 0, here, q_augs[0], None, None, None, init)
    for n in range(n_q):
        m = first_q + n
        q_aug = q_augs[n]
        half = m // 2

        def two_blocks(i, state, here=here, other=other, q_aug=q_aug, half=half, last_is_diagonal=(n % 2 == 0)):
            mx, carry = stage(2 * i + 1, True, other, q_aug, 2 * i, here, *state)
            keep_all = (i + 1 < half) if last_is_diagonal else True
            return stage(2 * i + 2, keep_all, here, q_aug, 2 * i + 1, other, mx, carry)

        maxima, carry = lax.fori_loop(0, half, two_blocks, (maxima, init))
        if n % 2 == 1:
            maxima, carry = stage(m, False, other, q_aug, m - 1, here, maxima, carry)
            here, other = other, here
        if n + 1 < n_q:
            maxima, carry = stage(0, True, other, q_augs[n + 1], m, here, maxima, carry)
            here, other = other, here
        else:
            _, carry = stage(None, None, None, None, m, here, maxima, carry)
        finish(slice(n * t, (n + 1) * t), carry)


def _fox_attn(fq, qx, fk, kx, fvt, og, *, batch, seq):
    T = fq.shape[0]
    t = FOX_BLOCK
    nq = seq // t
    pairs = FOX_PAIRS_PER_STEP
    width = pairs * LANES
    rows = min(nq, FOX_QBLOCKS_PER_STEP) * t
    steps = seq // rows
    q_spec = lambda w: pl.BlockSpec((rows, w), lambda b, g, s: (b * steps + s, g))
    return pl.pallas_call(
        functools.partial(_fox_kernel, pairs=pairs),
        grid=(batch, FOX_WIDTH // width, steps),
        in_specs=[q_spec(width),
                  pl.BlockSpec((rows, LANES), lambda b, g, s: (b * steps + s, 0)),
                  pl.BlockSpec((seq, width), lambda b, g, qi: (b, g)),
                  pl.BlockSpec((seq, LANES), lambda b, g, qi: (b, 0)),
                  pl.BlockSpec((nq, width, t), lambda b, g, qi: (b, g, 0)),
                  _const_spec((1, LANES))],
        out_specs=q_spec(width),
        out_shape=jax.ShapeDtypeStruct((T, FOX_WIDTH), BF16),
        scratch_shapes=[pltpu.VMEM((2 * pairs, t, t), F32)] * 2,
        compiler_params=_params(3),
        name="fox_attn",
    )(fq, qx, fk, kx, fvt, og)


def _gdn_kernel(q_ref, k_ref, v_ref, scol_ref, z_ref, og_ref, o_ref,
                state_ref, u_ref, w_ref, qe_ref, intra_ref, kdt_ref, sm_ref, *, groups_per_seq):
    C = GDN_CHUNK
    G = q_ref.shape[0]
    Dh = GDN_HEAD_DIM
    t = pl.program_id(0)
    heads = range(GDN_HEADS)
    col = lambda h: slice(h * Dh, (h + 1) * Dh)

    @pl.when(t == 0)
    def _():
        for ref in (u_ref, w_ref, qe_ref, intra_ref, kdt_ref, sm_ref):
            ref[...] = jnp.zeros(ref.shape, ref.dtype)

    @pl.when((t == 0) | ((t - 1) % groups_per_seq == 0))
    def _():
        state_ref[...] = jnp.zeros(state_ref.shape, F32)

    cur = t % 2
    prev = 1 - cur
    us = [u_ref[prev, :, col(h)] for h in heads]
    ws16 = [w_ref[prev, :, col(h)] for h in heads]
    qes_prev = [qe_ref[prev, :, col(h)] for h in heads]
    intras_prev = [intra_ref[prev, :, col(h)] for h in heads]
    kd_ts_prev = [kdt_ref[prev, h] for h in heads]
    sm_prev = sm_ref[prev]
    states = [state_ref[h] for h in heads]
    outs = [[] for _ in heads]
    pending = {}

    def recur_first(c):
        rs = slice(c * C, (c + 1) * C)
        for h in heads:
            ws = _bdot(jnp.concatenate([ws16[h][rs], qes_prev[h][rs]], axis=0), states[h])
            pending[h] = (ws[C:2 * C], (us[h][rs] - ws[0:C]).astype(BF16))

    def recur_second(c):
        rs = slice(c * C, (c + 1) * C)
        pair = (c * C) // LANES
        for h in heads:
            qs, vb = pending[h]
            zero = jnp.zeros_like(vb)
            outs[h].append(qs + jnp.dot(intras_prev[h][rs], jnp.concatenate([vb, vb], axis=0),
                                        preferred_element_type=F32))
            vpad = jnp.concatenate([vb, zero] if (c * C) % LANES == 0 else [zero, vb], axis=0)
            g_last = sm_prev[(c + 1) * C - 1:(c + 1) * C, DECAY_ROW + h:DECAY_ROW + h + 1]
            states[h] = states[h] * jnp.exp(g_last) + jnp.dot(
                kd_ts_prev[h][:, pair * LANES:(pair + 1) * LANES], vpad, preferred_element_type=F32)

    recur = [f for c in range(G // C) for f in (functools.partial(recur_first, c),
                                                functools.partial(recur_second, c))]

    ri = lax.broadcasted_iota(jnp.int32, (G, G), 0)
    ci = lax.broadcasted_iota(jnp.int32, (G, G), 1)
    same_chunk = (ri // C) == (ci // C)
    lower = same_chunk & (ri >= ci)
    strict = same_chunk & (ri > ci)
    sm = scol_ref[...]
    sm_t = sm.T
    lmats, intras, ys, pws, qes, kd_ts = ([None] * GDN_HEADS for _ in range(6))

    def prepare(h):
        q = q_ref[:, col(h)]
        k = k_ref[:, col(h)]
        beta = sm[:, BETA_ROW + h:BETA_ROW + h + 1]
        gc = sm[:, DECAY_ROW + h:DECAY_ROW + h + 1]
        rem = sm[:, REM_ROW + h:REM_ROW + h + 1]
        gc_row = sm_t[DECAY_ROW + h:DECAY_ROW + h + 1, :]
        eg = jnp.exp(gc)
        decay = jnp.exp(jnp.where(lower, gc - gc_row, NEG_INF))
        kb = k * beta
        a = _bdot_nt(jnp.concatenate([kb, q], axis=0), k)
        lmats[h] = jnp.where(strict, a[0:G] * decay, 0.0).astype(BF16)
        intra = (a[G:2 * G] * decay).astype(BF16)
        intra_fold = intra[:, 0:LANES]
        for j in range(1, G // LANES):
            intra_fold = intra_fold + intra[:, j * LANES:(j + 1) * LANES]
        intras[h] = intra_fold
        ys[h] = jnp.concatenate([v_ref[:, col(h)] * beta, kb * eg], axis=1)
        qes[h] = (q * eg).astype(BF16)
        kd_ts[h] = (k * jnp.exp(rem)).T.astype(BF16)

    def solve_first(h):
        ys[h] = ys[h] - _bdot(lmats[h], ys[h])
        pws[h] = _bdot(lmats[h], lmats[h]).astype(BF16)

    def solve_step(step, h):
        pw = pws[h]
        if step < 4:
            pws[h] = _bdot(pw, pw).astype(BF16)
        ys[h] = ys[h] + _bdot(pw, ys[h])

    for h in heads:
        recur.pop(0)()
        prepare(h)
    for h in heads:
        solve_first(h)
    recur.pop(0)()
    for step in range(5):
        for h in heads:
            solve_step(step, h)
        if recur:
            recur.pop(0)()
    while recur:
        recur.pop(0)()

    for h in heads:
        state_ref[h] = states[h]
        o = jnp.concatenate(outs[h], axis=0)
        o_ref[:, col(h)] = (_rms(o, og_ref[...]) * z_ref[:, col(h)]).astype(BF16)
        u_ref[cur, :, col(h)] = ys[h][:, 0:Dh]
        w_ref[cur, :, col(h)] = ys[h][:, Dh:2 * Dh].astype(BF16)
        qe_ref[cur, :, col(h)] = qes[h]
        intra_ref[cur, :, col(h)] = intras[h]
        kdt_ref[cur, h] = kd_ts[h]
    sm_ref[cur] = sm


def _gdn(gq, gk, gv, z, scol, og, *, batch, seq):
    T = gq.shape[0]
    G = min(seq, GDN_GROUP)
    ng = seq // G
    n = batch * ng
    cur_spec = lambda width: pl.BlockSpec((G, width), lambda t: (jnp.minimum(t, n - 1), 0))
    prev_spec = lambda width: pl.BlockSpec((G, width), lambda t: (jnp.maximum(t - 1, 0), 0))
    return pl.pallas_call(
        functools.partial(_gdn_kernel, groups_per_seq=ng),
        grid=(n + 1,),
        in_specs=[cur_spec(GDN_WIDTH)] * 3 + [cur_spec(LANES), prev_spec(GDN_WIDTH),
                                              _const_spec((1, GDN_HEAD_DIM))],
        out_specs=prev_spec(GDN_WIDTH),
        out_shape=jax.ShapeDtypeStruct((T, GDN_WIDTH), BF16),
        scratch_shapes=[pltpu.VMEM((GDN_HEADS, GDN_HEAD_DIM, GDN_HEAD_DIM), F32),
                        pltpu.VMEM((2, G, GDN_WIDTH), F32), pltpu.VMEM((2, G, GDN_WIDTH), BF16),
                        pltpu.VMEM((2, G, GDN_WIDTH), BF16), pltpu.VMEM((2, G, GDN_WIDTH), BF16),
                        pltpu.VMEM((2, GDN_HEADS, GDN_HEAD_DIM, G), BF16), pltpu.VMEM((2, G, LANES), F32)],
        compiler_params=_params(1),
        name="gdn",
    )(gq, gk, gv, scol, z, og)


def _mix_mlp_kernel(x_ref, oa_ref, ob_ref, wout_ref, gx_ref, wcq_ref, cqg_ref, ck_ref, cv_ref,
                    wco_ref, gm_ref, w1_ref, w2_ref, o_ref, *, ff_block):
    o = jnp.concatenate([oa_ref[...], ob_ref[...]], axis=-1)
    x1 = x_ref[...] + jnp.dot(o, wout_ref[...], preferred_element_type=F32)
    hq = _rms(x1, gx_ref[...])
    cq = _bdot(hq, wcq_ref[...])
    heads = []
    for h in range(XATTN_HEADS):
        sl = slice(h * XATTN_HEAD_DIM, (h + 1) * XATTN_HEAD_DIM)
        s = _bdot_nt(_rms(cq[:, sl], cqg_ref[...]), ck_ref[0, :, sl])
        p = jnp.exp(s - jnp.max(s, axis=-1, keepdims=True))
        pv = jnp.dot(p.astype(BF16), cv_ref[0, :, sl], preferred_element_type=F32)
        heads.append(pv / jnp.sum(p, axis=-1, keepdims=True))
    co = jnp.concatenate(heads, axis=-1)
    x2 = x1 + _bdot(co, wco_ref[...])
    hb = _rms(x2, gm_ref[...]).astype(BF16)
    acc = x2
    for c in range(w1_ref.shape[1] // ff_block):
        sl = slice(c * ff_block, (c + 1) * ff_block)
        a = jnp.maximum(jnp.dot(hb, w1_ref[:, sl], preferred_element_type=F32), 0.0)
        acc = acc + jnp.dot((a * a).astype(BF16), w2_ref[sl, :], preferred_element_type=F32)
    o_ref[...] = acc


def _mix_mlp(x2d, oa, ob, w_out, gx, w_cq, cqg, ck, cv, w_co, gm, w1, w2, *, seq, tm, ff_block):
    T, D = x2d.shape
    M = ck.shape[1]
    tiles_per_seq = seq // tm
    row_spec = lambda width: pl.BlockSpec((tm, width), lambda i: (i, 0))
    mem_spec = pl.BlockSpec((1, M, XATTN_WIDTH), lambda i: (i // tiles_per_seq, 0, 0))
    return pl.pallas_call(
        functools.partial(_mix_mlp_kernel, ff_block=ff_block),
        grid=(T // tm,),
        in_specs=[row_spec(D), row_spec(FOX_WIDTH), row_spec(GDN_WIDTH), _const_spec(w_out.shape),
                  _const_spec((1, D)), _const_spec(w_cq.shape), _const_spec((1, XATTN_HEAD_DIM)),
                  mem_spec, mem_spec, _const_spec(w_co.shape),
                  _const_spec((1, D)), _const_spec(w1.shape), _const_spec(w2.shape)],
        out_specs=row_spec(D),
        out_shape=jax.ShapeDtypeStruct((T, D), F32),
        compiler_params=_params(1),
        name="mix_mlp",
    )(x2d, oa, ob, w_out, gx, w_cq, cqg, ck, cv, w_co, gm, w1, w2)


def _row_tile(seq):
    return min(seq, 512)


def _layer(x, mem, norm_mix_g, w_in, fox_qnorm_g, fox_knorm_g, fox_f_bias, fox_onorm_g,
           gdn_conv_w, gdn_A_log, gdn_dt_bias, gdn_onorm_g, w_out,
           norm_xattn_g, mem_norm_g, w_cq, w_ckv, xattn_qnorm_g, xattn_knorm_g, w_co,
           norm_mlp_g, w_mlp1, w_mlp2):
    B, S, D = x.shape
    T = B * S
    tm = _row_tile(S)
    row = lambda v: v.reshape(1, -1).astype(F32)

    o_f = 3 * FOX_WIDTH
    o_g = o_f + FOX_HEADS
    o_b = o_g + 3 * GDN_WIDTH
    o_a = o_b + GDN_HEADS
    o_z = o_a + GDN_HEADS
    w_small = jnp.concatenate([w_in[:, o_f:o_g], w_in[:, o_b:o_z], w_in[:, o_a:o_z]], axis=1)
    w_small = jnp.pad(w_small, ((0, 0), (0, LANES - w_small.shape[1])))
    w_all = jnp.concatenate([w_in[:, :o_f], w_in[:, o_g:o_b], w_in[:, o_z:], w_small], axis=1).astype(BF16)
    pad4 = jnp.zeros((GDN_HEADS,), F32)
    bias_col = jnp.concatenate([fox_f_bias, pad4, gdn_dt_bias, gdn_dt_bias, pad4]).reshape(-1, 1)
    alog_col = jnp.concatenate([jnp.zeros((DECAY_ROW,), F32), gdn_A_log, gdn_A_log, pad4]).reshape(-1, 1)
    qg = row(jnp.tile(fox_qnorm_g, 2)) * (FOX_HEAD_DIM ** -0.5 * LOG2E)
    kg = row(jnp.tile(fox_knorm_g, 2))
    og = row(jnp.tile(fox_onorm_g, 2))

    x2d = x.reshape(T, D)
    ck, cv = _mem_kv(mem, row(mem_norm_g), w_ckv.astype(BF16), row(xattn_knorm_g))
    fq, fk, fvt, gq, gk, gv, z, scol, kx, qx = _in_proj(
        x2d, row(norm_mix_g), w_all, qg, kg, gdn_conv_w, bias_col, alog_col, seq=S, tm=tm)
    o_fox = _fox_attn(fq, qx, fk, kx, fvt, og, batch=B, seq=S)
    o_gdn = _gdn(gq, gk, gv, z, scol, row(gdn_onorm_g), batch=B, seq=S)
    out = _mix_mlp(x2d, o_fox, o_gdn, w_out.astype(BF16), row(norm_xattn_g), w_cq.astype(BF16),
                   row(xattn_qnorm_g) * (XATTN_HEAD_DIM ** -0.5), ck, cv, w_co.astype(BF16),
                   row(norm_mlp_g), w_mlp1.astype(BF16), w_mlp2.astype(BF16), seq=S, tm=tm, ff_block=1024)
    return out.reshape(B, S, D)


def kernel(x, mem, norm_mix_g, w_in, fox_qnorm_g, fox_knorm_g, fox_f_bias, fox_onorm_g, gdn_conv_w, gdn_A_log, gdn_dt_bias, gdn_onorm_g, w_out, norm_xattn_g, mem_norm_g, w_cq, w_ckv, xattn_qnorm_g, xattn_knorm_g, w_co, norm_mlp_g, w_mlp1, w_mlp2):
    for l in range(w_in.shape[0]):
        x = _layer(x, mem, norm_mix_g[l], w_in[l], fox_qnorm_g[l], fox_knorm_g[l], fox_f_bias[l],
                   fox_onorm_g[l], gdn_conv_w[l], gdn_A_log[l], gdn_dt_bias[l], gdn_onorm_g[l],
                   w_out[l], norm_xattn_g[l], mem_norm_g[l], w_cq[l], w_ckv[l], xattn_qnorm_g[l],
                   xattn_knorm_g[l], w_co[l], norm_mlp_g[l], w_mlp1[l], w_mlp2[l])
    return x
```

```python
import functools

import jax
import jax.numpy as jnp
from jax import lax
from jax.experimental import pallas as pl
from jax.experimental.pallas import tpu as pltpu

EPS = 1e-6
NEG_INF = -1e30
LOG2E = 1.4426950408889634

FOX_HEADS = 8
FOX_HEAD_DIM = 64
FOX_WIDTH = FOX_HEADS * FOX_HEAD_DIM
FOX_BLOCK = 256
FOX_QBLOCKS_PER_STEP = 4
GDN_HEADS = 4
GDN_HEAD_DIM = 128
GDN_WIDTH = GDN_HEADS * GDN_HEAD_DIM
CONV_WIDTH = 4
GDN_CHUNK = 64
GDN_GROUP = 256
XATTN_HEADS = 4
XATTN_HEAD_DIM = 128
XATTN_WIDTH = XATTN_HEADS * XATTN_HEAD_DIM

LANES = 128
SUBLANES = 8
SMALL_ROWS = 24
BETA_ROW = FOX_HEADS
DECAY_ROW = BETA_ROW + GDN_HEADS
REM_ROW = DECAY_ROW + GDN_HEADS
VMEM_LIMIT_BYTES = 56 * 1024 * 1024

F32 = jnp.float32
BF16 = jnp.bfloat16


def _bdot(a, b):
    return jnp.dot(a.astype(BF16), b.astype(BF16), preferred_element_type=F32)


def _bdot_nt(a, b):
    return lax.dot_general(a.astype(BF16), b.astype(BF16), (((1,), (1,)), ((), ())),
                           preferred_element_type=F32)


def _rms(x, gain):
    return x * lax.rsqrt(jnp.mean(x * x, axis=-1, keepdims=True) + EPS) * gain


def _half_lane_rms(x, gain, lane_lo):
    ss = x * x
    s_lo = jnp.sum(jnp.where(lane_lo, ss, 0.0), axis=-1, keepdims=True)
    s_hi = jnp.sum(jnp.where(lane_lo, 0.0, ss), axis=-1, keepdims=True)
    ms = jnp.where(lane_lo, s_lo, s_hi) * (1.0 / FOX_HEAD_DIM)
    return x * lax.rsqrt(ms + EPS) * gain


def _softplus(y):
    return jnp.maximum(y, 0.0) + jnp.log1p(jnp.exp(-jnp.abs(y)))


def _silu(y):
    return y / (1.0 + jnp.exp2(y * (-LOG2E)))


def _split3_bf16(v):
    p1 = v.astype(BF16)
    r1 = v - p1.astype(F32)
    p2 = r1.astype(BF16)
    r2 = r1 - p2.astype(F32)
    return p1, p2, r2.astype(BF16)


def _const_spec(shape):
    return pl.BlockSpec(shape, lambda *_: (0,) * len(shape), pipeline_mode=pl.Buffered(1))


def _params(n_grid):
    return pltpu.CompilerParams(dimension_semantics=("arbitrary",) * n_grid,
                                vmem_limit_bytes=VMEM_LIMIT_BYTES)


def _mem_kv_kernel(mem_ref, g_ref, w_ref, kg_ref, ck_ref, cv_ref):
    hm = _rms(mem_ref[0], g_ref[...])
    kv = _bdot(hm, w_ref[...])
    for h in range(XATTN_HEADS):
        sl = slice(h * XATTN_HEAD_DIM, (h + 1) * XATTN_HEAD_DIM)
        ck_ref[0, :, sl] = _rms(kv[:, sl], kg_ref[...]).astype(BF16)
    cv_ref[0] = kv[:, XATTN_WIDTH:].astype(BF16)


def _mem_kv(mem, g, w_ckv, kg):
    B, M, D = mem.shape
    return pl.pallas_call(
        _mem_kv_kernel,
        grid=(B,),
        in_specs=[pl.BlockSpec((1, M, D), lambda b: (b, 0, 0)),
                  _const_spec((1, D)), _const_spec(w_ckv.shape), _const_spec((1, XATTN_HEAD_DIM))],
        out_specs=[pl.BlockSpec((1, M, XATTN_WIDTH), lambda b: (b, 0, 0))] * 2,
        out_shape=[jax.ShapeDtypeStruct((B, M, XATTN_WIDTH), BF16)] * 2,
        compiler_params=_params(1),
        name="mem_kv",
    )(mem, g, w_ckv, kg)


def _in_proj_kernel(x_ref, g_ref, w_ref, qg_ref, kg_ref, convw_ref, bias_ref, alog_ref,
                    fq_ref, fk_ref, fvt_ref, gq_ref, gk_ref, gv_ref, z_ref, scol_ref,
                    gbuf_q, gbuf_k, gbuf_v, carry_ref, *, tiles_per_seq):
    tm = x_ref.shape[0]
    i = pl.program_id(0)
    gbufs = (gbuf_q, gbuf_k, gbuf_v)

    @pl.when(i % tiles_per_seq == 0)
    def _():
        for gbuf in gbufs:
            gbuf[0:SUBLANES, :] = jnp.zeros((SUBLANES, GDN_WIDTH), F32)
        carry_ref[...] = jnp.zeros(carry_ref.shape, F32)

    hb = _rms(x_ref[...], g_ref[...]).astype(BF16)
    lane_lo = lax.broadcasted_iota(jnp.int32, (1, LANES), 1) < FOX_HEAD_DIM
    off_gdn = 3 * FOX_WIDTH
    off_z = off_gdn + 3 * GDN_WIDTH
    off_small = off_z + GDN_WIDTH
    proj = lambda off, width: jnp.dot(hb, w_ref[:, off:off + width], preferred_element_type=F32)

    def gdn_proj(part):
        gbufs[part][SUBLANES:SUBLANES + tm, :] = proj(off_gdn + part * GDN_WIDTH, GDN_WIDTH)

    def gdn_head(part, h):
        gbuf, out_ref = gbufs[part], (gq_ref, gk_ref, gv_ref)[part]
        cs = slice(h * GDN_HEAD_DIM, (h + 1) * GDN_HEAD_DIM)
        ws = slice(part * GDN_WIDTH + h * GDN_HEAD_DIM, part * GDN_WIDTH + (h + 1) * GDN_HEAD_DIM)
        acc = convw_ref[CONV_WIDTH - 1:CONV_WIDTH, ws] * gbuf[SUBLANES:SUBLANES + tm, cs]
        for d in range(1, CONV_WIDTH):
            acc = acc + (convw_ref[CONV_WIDTH - 1 - d:CONV_WIDTH - d, ws]
                         * gbuf[SUBLANES - d:SUBLANES - d + tm, cs])
        y = _silu(acc)
        if part == 0:
            y = y * lax.rsqrt(jnp.sum(y * y, axis=-1, keepdims=True) + EPS) * (GDN_HEAD_DIM ** -0.5)
        elif part == 1:
            y = y * lax.rsqrt(jnp.sum(y * y, axis=-1, keepdims=True) + EPS)
        out_ref[:, cs] = y
        gbuf[0:SUBLANES, cs] = gbuf[tm:tm + SUBLANES, cs]

    MXU_COLS = 2 * LANES

    def fox_qk(out_ref, gain_ref, off, c0):
        y = proj(off + c0, MXU_COLS)
        for j in range(MXU_COLS // LANES):
            sl = slice(c0 + j * LANES, c0 + (j + 1) * LANES)
            out_ref[:, sl] = _half_lane_rms(y[:, j * LANES:(j + 1) * LANES], gain_ref[...],
                                            lane_lo).astype(BF16)

    def fox_v(c0):
        y = proj(2 * FOX_WIDTH + c0, MXU_COLS)
        for r in range(tm // FOX_BLOCK):
            fvt_ref[r, c0:c0 + MXU_COLS, :] = y[r * FOX_BLOCK:(r + 1) * FOX_BLOCK, :].T.astype(BF16)

    def gate(c0):
        z_ref[:, c0:c0 + MXU_COLS] = _silu(proj(off_z + c0, MXU_COLS))

    row = lax.broadcasted_iota(jnp.int32, (SMALL_ROWS, tm), 0)
    is_fox = row < BETA_ROW
    is_beta = row < DECAY_ROW

    def scalars_gates():
        xb = proj(off_small, LANES).T[0:SMALL_ROWS, :] + bias_ref[...]
        sp = _softplus(jnp.where(is_fox, -xb, xb))
        return jnp.where(is_fox, -sp,
                         jnp.where(is_beta, 1.0 / (1.0 + jnp.exp(-xb)), -jnp.exp(alog_ref[...]) * sp))

    def scalars_scans(val):
        ti = lax.broadcasted_iota(jnp.int32, (tm, tm), 0)
        tj = lax.broadcasted_iota(jnp.int32, (tm, tm), 1)
        same_chunk = (ti // GDN_CHUNK) == (tj // GDN_CHUNK)
        one_hot = lambda m: jnp.where(m, 1.0, 0.0).astype(BF16)
        scan_mats = jnp.concatenate([one_hot(ti <= tj), one_hot((ti <= tj) & same_chunk),
                                     one_hot((ti > tj) & same_chunk)], axis=1)
        sums = jnp.zeros((SMALL_ROWS, 3 * tm), F32)
        for part in _split3_bf16(val):
            sums = sums + jnp.dot(part, scan_mats, preferred_element_type=F32)
        cs_full = sums[:, 0:tm] + carry_ref[:, 0:1]
        carry_ref[...] = jnp.broadcast_to(cs_full[:, tm - 1:tm], carry_ref.shape)
        return jnp.where(is_fox, cs_full, jnp.where(is_beta, val, jnp.where(
            row < REM_ROW, sums[:, tm:2 * tm], sums[:, 2 * tm:])))

    def scalars_columns(rows):
        rows = jnp.where(is_fox, rows * LOG2E, rows)
        padded = jnp.concatenate([rows, jnp.zeros((LANES - SMALL_ROWS, tm), F32)], axis=0)
        scol_ref[...] = padded.T

    val = scalars_gates()
    gdn_proj(0)
    gdn_proj(1)
    gdn_proj(2)
    rows = scalars_scans(val)
    light = ([functools.partial(fox_qk, fq_ref, qg_ref, 0, c) for c in range(0, FOX_WIDTH, MXU_COLS)]
             + [functools.partial(fox_qk, fk_ref, kg_ref, FOX_WIDTH, c) for c in range(0, FOX_WIDTH, MXU_COLS)]
             + [functools.partial(scalars_columns, rows)]
             + [functools.partial(fox_v, c) for c in range(0, FOX_WIDTH, MXU_COLS)]
             + [functools.partial(gate, c) for c in range(0, GDN_WIDTH, MXU_COLS)])
    for part in range(3):
        for h in range(GDN_HEADS):
            gdn_head(part, h)
            if light:
                light.pop(0)()
    while light:
        light.pop(0)()


def _in_proj(x2d, g, w_all, qg, kg, conv_w, bias_col, alog_col, *, seq, tm):
    T, D = x2d.shape
    n = T // tm
    row_spec = lambda width: pl.BlockSpec((tm, width), lambda i: (i, 0))
    kernel = functools.partial(_in_proj_kernel, tiles_per_seq=seq // tm)
    return pl.pallas_call(
        kernel,
        grid=(n,),
        in_specs=[row_spec(D), _const_spec((1, D)), _const_spec(w_all.shape),
                  _const_spec((1, LANES)), _const_spec((1, LANES)), _const_spec(conv_w.shape),
                  _const_spec((SMALL_ROWS, 1)), _const_spec((SMALL_ROWS, 1))],
        out_specs=[row_spec(FOX_WIDTH)] * 2
        + [pl.BlockSpec((tm // FOX_BLOCK, FOX_WIDTH, FOX_BLOCK), lambda i: (i, 0, 0))]
        + [row_spec(GDN_WIDTH)] * 4 + [row_spec(LANES)],
        out_shape=[jax.ShapeDtypeStruct((T, FOX_WIDTH), BF16)] * 2
        + [jax.ShapeDtypeStruct((T // FOX_BLOCK, FOX_WIDTH, FOX_BLOCK), BF16)]
        + [jax.ShapeDtypeStruct((T, GDN_WIDTH), F32)] * 4
        + [jax.ShapeDtypeStruct((T, LANES), F32)],
        scratch_shapes=[pltpu.VMEM((SUBLANES + tm, GDN_WIDTH), F32)] * 3
        + [pltpu.VMEM((SMALL_ROWS, LANES), F32)],
        compiler_params=_params(1),
        name="in_proj",
    )(x2d, g, w_all, qg, kg, conv_w, bias_col, alog_col)


def _fox_kernel(q_ref, cq_ref, k_ref, ck_ref, vt_ref, og_ref, o_ref, sa_ref, sb_ref):
    t = FOX_BLOCK
    Dh = FOX_HEAD_DIM
    n_q = q_ref.shape[0] // t
    first_q = pl.program_id(1) * n_q
    lane_lo = lax.broadcasted_iota(jnp.int32, (1, LANES), 1) < Dh
    chains = [(p, hh) for p in range(FOX_HEADS // 2) for hh in range(2)]

    def query_block(rows):
        cq_t = cq_ref[rows, :].T
        out = []
        for h, (p, hh) in enumerate(chains):
            q = q_ref[rows, p * LANES:(p + 1) * LANES]
            q = jnp.where(lane_lo if hh == 0 else jnp.logical_not(lane_lo), q, jnp.zeros_like(q))
            out.append((q, cq_t[h:h + 1, :]))
        return out

    q_blocks = [query_block(slice(n * t, (n + 1) * t)) for n in range(n_q)]
    n_chains = len(chains)
    lookahead = 2

    def stage(j_new, keep_all, new_ref, q_new, j_old, old_ref, q_old, maxima, carry):
        if j_new is not None:
            r0 = pl.multiple_of(j_new * t, t)
            ck = ck_ref[pl.ds(r0, t), :]
            if keep_all is not True:
                visible = (lax.broadcasted_iota(jnp.int32, (t, t), 0)
                           <= lax.broadcasted_iota(jnp.int32, (t, t), 1))
                if keep_all is not False:
                    visible = visible | keep_all
        new_maxima = [None] * n_chains
        out = [None] * n_chains

        def score(i):
            p = chains[i][0]
            s = _bdot_nt(k_ref[pl.ds(r0, t), p * LANES:(p + 1) * LANES], q_new[i][0])
            s = s - ck[:, i:i + 1]
            if keep_all is not True:
                s = jnp.where(visible, s, NEG_INF)
            new_ref[i] = s
            new_maxima[i] = jnp.max(s, axis=0, keepdims=True)

        def fold(i):
            m, l, acc = carry[i]
            cq = q_old[i][1]
            m_new = jnp.maximum(m, maxima[i] + cq)
            alpha = jnp.exp2(m - m_new)
            prob = jnp.exp2(old_ref[i] + (cq - m_new))
            pv = jnp.dot(vt_ref[j_old, i * Dh:(i + 1) * Dh, :], prob.astype(BF16), preferred_element_type=F32)
            out[i] = (m_new, alpha * l + jnp.sum(prob, axis=0, keepdims=True), alpha * acc + pv)

        if j_new is not None:
            for i in range(min(lookahead, n_chains)):
                score(i)
        for i in range(n_chains):
            if j_old is not None:
                fold(i)
            if j_new is not None and i + lookahead < n_chains:
                score(i + lookahead)
        return (tuple(new_maxima) if j_new is not None else None,
                tuple(out) if j_old is not None else carry)

    def finish(rows, carry):
        for p in range(FOX_HEADS // 2):
            (_, l0, a0), (_, l1, a1) = carry[2 * p], carry[2 * p + 1]
            o = jnp.concatenate([a0 / l0, a1 / l1], axis=0).T
            o_ref[rows, p * LANES:(p + 1) * LANES] = _half_lane_rms(o, og_ref[...], lane_lo).astype(BF16)

    init = tuple((jnp.full((1, t), NEG_INF, F32), jnp.zeros((1, t), F32), jnp.zeros((Dh, t), F32))
                 for _ in chains)
    here, other = sa_ref, sb_ref
    maxima, _ = stage(0, first_q > 0, here, q_blocks[0], None, None, None, None, init)
    for n in range(n_q):
        m = first_q + n
        qb = q_blocks[n]
        half = m // 2

        def two_blocks(i, state, here=here, other=other, qb=qb, half=half, last_is_diagonal=(n % 2 == 0)):
            mx, carry = stage(2 * i + 1, True, other, qb, 2 * i, here, qb, *state)
            keep_all = (i + 1 < half) if last_is_diagonal else True
            return stage(2 * i + 2, keep_all, here, qb, 2 * i + 1, other, qb, mx, carry)

        maxima, carry = lax.fori_loop(0, half, two_blocks, (maxima, init))
        if n % 2 == 1:
            maxima, carry = stage(m, False, other, qb, m - 1, here, qb, maxima, carry)
            here, other = other, here
        if n + 1 < n_q:
            maxima, carry = stage(0, True, other, q_blocks[n + 1], m, here, qb, maxima, carry)
            here, other = other, here
        else:
            _, carry = stage(None, None, None, None, m, here, qb, maxima, carry)
        finish(slice(n * t, (n + 1) * t), carry)


def _fox_attn(fq, fk, fvt, scol, og, *, batch, seq):
    T = fq.shape[0]
    t = FOX_BLOCK
    nq = seq // t
    rows = min(nq, FOX_QBLOCKS_PER_STEP) * t
    steps = seq // rows
    q_spec = lambda w: pl.BlockSpec((rows, w), lambda b, s: (b * steps + s, 0))
    return pl.pallas_call(
        _fox_kernel,
        grid=(batch, steps),
        in_specs=[q_spec(FOX_WIDTH), q_spec(LANES),
                  pl.BlockSpec((seq, FOX_WIDTH), lambda b, s: (b, 0)),
                  pl.BlockSpec((seq, LANES), lambda b, s: (b, 0)),
                  pl.BlockSpec((nq, FOX_WIDTH, t), lambda b, s: (b, 0, 0)),
                  _const_spec((1, LANES))],
        out_specs=q_spec(FOX_WIDTH),
        out_shape=jax.ShapeDtypeStruct((T, FOX_WIDTH), BF16),
        scratch_shapes=[pltpu.VMEM((FOX_HEADS, t, t), F32)] * 2,
        compiler_params=_params(2),
        name="fox_attn",
    )(fq, scol, fk, scol, fvt, og)


def _gdn_kernel(q_ref, k_ref, v_ref, scol_ref, z_ref, og_ref, o_ref,
                state_ref, u_ref, w_ref, qe_ref, intra_ref, kdt_ref, sm_ref, *, groups_per_seq):
    C = GDN_CHUNK
    G = q_ref.shape[0]
    Dh = GDN_HEAD_DIM
    t = pl.program_id(0)
    heads = range(GDN_HEADS)
    col = lambda h: slice(h * Dh, (h + 1) * Dh)

    @pl.when(t == 0)
    def _():
        for ref in (u_ref, w_ref, qe_ref, intra_ref, kdt_ref, sm_ref):
            ref[...] = jnp.zeros(ref.shape, ref.dtype)

    @pl.when((t == 0) | ((t - 1) % groups_per_seq == 0))
    def _():
        state_ref[...] = jnp.zeros(state_ref.shape, F32)

    cur = t % 2
    prev = 1 - cur
    us = [u_ref[prev, :, col(h)] for h in heads]
    ws16 = [w_ref[prev, :, col(h)] for h in heads]
    qes_prev = [qe_ref[prev, :, col(h)] for h in heads]
    intras_prev = [intra_ref[prev, :, col(h)] for h in heads]
    kd_ts_prev = [kdt_ref[prev, h] for h in heads]
    sm_prev = sm_ref[prev]
    states = [state_ref[h] for h in heads]
    outs = [[] for _ in heads]
    pending = {}

    def recur_first(c):
        rs = slice(c * C, (c + 1) * C)
        for h in heads:
            ws = _bdot(jnp.concatenate([ws16[h][rs], qes_prev[h][rs]], axis=0), states[h])
            pending[h] = (ws[C:2 * C], (us[h][rs] - ws[0:C]).astype(BF16))

    def recur_second(c):
        rs = slice(c * C, (c + 1) * C)
        pair = (c * C) // LANES
        for h in heads:
            qs, vb = pending[h]
            zero = jnp.zeros_like(vb)
            outs[h].append(qs + jnp.dot(intras_prev[h][rs], jnp.concatenate([vb, vb], axis=0),
                                        preferred_element_type=F32))
            vpad = jnp.concatenate([vb, zero] if (c * C) % LANES == 0 else [zero, vb], axis=0)
            g_last = sm_prev[(c + 1) * C - 1:(c + 1) * C, DECAY_ROW + h:DECAY_ROW + h + 1]
            states[h] = states[h] * jnp.exp(g_last) + jnp.dot(
                kd_ts_prev[h][:, pair * LANES:(pair + 1) * LANES], vpad, preferred_element_type=F32)

    recur = [f for c in range(G // C) for f in (functools.partial(recur_first, c),
                                                functools.partial(recur_second, c))]

    ri = lax.broadcasted_iota(jnp.int32, (G, G), 0)
    ci = lax.broadcasted_iota(jnp.int32, (G, G), 1)
    same_chunk = (ri // C) == (ci // C)
    lower = same_chunk & (ri >= ci)
    strict = same_chunk & (ri > ci)
    sm = scol_ref[...]
    sm_t = sm.T
    lmats, intras, ys, pws, qes, kd_ts = ([None] * GDN_HEADS for _ in range(6))

    def prepare(h):
        q = q_ref[:, col(h)]
        k = k_ref[:, col(h)]
        beta = sm[:, BETA_ROW + h:BETA_ROW + h + 1]
        gc = sm[:, DECAY_ROW + h:DECAY_ROW + h + 1]
        rem = sm[:, REM_ROW + h:REM_ROW + h + 1]
        gc_row = sm_t[DECAY_ROW + h:DECAY_ROW + h + 1, :]
        eg = jnp.exp(gc)
        decay = jnp.exp(jnp.where(lower, gc - gc_row, NEG_INF))
        kb = k * beta
        a = _bdot_nt(jnp.concatenate([kb, q], axis=0), k)
        lmats[h] = jnp.where(strict, a[0:G] * decay, 0.0).astype(BF16)
        intra = (a[G:2 * G] * decay).astype(BF16)
        intra_fold = intra[:, 0:LANES]
        for j in range(1, G // LANES):
            intra_fold = intra_fold + intra[:, j * LANES:(j + 1) * LANES]
        intras[h] = intra_fold
        ys[h] = jnp.concatenate([v_ref[:, col(h)] * beta, kb * eg], axis=1)
        qes[h] = (q * eg).astype(BF16)
        kd_ts[h] = (k * jnp.exp(rem)).T.astype(BF16)

    def solve_first(h):
        ys[h] = ys[h] - _bdot(lmats[h], ys[h])
        pws[h] = _bdot(lmats[h], lmats[h]).astype(BF16)

    def solve_step(step, h):
        pw = pws[h]
        if step < 4:
            pws[h] = _bdot(pw, pw).astype(BF16)
        ys[h] = ys[h] + _bdot(pw, ys[h])

    for h in heads:
        recur.pop(0)()
        prepare(h)
    for h in heads:
        solve_first(h)
    recur.pop(0)()
    for step in range(5):
        for h in heads:
            solve_step(step, h)
        if recur:
            recur.pop(0)()
    while recur:
        recur.pop(0)()

    for h in heads:
        state_ref[h] = states[h]
        o = jnp.concatenate(outs[h], axis=0)
        o_ref[:, col(h)] = (_rms(o, og_ref[...]) * z_ref[:, col(h)]).astype(BF16)
        u_ref[cur, :, col(h)] = ys[h][:, 0:Dh]
        w_ref[cur, :, col(h)] = ys[h][:, Dh:2 * Dh].astype(BF16)
        qe_ref[cur, :, col(h)] = qes[h]
        intra_ref[cur, :, col(h)] = intras[h]
        kdt_ref[cur, h] = kd_ts[h]
    sm_ref[cur] = sm


def _gdn(gq, gk, gv, z, scol, og, *, batch, seq):
    T = gq.shape[0]
    G = min(seq, GDN_GROUP)
    ng = seq // G
    n = batch * ng
    cur_spec = lambda width: pl.BlockSpec((G, width), lambda t: (jnp.minimum(t, n - 1), 0))
    prev_spec = lambda width: pl.BlockSpec((G, width), lambda t: (jnp.maximum(t - 1, 0), 0))
    return pl.pallas_call(
        functools.partial(_gdn_kernel, groups_per_seq=ng),
        grid=(n + 1,),
        in_specs=[cur_spec(GDN_WIDTH)] * 3 + [cur_spec(LANES), prev_spec(GDN_WIDTH),
                                              _const_spec((1, GDN_HEAD_DIM))],
        out_specs=prev_spec(GDN_WIDTH),
        out_shape=jax.ShapeDtypeStruct((T, GDN_WIDTH), BF16),
        scratch_shapes=[pltpu.VMEM((GDN_HEADS, GDN_HEAD_DIM, GDN_HEAD_DIM), F32),
                        pltpu.VMEM((2, G, GDN_WIDTH), F32), pltpu.VMEM((2, G, GDN_WIDTH), BF16),
                        pltpu.VMEM((2, G, GDN_WIDTH), BF16), pltpu.VMEM((2, G, GDN_WIDTH), BF16),
                        pltpu.VMEM((2, GDN_HEADS, GDN_HEAD_DIM, G), BF16), pltpu.VMEM((2, G, LANES), F32)],
        compiler_params=_params(1),
        name="gdn",
    )(gq, gk, gv, scol, z, og)


def _mix_mlp_kernel(x_ref, oa_ref, ob_ref, wout_ref, gx_ref, wcq_ref, cqg_ref, ck_ref, cv_ref,
                    wco_ref, gm_ref, w1_ref, w2_ref, o_ref, *, ff_block):
    o = jnp.concatenate([oa_ref[...], ob_ref[...]], axis=-1)
    x1 = x_ref[...] + jnp.dot(o, wout_ref[...], preferred_element_type=F32)
    hq = _rms(x1, gx_ref[...])
    cq = _bdot(hq, wcq_ref[...])
    heads = []
    for h in range(XATTN_HEADS):
        sl = slice(h * XATTN_HEAD_DIM, (h + 1) * XATTN_HEAD_DIM)
        s = _bdot_nt(_rms(cq[:, sl], cqg_ref[...]), ck_ref[0, :, sl])
        p = jnp.exp(s - jnp.max(s, axis=-1, keepdims=True))
        pv = jnp.dot(p.astype(BF16), cv_ref[0, :, sl], preferred_element_type=F32)
        heads.append(pv / jnp.sum(p, axis=-1, keepdims=True))
    co = jnp.concatenate(heads, axis=-1)
    x2 = x1 + _bdot(co, wco_ref[...])
    hb = _rms(x2, gm_ref[...]).astype(BF16)
    acc = x2
    for c in range(w1_ref.shape[1] // ff_block):
        sl = slice(c * ff_block, (c + 1) * ff_block)
        a = jnp.maximum(jnp.dot(hb, w1_ref[:, sl], preferred_element_type=F32), 0.0)
        acc = acc + jnp.dot((a * a).astype(BF16), w2_ref[sl, :], preferred_element_type=F32)
    o_ref[...] = acc


def _mix_mlp(x2d, oa, ob, w_out, gx, w_cq, cqg, ck, cv, w_co, gm, w1, w2, *, seq, tm, ff_block):
    T, D = x2d.shape
    M = ck.shape[1]
    tiles_per_seq = seq // tm
    row_spec = lambda width: pl.BlockSpec((tm, width), lambda i: (i, 0))
    mem_spec = pl.BlockSpec((1, M, XATTN_WIDTH), lambda i: (i // tiles_per_seq, 0, 0))
    return pl.pallas_call(
        functools.partial(_mix_mlp_kernel, ff_block=ff_block),
        grid=(T // tm,),
        in_specs=[row_spec(D), row_spec(FOX_WIDTH), row_spec(GDN_WIDTH), _const_spec(w_out.shape),
                  _const_spec((1, D)), _const_spec(w_cq.shape), _const_spec((1, XATTN_HEAD_DIM)),
                  mem_spec, mem_spec, _const_spec(w_co.shape),
                  _const_spec((1, D)), _const_spec(w1.shape), _const_spec(w2.shape)],
        out_specs=row_spec(D),
        out_shape=jax.ShapeDtypeStruct((T, D), F32),
        compiler_params=_params(1),
        name="mix_mlp",
    )(x2d, oa, ob, w_out, gx, w_cq, cqg, ck, cv, w_co, gm, w1, w2)


def _row_tile(seq):
    return min(seq, 512)


def _layer(x, mem, norm_mix_g, w_in, fox_qnorm_g, fox_knorm_g, fox_f_bias, fox_onorm_g,
           gdn_conv_w, gdn_A_log, gdn_dt_bias, gdn_onorm_g, w_out,
           norm_xattn_g, mem_norm_g, w_cq, w_ckv, xattn_qnorm_g, xattn_knorm_g, w_co,
           norm_mlp_g, w_mlp1, w_mlp2):
    B, S, D = x.shape
    T = B * S
    tm = _row_tile(S)
    row = lambda v: v.reshape(1, -1).astype(F32)

    o_f = 3 * FOX_WIDTH
    o_g = o_f + FOX_HEADS
    o_b = o_g + 3 * GDN_WIDTH
    o_a = o_b + GDN_HEADS
    o_z = o_a + GDN_HEADS
    w_small = jnp.concatenate([w_in[:, o_f:o_g], w_in[:, o_b:o_z], w_in[:, o_a:o_z]], axis=1)
    w_small = jnp.pad(w_small, ((0, 0), (0, LANES - w_small.shape[1])))
    w_all = jnp.concatenate([w_in[:, :o_f], w_in[:, o_g:o_b], w_in[:, o_z:], w_small], axis=1).astype(BF16)
    pad4 = jnp.zeros((GDN_HEADS,), F32)
    bias_col = jnp.concatenate([fox_f_bias, pad4, gdn_dt_bias, gdn_dt_bias, pad4]).reshape(-1, 1)
    alog_col = jnp.concatenate([jnp.zeros((DECAY_ROW,), F32), gdn_A_log, gdn_A_log, pad4]).reshape(-1, 1)
    qg = row(jnp.tile(fox_qnorm_g, 2)) * (FOX_HEAD_DIM ** -0.5 * LOG2E)
    kg = row(jnp.tile(fox_knorm_g, 2))
    og = row(jnp.tile(fox_onorm_g, 2))

    x2d = x.reshape(T, D)
    ck, cv = _mem_kv(mem, row(mem_norm_g), w_ckv.astype(BF16), row(xattn_knorm_g))
    fq, fk, fvt, gq, gk, gv, z, scol = _in_proj(
        x2d, row(norm_mix_g), w_all, qg, kg, gdn_conv_w, bias_col, alog_col, seq=S, tm=tm)
    o_fox = _fox_attn(fq, fk, fvt, scol, og, batch=B, seq=S)
    o_gdn = _gdn(gq, gk, gv, z, scol, row(gdn_onorm_g), batch=B, seq=S)
    out = _mix_mlp(x2d, o_fox, o_gdn, w_out.astype(BF16), row(norm_xattn_g), w_cq.astype(BF16),
                   row(xattn_qnorm_g) * (XATTN_HEAD_DIM ** -0.5), ck, cv, w_co.astype(BF16),
                   row(norm_mlp_g), w_mlp1.astype(BF16), w_mlp2.astype(BF16), seq=S, tm=tm, ff_block=1024)
    return out.reshape(B, S, D)


def kernel(x, mem, norm_mix_g, w_in, fox_qnorm_g, fox_knorm_g, fox_f_bias, fox_onorm_g, gdn_conv_w, gdn_A_log, gdn_dt_bias, gdn_onorm_g, w_out, norm_xattn_g, mem_norm_g, w_cq, w_ckv, xattn_qnorm_g, xattn_knorm_g, w_co, norm_mlp_g, w_mlp1, w_mlp2):
    for l in range(w_in.shape[0]):
        x = _layer(x, mem, norm_mix_g[l], w_in[l], fox_qnorm_g[l], fox_knorm_g[l], fox_f_bias[l],
                   fox_onorm_g[l], gdn_conv_w[l], gdn_A_log[l], gdn_dt_bias[l], gdn_onorm_g[l],
                   w_out[l], norm_xattn_g[l], mem_norm_g[l], w_cq[l], w_ckv[l], xattn_qnorm_g[l],
                   xattn_knorm_g[l], w_co[l], norm_mlp_g[l], w_mlp1[l], w_mlp2[l])
    return x
```
